```python
import math
import jax, jax.numpy as jnp
from jax import lax
import numpy as np

D_MODEL = 2048
BATCH = 2
SEQ = 8192
DEPTH = 1

ATTN_HEADS = 8
ATTN_QK_DIM = 64
ATTN_V_DIM = 128
HGRN_HEADS = 8
HGRN_K_DIM = 128
HGRN_V_DIM = 128
ATTN_WIDTH = ATTN_HEADS * ATTN_V_DIM
HGRN_WIDTH = HGRN_HEADS * HGRN_V_DIM
MIX_WIDTH = ATTN_WIDTH + HGRN_WIDTH
IN_SIZES = (ATTN_HEADS * 2 * ATTN_QK_DIM, ATTN_HEADS * 2 * ATTN_QK_DIM, ATTN_WIDTH,
            HGRN_HEADS * HGRN_K_DIM, HGRN_WIDTH, HGRN_HEADS * HGRN_K_DIM,
            HGRN_HEADS * HGRN_K_DIM, HGRN_WIDTH)
IN_WIDTH = sum(IN_SIZES)
Q_BLOCK = 128
HGRN_CHUNK = 64
REL_BUCKETS = 32
REL_MAX_DIST = 128
N_EXPERTS = 256
TOP_K = 8
N_EXPERT_GROUPS = 8
TOPK_EXPERT_GROUPS = 4
EXPERT_DIM = 512
ROUTED_SCALE = 2.5
MOE_BLOCK = 128
ADA_SCALE = 0.2
EPS = 1e-6

kernel_name = 'hybrid_diffattn_hgrn2_moe_encoder'


def _rmsnorm(x, g):
    xf = x.astype(jnp.float32)
    y = xf * lax.rsqrt(jnp.mean(xf * xf, axis=-1, keepdims=True) + EPS)
    return (y * g.astype(jnp.float32)).astype(x.dtype)


def _t5_bucket(rel):
    half = REL_BUCKETS // 2
    max_exact = half // 2
    ret = jnp.where(rel > 0, half, 0)
    n = jnp.abs(rel)
    nf = jnp.maximum(n, 1).astype(jnp.float32)
    large = max_exact + (jnp.log(nf / max_exact) / math.log(REL_MAX_DIST / max_exact)
                         * (half - max_exact)).astype(jnp.int32)
    large = jnp.minimum(large, half - 1)
    return ret + jnp.where(n < max_exact, n, large)


def _diff_attention(q1, q2, k1, k2, v, pos, rel_bias, lam):
    B, H, S, _ = q1.shape
    nq = S // Q_BLOCK
    scale = ATTN_QK_DIM ** -0.5

    def blocks(t):
        return jnp.moveaxis(t.reshape(B, H, nq, Q_BLOCK, t.shape[-1]), 2, 0)

    pos_blocks = jnp.moveaxis(pos.reshape(B, nq, Q_BLOCK), 1, 0)

    def one_block(args):
        a1, a2, pq = args
        rel = pos[:, None, :] - pq[:, :, None]
        bias = jnp.moveaxis(rel_bias[_t5_bucket(rel)], -1, 1).astype(jnp.float32)
        s1 = jnp.einsum('bhqd,bhkd->bhqk', a1, k1).astype(jnp.float32) * scale + bias
        s2 = jnp.einsum('bhqd,bhkd->bhqk', a2, k2).astype(jnp.float32) * scale + bias
        p = jax.nn.softmax(s1, axis=-1) - lam * jax.nn.softmax(s2, axis=-1)
        return jnp.einsum('bhqk,bhkd->bhqd', p.astype(v.dtype), v)

    out = lax.map(one_block, (blocks(q1), blocks(q2), pos_blocks))
    return jnp.moveaxis(out, 0, 2).reshape(B, H, S, v.shape[-1])


def _hgrn2_scan(q, k, v, logf):
    B, H, S, dk = q.shape
    dv = v.shape[-1]
    nc = S // HGRN_CHUNK

    def chunks(t):
        return jnp.moveaxis(t.reshape(B, H, nc, HGRN_CHUNK, t.shape[-1]), 2, 0)

    mask = jnp.tril(jnp.ones((HGRN_CHUNK, HGRN_CHUNK), bool))

    def step(state, inp):
        qc, kc, vc, gc = inp
        b = jnp.cumsum(gc, axis=2)
        o_inter = jnp.einsum('bhtk,bhkv->bhtv', qc * jnp.exp(b), state)
        diff = b[:, :, :, None, :] - b[:, :, None, :, :]
        decay = jnp.exp(jnp.where(mask[:, :, None], diff, -jnp.inf))
        scores = jnp.einsum('bhtk,bhsk,bhtsk->bhts', qc, kc, decay)
        o_intra = jnp.einsum('bhts,bhsv->bhtv', scores, vc)
        b_last = b[:, :, -1:, :]
        state = (jnp.exp(b_last[:, :, 0, :, None]) * state
                 + jnp.einsum('bhsk,bhsv->bhkv', kc * jnp.exp(b_last - b), vc))
        return state, o_inter + o_intra

    s0 = jnp.zeros((B, H, dk, dv), q.dtype)
    _, o = lax.scan(step, s0, (chunks(q), chunks(k), chunks(v), chunks(logf)))
    return jnp.moveaxis(o, 0, 2).reshape(B, H, S, dv)


def _hgrn2_mixer(q, i, z_fwd, z_bwd, g, lb, g_norm):
    B, S = q.shape[:2]

    def heads(t, d):
        return jnp.transpose(t.reshape(B, S, HGRN_HEADS, d).astype(jnp.float32), (0, 2, 1, 3))

    qh = jax.nn.silu(heads(q, HGRN_K_DIM))
    vh = heads(i, HGRN_V_DIM)

    def gates(z, lb_dir):
        lbh = lb_dir.astype(jnp.float32).reshape(HGRN_HEADS, 1, HGRN_K_DIM)
        f = lbh + (1.0 - lbh) * jax.nn.sigmoid(heads(z, HGRN_K_DIM))
        return 1.0 - f, jnp.log(f)

    k_f, logf_f = gates(z_fwd, lb[0])
    k_b, logf_b = gates(z_bwd, lb[1])
    o_f = _hgrn2_scan(qh, k_f, vh, logf_f)
    flip = lambda t: jnp.flip(t, axis=2)
    o_b = flip(_hgrn2_scan(flip(qh), flip(k_b), flip(vh), flip(logf_b)))
    o = jnp.transpose(o_f + o_b, (0, 2, 1, 3))
    o = _rmsnorm(o, g_norm) * jax.nn.silu(g.reshape(B, S, HGRN_HEADS, HGRN_V_DIM).astype(jnp.float32))
    return o.reshape(B, S, HGRN_WIDTH)


def _swiglu(x, wg, wu, wd):
    return (jax.nn.silu(x @ wg) * (x @ wu)) @ wd


def _moe(h, w_router, router_bias, w_gate, w_up, w_down, ws_gate, ws_up, ws_down):
    B, S, D = h.shape
    N = B * S
    hf = h.reshape(N, D)
    scores = jax.nn.sigmoid((hf @ w_router).astype(jnp.float32))
    biased = scores + router_bias.astype(jnp.float32)
    per_group = N_EXPERTS // N_EXPERT_GROUPS
    group_score = lax.top_k(biased.reshape(N, N_EXPERT_GROUPS, per_group), 2)[0].sum(-1)
    _, gidx = lax.top_k(group_score, TOPK_EXPERT_GROUPS)
    gmask = jnp.any(gidx[:, :, None] == jnp.arange(N_EXPERT_GROUPS)[None, None, :], axis=1)
    emask = jnp.repeat(gmask, per_group, axis=1)
    _, eidx = lax.top_k(jnp.where(emask, biased, -jnp.inf), TOP_K)
    gate = jnp.take_along_axis(scores, eidx, axis=1)
    gate = gate / jnp.sum(gate, axis=-1, keepdims=True) * ROUTED_SCALE

    flat_e = eidx.reshape(-1).astype(jnp.int32)
    flat_tok = jnp.repeat(jnp.arange(N, dtype=jnp.int32), TOP_K)
    flat_w = gate.reshape(-1)
    order = jnp.argsort(flat_e)
    se = flat_e[order]
    counts = jnp.bincount(flat_e, length=N_EXPERTS)
    starts = jnp.cumsum(counts) - counts
    padded = (counts + MOE_BLOCK - 1) // MOE_BLOCK * MOE_BLOCK
    pends = jnp.cumsum(padded)
    pstarts = pends - padded
    dest = pstarts[se] + jnp.arange(N * TOP_K, dtype=pstarts.dtype) - starts[se]
    n_rows = N * TOP_K + N_EXPERTS * MOE_BLOCK
    n_blocks = n_rows // MOE_BLOCK
    row_tok = jnp.zeros((n_rows,), jnp.int32).at[dest].set(flat_tok[order])
    row_w = jnp.zeros((n_rows,), jnp.float32).at[dest].set(flat_w[order])
    block_start = jnp.arange(n_blocks, dtype=pends.dtype) * MOE_BLOCK
    block_e = jnp.minimum(jnp.searchsorted(pends, block_start, side='right'), N_EXPERTS - 1)

    def body(acc, blk):
        toks, wts, e = blk
        y = _swiglu(hf[toks], w_gate[e], w_up[e], w_down[e])
        return acc.at[toks].add(y * wts[:, None].astype(y.dtype)), None

    routed, _ = lax.scan(body, jnp.zeros_like(hf),
                         (row_tok.reshape(n_blocks, MOE_BLOCK), row_w.reshape(n_blocks, MOE_BLOCK), block_e))
    out = routed + _swiglu(hf, ws_gate, ws_up, ws_down)
    return out.reshape(B, S, D)


def setup_inputs(seed: int = 0) -> dict:
    key = jax.random.key(seed)
    ks = jax.random.split(key, 32)
    f32 = jnp.float32
    L, D, E, F = DEPTH, D_MODEL, N_EXPERTS, EXPERT_DIM

    def nrm(k, shape, scale):
        return jax.random.normal(k, shape, f32) * scale

    def gain(k, shape):
        return 1.0 + 0.02 * jax.random.normal(k, shape, f32)

    offs = jax.random.randint(ks[2], (BATCH, 1), 0, 4096, dtype=jnp.int32)
    positions = (jnp.arange(SEQ, dtype=jnp.int32)[None, :] + offs).astype(jnp.int32)
    return {
        'x': nrm(ks[0], (BATCH, SEQ, D), 1.0),
        'c': nrm(ks[1], (BATCH, D), 1.0),
        'positions': positions,
        'rel_bias': nrm(ks[3], (REL_BUCKETS, ATTN_HEADS), 0.2),
        'hgrn_lb_logits': nrm(ks[4], (2, DEPTH + 1, HGRN_HEADS * HGRN_K_DIM), 0.5),
        'w_ada': nrm(ks[5], (L, D, 6 * D), ADA_SCALE * D ** -0.5),
        'b_ada': nrm(ks[6], (L, 6 * D), 0.02),
        'g_mix': gain(ks[7], (L, D)),
        'w_in': nrm(ks[8], (L, D, IN_WIDTH), D ** -0.5),
        'g_q': gain(ks[9], (L, ATTN_QK_DIM)),
        'g_k': gain(ks[10], (L, ATTN_QK_DIM)),
        'lam_q1': nrm(ks[11], (L, ATTN_QK_DIM), 0.1),
        'lam_k1': nrm(ks[12], (L, ATTN_QK_DIM), 0.1),
        'lam_q2': nrm(ks[13], (L, ATTN_QK_DIM), 0.1),
        'lam_k2': nrm(ks[14], (L, ATTN_QK_DIM), 0.1),
        'g_sub': gain(ks[15], (L, ATTN_V_DIM)),
        'g_hgrn': gain(ks[16], (L, HGRN_V_DIM)),
        'w_out': nrm(ks[17], (L, MIX_WIDTH, D), MIX_WIDTH ** -0.5),
        'g_ffn': gain(ks[18], (L, D)),
        'w_router': nrm(ks[19], (L, D, E), D ** -0.5),
        'router_bias': nrm(ks[20], (L, E), 0.01),
        'w_exp_gate': nrm(ks[21], (L, E, D, F), D ** -0.5),
        'w_exp_up': nrm(ks[22], (L, E, D, F), D ** -0.5),
        'w_exp_down': nrm(ks[23], (L, E, F, D), F ** -0.5),
        'w_sh_gate': nrm(ks[24], (L, D, F), D ** -0.5),
        'w_sh_up': nrm(ks[25], (L, D, F), D ** -0.5),
        'w_sh_down': nrm(ks[26], (L, F, D), F ** -0.5),
    }


def reference(x, c, positions, rel_bias, hgrn_lb_logits, w_ada, b_ada, g_mix, w_in, g_q, g_k,
              lam_q1, lam_k1, lam_q2, lam_k2, g_sub, g_hgrn, w_out, g_ffn, w_router, router_bias,
              w_exp_gate, w_exp_up, w_exp_down, w_sh_gate, w_sh_up, w_sh_down):
    B, S, D = x.shape
    lbs = jnp.cumsum(jax.nn.softmax(hgrn_lb_logits.astype(jnp.float32), axis=1), axis=1)
    split_pts = np.cumsum(IN_SIZES)[:-1].tolist()
    for l in range(DEPTH):
        lam_init = 0.8 - 0.6 * math.exp(-0.3 * l)
        mod = jax.nn.silu(c) @ w_ada[l] + b_ada[l]
        sh1, sc1, gt1, sh2, sc2, gt2 = [m[:, None, :] for m in jnp.split(mod, 6, axis=-1)]

        h = _rmsnorm(x, g_mix[l]) * (1.0 + sc1) + sh1
        proj = h @ w_in[l]
        qa, ka, va, qh, ih, zf, zb, gh = jnp.split(proj, split_pts, axis=-1)

        qa = _rmsnorm(qa.reshape(B, S, ATTN_HEADS, 2, ATTN_QK_DIM), g_q[l])
        ka = _rmsnorm(ka.reshape(B, S, ATTN_HEADS, 2, ATTN_QK_DIM), g_k[l])
        to_bhsd = lambda t: jnp.transpose(t, (0, 2, 1, 3))
        q1, q2 = to_bhsd(qa[..., 0, :]), to_bhsd(qa[..., 1, :])
        k1, k2 = to_bhsd(ka[..., 0, :]), to_bhsd(ka[..., 1, :])
        vh = to_bhsd(va.reshape(B, S, ATTN_HEADS, ATTN_V_DIM))
        lam = (jnp.exp(jnp.sum(lam_q1[l].astype(jnp.float32) * lam_k1[l].astype(jnp.float32)))
               - jnp.exp(jnp.sum(lam_q2[l].astype(jnp.float32) * lam_k2[l].astype(jnp.float32)))
               + lam_init)
        oa = _diff_attention(q1, q2, k1, k2, vh, positions, rel_bias, lam)
        oa = _rmsnorm(to_bhsd(oa), g_sub[l]) * (1.0 - lam_init)
        oa = oa.reshape(B, S, ATTN_WIDTH).astype(x.dtype)

        oh = _hgrn2_mixer(qh, ih, zf, zb, gh, lbs[:, l], g_hgrn[l]).astype(x.dtype)

        x = x + gt1 * (jnp.concatenate([oa, oh], axis=-1) @ w_out[l])

        h2 = _rmsnorm(x, g_ffn[l]) * (1.0 + sc2) + sh2
        x = x + gt2 * _moe(h2, w_router[l], router_bias[l], w_exp_gate[l], w_exp_up[l],
                           w_exp_down[l], w_sh_gate[l], w_sh_up[l], w_sh_down[l])
    return x
```

```python
import functools
import math

import jax
import jax.numpy as jnp
from jax import lax
from jax.experimental import pallas as pl
from jax.experimental.pallas import tpu as pltpu

F32 = jnp.float32
BF16 = jnp.bfloat16
I32 = jnp.int32
U32 = jnp.uint32

D_MODEL = 2048
N_HEADS = 8
QK_DIM = 64
HEAD_DIM = 128
SEG = 1024
N_SEG = 8
REL_BUCKETS = 32
REL_MAX_DIST = 128
N_EXPERTS = 256
TOP_K = 8
N_GROUPS = 8
TOPK_GROUPS = 4
GROUP_SIZE = N_EXPERTS // N_GROUPS
EXPERT_DIM = 512
ROUTED_SCALE = 2.5
EPS = 1e-6
LAM_INIT = 0.8 - 0.6 * math.exp(-0.3 * 0)

LANES = 128
VMEM_LIMIT = 56 * 1024 * 1024

TM_IN = 512
T_ATT = 512
C_HGRN = 64
TM_OUT = 256
T_ROW = 256
TB_EXP = 256
NEG_BIG = -1e30
EXP_CLAMP = 80.0

NT_DIMS = (((1,), (1,)), ((), ()))


def _silu(x):
    return x * jax.nn.sigmoid(x)


def _pack_pair(lo_f32, hi_f32):
    lo = lax.bitcast_convert_type(lo_f32.astype(BF16).astype(F32), U32)
    hi = lax.bitcast_convert_type(hi_f32.astype(BF16).astype(F32), U32)
    return (hi & jnp.uint32(0xFFFF0000)) | (lo >> 16)


def _unpack_pair(word):
    lo = lax.bitcast_convert_type(word << 16, F32)
    hi = lax.bitcast_convert_type(word & jnp.uint32(0xFFFF0000), F32)
    return lo, hi


def _ada_kernel(c_ref, w_ref, b_ref, o_ref):
    a = _silu(c_ref[...]).astype(BF16)
    o_ref[...] = jnp.dot(a, w_ref[...].astype(BF16), preferred_element_type=F32) + b_ref[...]


def _ada(c_pad, w, b):
    d, n = w.shape
    tn = 1024
    return pl.pallas_call(
        _ada_kernel,
        grid=(n // tn,),
        in_specs=[pl.BlockSpec((8, d), lambda j: (0, 0)),
                  pl.BlockSpec((d, tn), lambda j: (0, j)),
                  pl.BlockSpec((1, tn), lambda j: (0, j))],
        out_specs=pl.BlockSpec((8, tn), lambda j: (0, j)),
        out_shape=jax.ShapeDtypeStruct((8, n), F32),
        compiler_params=pltpu.CompilerParams(dimension_semantics=("arbitrary",),
                                             vmem_limit_bytes=VMEM_LIMIT),
        name="ada",
    )(c_pad, w, b)


def _inproj_kernel(x_ref, sc_ref, sh_ref, gmix_ref, w_ref, qkg_ref, lb_ref, g64_ref,
                   p_ref, lf_ref, h_scr, acc_scr):
    j = pl.program_id(1)

    @pl.when(j == 0)
    def _():
        x = x_ref[...]
        ms = jnp.mean(x * x, axis=-1, keepdims=True)
        y = x * lax.rsqrt(ms + EPS) * gmix_ref[...]
        h_scr[...] = (y * (1.0 + sc_ref[0]) + sh_ref[0]).astype(BF16)

    acc_scr[...] = jnp.dot(h_scr[...], w_ref[...], preferred_element_type=F32)

    @pl.when(j < 2)
    def _():
        gain = qkg_ref[0]
        for c in range(SEG // LANES):
            sl = slice(c * LANES, (c + 1) * LANES)
            xs = acc_scr[:, sl]
            ms = jnp.dot((xs * xs).astype(BF16), g64_ref[...], preferred_element_type=F32)
            p_ref[:, sl] = (xs * lax.rsqrt(ms + EPS) * gain[:, sl]).astype(BF16)

    @pl.when((j == 2) | (j == 4))
    def _():
        p_ref[...] = acc_scr[...].astype(BF16)

    @pl.when((j == 3) | (j == 7))
    def _():
        p_ref[...] = _silu(acc_scr[...]).astype(BF16)

    @pl.when((j == 5) | (j == 6))
    def _():
        z = acc_scr[...]
        lb = lb_ref[0]
        f = lb + (1.0 - lb) * jax.nn.sigmoid(z)
        lf_ref[...] = jnp.log(f)
        p_ref[...] = z.astype(BF16)


def _inproj(x2d, sc1, sh1, g_mix, w_in_bf, qk_gain, lbs, g64, seq):
    n, d = x2d.shape
    tm = TM_IN
    tiles_per_batch = seq // tm
    return pl.pallas_call(
        _inproj_kernel,
        grid=(n // tm, N_SEG),
        in_specs=[
            pl.BlockSpec((tm, d), lambda i, j: (i, 0)),
            pl.BlockSpec((1, 1, d), lambda i, j: (i // tiles_per_batch, 0, 0)),
            pl.BlockSpec((1, 1, d), lambda i, j: (i // tiles_per_batch, 0, 0)),
            pl.BlockSpec((1, d), lambda i, j: (0, 0)),
            pl.BlockSpec((d, SEG), lambda i, j: (0, j)),
            pl.BlockSpec((1, 1, SEG), lambda i, j: (jnp.minimum(j, 1), 0, 0)),
            pl.BlockSpec((1, 1, SEG), lambda i, j: (jnp.clip(j - 5, 0, 1), 0, 0)),
            pl.BlockSpec((LANES, LANES), lambda i, j: (0, 0)),
        ],
        out_specs=[
            pl.BlockSpec((tm, SEG), lambda i, j: (i, j)),
            pl.BlockSpec((tm, SEG), lambda i, j: (i, jnp.clip(j - 5, 0, 1))),
        ],
        out_shape=[jax.ShapeDtypeStruct((n, N_SEG * SEG), BF16),
                   jax.ShapeDtypeStruct((n, 2 * SEG), F32)],
        scratch_shapes=[pltpu.VMEM((tm, d), BF16), pltpu.VMEM((tm, SEG), F32)],
        compiler_params=pltpu.CompilerParams(dimension_semantics=("arbitrary", "arbitrary"),
                                             vmem_limit_bytes=VMEM_LIMIT),
        name="inproj",
    )(x2d, sc1, sh1, g_mix, w_in_bf, qk_gain, lbs, g64)


def _t5_bias_tile(pos_q, pos_k, rb_ref, h):
    half = REL_BUCKETS // 2
    max_exact = half // 2
    rel = pos_k - pos_q
    n = jnp.abs(rel)
    nf = jnp.maximum(n, 1).astype(F32)
    large = max_exact + (jnp.log(nf / max_exact) / math.log(REL_MAX_DIST / max_exact)
                         * (half - max_exact)).astype(I32)
    large = jnp.minimum(large, half - 1)
    bucket = jnp.where(rel > 0, half, 0) + jnp.where(n < max_exact, n, large)
    level = [rb_ref[h, b] for b in range(REL_BUCKETS)]
    bit = 1
    while len(level) > 1:
        m = (bucket & bit) != 0
        level = [jnp.where(m, level[2 * i + 1], level[2 * i]) for i in range(len(level) // 2)]
        bit *= 2
    return level[0]


def _flash_update(s, vv, m_ref, l_ref, a_ref):
    m_old = m_ref[...]
    m_new = jnp.maximum(m_old, jnp.max(s, axis=1, keepdims=True))
    p = jnp.exp(s - m_new)
    alpha = jnp.exp(m_old - m_new)
    l_ref[...] = alpha * l_ref[...] + jnp.sum(p, axis=1, keepdims=True)
    a_ref[...] = alpha * a_ref[...] + jnp.dot(p.astype(BF16), vv, preferred_element_type=F32)
    m_ref[...] = m_new


def _attn_kernel(tmin_ref, tmax_ref, q_ref, k_ref, v_ref, posq_ref, posk_ref, rb_ref, lam_ref,
                 gsub_ref, o_ref, m1, l1, a1, m2, l2, a2):
    b = pl.program_id(0)
    h = pl.program_id(1)
    i = pl.program_id(2)
    nt = pl.num_programs(2)
    t = T_ATT

    q = q_ref[...]
    q1 = q[:, :QK_DIM]
    q2 = q[:, QK_DIM:]
    for m_ref, l_ref, a_ref in ((m1, l1, a1), (m2, l2, a2)):
        m_ref[...] = jnp.full(m_ref.shape, NEG_BIG, F32)
        l_ref[...] = jnp.zeros(l_ref.shape, F32)
        a_ref[...] = jnp.zeros(a_ref.shape, F32)

    q_lo = tmin_ref[b * nt + i]
    q_hi = tmax_ref[b * nt + i]
    c_pos = rb_ref[h, REL_BUCKETS - 1]
    c_neg = rb_ref[h, REL_BUCKETS // 2 - 1]
    pos_q = posq_ref[0]

    def body(j, carry):
        r0 = pl.multiple_of(j * t, t)
        kk = k_ref[pl.ds(r0, t), :]
        vv = v_ref[pl.ds(r0, t), :]
        s1 = lax.dot_general(q1, kk[:, :QK_DIM], NT_DIMS, preferred_element_type=F32)
        s2 = lax.dot_general(q2, kk[:, QK_DIM:], NT_DIMS, preferred_element_type=F32)
        lo = tmin_ref[b * nt + j] - q_hi
        hi = tmax_ref[b * nt + j] - q_lo
        far = (lo >= REL_MAX_DIST) | (hi <= -REL_MAX_DIST)

        @pl.when(far)
        def _():
            c = jnp.where(lo >= REL_MAX_DIST, c_pos, c_neg)
            _flash_update(s1 + c, vv, m1, l1, a1)
            _flash_update(s2 + c, vv, m2, l2, a2)

        @pl.when(jnp.logical_not(far))
        def _():
            bias = _t5_bias_tile(pos_q, posk_ref[0, j], rb_ref, h)
            _flash_update(s1 + bias, vv, m1, l1, a1)
            _flash_update(s2 + bias, vv, m2, l2, a2)

        return carry

    lax.fori_loop(0, nt, body, 0)

    o = a1[...] / l1[...] - lam_ref[0] * (a2[...] / l2[...])
    ms = jnp.mean(o * o, axis=-1, keepdims=True)
    o_ref[...] = (o * lax.rsqrt(ms + EPS) * gsub_ref[...] * (1.0 - LAM_INIT)).astype(BF16)


def _attention(p, tmin, tmax, posq, posk, rb_t, lam, g_sub, batch, seq):
    t = T_ATT
    nt = seq // t
    n = batch * seq
    grid_spec = pltpu.PrefetchScalarGridSpec(
        num_scalar_prefetch=2,
        grid=(batch, N_HEADS, nt),
        in_specs=[
            pl.BlockSpec((t, HEAD_DIM), lambda b, h, i, *_: (b * nt + i, h)),
            pl.BlockSpec((seq, HEAD_DIM), lambda b, h, i, *_: (b, N_HEADS + h)),
            pl.BlockSpec((seq, HEAD_DIM), lambda b, h, i, *_: (b, 2 * N_HEADS + h)),
            pl.BlockSpec((1, t, 1), lambda b, h, i, *_: (b, i, 0)),
            pl.BlockSpec((1, nt, 1, t), lambda b, h, i, *_: (b, 0, 0, 0)),
            pl.BlockSpec(memory_space=pltpu.SMEM),
            pl.BlockSpec(memory_space=pltpu.SMEM),
            pl.BlockSpec((1, HEAD_DIM), lambda b, h, i, *_: (0, 0)),
        ],
        out_specs=pl.BlockSpec((t, HEAD_DIM), lambda b, h, i, *_: (b * nt + i, h)),
        scratch_shapes=[pltpu.VMEM((t, 1), F32), pltpu.VMEM((t, 1), F32), pltpu.VMEM((t, HEAD_DIM), F32),
                        pltpu.VMEM((t, 1), F32), pltpu.VMEM((t, 1), F32), pltpu.VMEM((t, HEAD_DIM), F32)],
    )
    return pl.pallas_call(
        _attn_kernel,
        grid_spec=grid_spec,
        out_shape=jax.ShapeDtypeStruct((n, N_HEADS * HEAD_DIM), BF16),
        compiler_params=pltpu.CompilerParams(
            dimension_semantics=("arbitrary", "arbitrary", "arbitrary"),
            vmem_limit_bytes=VMEM_LIMIT),
        name="attn",
    )(tmin, tmax, p, p, p, posq, posk, rb_t, lam, g_sub)


def _hgrn_chunk(r0, forward, q_ref, v_ref, lf_ref, tri_ref, st_scr):
    c = C_HGRN
    g = lf_ref[pl.ds(r0, c), :]
    g_hi = g.astype(BF16)
    g_lo = (g - g_hi.astype(F32)).astype(BF16)
    tri = tri_ref[...]
    bsum = (jnp.dot(tri, g_hi, preferred_element_type=F32)
            + jnp.dot(tri, g_lo, preferred_element_type=F32))
    if forward:
        ref = bsum[c // 2 - 1:c // 2, :]
        b_end = bsum[c - 1:c, :]
    else:
        ref = bsum[c // 2:c // 2 + 1, :]
        b_end = bsum[0:1, :]
    q = q_ref[pl.ds(r0, c), :].astype(F32)
    v = v_ref[pl.ds(r0, c), :]
    kf = 1.0 - jnp.exp(g)
    q_in = (q * jnp.exp(bsum)).astype(BF16)
    q_t = (q * jnp.exp(jnp.minimum(bsum - ref, EXP_CLAMP))).astype(BF16)
    k_t = (kf * jnp.exp(jnp.minimum(ref - bsum, EXP_CLAMP))).astype(BF16)
    k_st = (kf * jnp.exp(b_end - bsum)).astype(BF16)
    a = lax.dot_general(q_t, k_t, NT_DIMS, preferred_element_type=F32)
    row = lax.broadcasted_iota(I32, (c, c), 0)
    col = lax.broadcasted_iota(I32, (c, c), 1)
    keep = (col <= row) if forward else (col >= row)
    a = jnp.where(keep, a, 0.0).astype(BF16)
    st = st_scr[...]
    o = (lax.dot_general(q_in, st.astype(BF16), NT_DIMS, preferred_element_type=F32)
         + jnp.dot(a, v, preferred_element_type=F32))
    v_t = v.astype(F32).T.astype(BF16)
    st_scr[...] = st * jnp.exp(b_end) + jnp.dot(v_t, k_st, preferred_element_type=F32)
    return o


def _hgrn_kernel(q_ref, v_ref, g_ref, lff_ref, lfb_ref, tril_ref, triu_ref, gh_ref, o_ref,
                 of_scr, st_scr):
    c = C_HGRN
    nc = q_ref.shape[0] // c

    st_scr[...] = jnp.zeros(st_scr.shape, F32)

    def fwd(ci, carry):
        r0 = pl.multiple_of(ci * c, c)
        of_scr[pl.ds(r0, c), :] = _hgrn_chunk(r0, True, q_ref, v_ref, lff_ref, tril_ref, st_scr)
        return carry

    lax.fori_loop(0, nc, fwd, 0)

    st_scr[...] = jnp.zeros(st_scr.shape, F32)

    def bwd(ci, carry):
        r0 = pl.multiple_of((nc - 1 - ci) * c, c)
        o = of_scr[pl.ds(r0, c), :] + _hgrn_chunk(r0, False, q_ref, v_ref, lfb_ref, triu_ref, st_scr)
        ms = jnp.mean(o * o, axis=-1, keepdims=True)
        y = o * lax.rsqrt(ms + EPS) * gh_ref[...] * g_ref[pl.ds(r0, c), :].astype(F32)
        o_ref[pl.ds(r0, c), :] = y.astype(BF16)
        return carry

    lax.fori_loop(0, nc, bwd, 0)


def _hgrn(p, lf, tril, triu, g_hgrn, batch, seq):
    n = batch * seq
    c = C_HGRN
    blk = lambda off: pl.BlockSpec((seq, HEAD_DIM), lambda b, h: (b, off + h))
    return pl.pallas_call(
        _hgrn_kernel,
        grid=(batch, N_HEADS),
        in_specs=[blk(3 * N_HEADS), blk(4 * N_HEADS), blk(7 * N_HEADS), blk(0), blk(N_HEADS),
                  pl.BlockSpec((c, c), lambda b, h: (0, 0)),
                  pl.BlockSpec((c, c), lambda b, h: (0, 0)),
                  pl.BlockSpec((1, HEAD_DIM), lambda b, h: (0, 0))],
        out_specs=pl.BlockSpec((seq, HEAD_DIM), lambda b, h: (b, h)),
        out_shape=jax.ShapeDtypeStruct((n, N_HEADS * HEAD_DIM), BF16),
        scratch_shapes=[pltpu.VMEM((seq, HEAD_DIM), F32), pltpu.VMEM((HEAD_DIM, HEAD_DIM), F32)],
        compiler_params=pltpu.CompilerParams(dimension_semantics=("arbitrary", "arbitrary"),
                                             vmem_limit_bytes=VMEM_LIMIT),
        name="hgrn",
    )(p, p, p, lf, lf, tril, triu, g_hgrn)


def _col_max(x):
    return jnp.max(x, axis=0, keepdims=True)


def _outproj_kernel(oa_ref, oh_ref, wa_ref, wb_ref, x_ref, gt_ref, gffn_ref, sc_ref, sh_ref,
                    wr_hi_ref, wr_lo_ref, rbias_ref, upper_ref, ones_ref,
                    x1_ref, hp_ref, eidx_ref, slot_ref, gate_ref, cnt_ref, cnt_scr):
    i = pl.program_id(0)
    tm = x_ref.shape[0]

    @pl.when(i == 0)
    def _():
        cnt_scr[...] = jnp.zeros(cnt_scr.shape, F32)

    acc = (jnp.dot(oa_ref[...], wa_ref[...], preferred_element_type=F32)
           + jnp.dot(oh_ref[...], wb_ref[...], preferred_element_type=F32))
    x1 = x_ref[...] + gt_ref[0] * acc
    x1_ref[...] = x1
    ms = jnp.mean(x1 * x1, axis=-1, keepdims=True)
    h2 = x1 * lax.rsqrt(ms + EPS) * gffn_ref[...] * (1.0 + sc_ref[0]) + sh_ref[0]
    half = h2.shape[1] // 2
    hp_ref[...] = _pack_pair(h2[:, :half], h2[:, half:])

    h_hi = h2.astype(BF16)
    h_lo = (h2 - h_hi.astype(F32)).astype(BF16)
    wr_hi = wr_hi_ref[...]
    logits = (lax.dot_general(wr_hi, h_hi, NT_DIMS, preferred_element_type=F32)
              + lax.dot_general(wr_hi, h_lo, NT_DIMS, preferred_element_type=F32)
              + lax.dot_general(wr_lo_ref[...], h_hi, NT_DIMS, preferred_element_type=F32))
    scores = jax.nn.sigmoid(logits)
    biased = scores + rbias_ref[...]

    gs = []
    for g in range(N_GROUPS):
        blk = biased[g * GROUP_SIZE:(g + 1) * GROUP_SIZE, :]
        top1 = _col_max(blk)
        eq = blk == top1
        n_eq = jnp.sum(eq.astype(F32), axis=0, keepdims=True)
        second = _col_max(jnp.where(eq, -jnp.inf, blk))
        gs.append(top1 + jnp.where(n_eq > 1.0, top1, second))
    gsm = jnp.concatenate(gs, axis=0)
    giota = lax.broadcasted_iota(I32, gsm.shape, 0)
    gsel = jnp.zeros(gsm.shape, F32)
    for _ in range(TOPK_GROUPS):
        top = _col_max(gsm)
        idx = jnp.min(jnp.where(gsm == top, giota, N_GROUPS), axis=0, keepdims=True)
        pick = giota == idx
        gsel = jnp.where(pick, 1.0, gsel)
        gsm = jnp.where(pick, -jnp.inf, gsm)
    emask = jnp.concatenate(
        [jnp.broadcast_to(gsel[g:g + 1, :], (GROUP_SIZE, tm)) for g in range(N_GROUPS)], axis=0)
    masked = jnp.where(emask > 0.5, biased, -jnp.inf)

    eiota = lax.broadcasted_iota(I32, masked.shape, 0)
    idxs, gates = [], []
    for _ in range(TOP_K):
        top = _col_max(masked)
        idx = jnp.min(jnp.where(masked == top, eiota, N_EXPERTS), axis=0, keepdims=True)
        pick = eiota == idx
        gates.append(jnp.sum(jnp.where(pick, scores, 0.0), axis=0, keepdims=True))
        idxs.append(idx)
        masked = jnp.where(pick, -jnp.inf, masked)
    gate = jnp.concatenate(gates, axis=0)
    gate = gate / jnp.sum(gate, axis=0, keepdims=True) * ROUTED_SCALE
    eidx = jnp.concatenate(idxs, axis=0)
    eidx_ref[...] = eidx
    gate_ref[...] = gate

    sel = jnp.zeros(masked.shape, F32)
    for k in range(TOP_K):
        sel = jnp.where(eiota == idxs[k], 1.0, sel)
    sel_bf = sel.astype(BF16)
    rank = jnp.dot(sel_bf, upper_ref[...], preferred_element_type=F32)
    base = cnt_scr[...]
    posn = base[:, :1] + rank
    slots = [jnp.sum(jnp.where(eiota == idxs[k], posn, 0.0), axis=0, keepdims=True)
             for k in range(TOP_K)]
    slot_ref[...] = jnp.concatenate(slots, axis=0).astype(I32)
    new_cnt = base + jnp.dot(sel_bf, ones_ref[...], preferred_element_type=F32)
    cnt_scr[...] = new_cnt
    cnt_ref[...] = new_cnt


def _outproj(oa, oh, wa, wb, x2d, gt1, g_ffn, sc2, sh2, wr_hi, wr_lo, rbias, upper, ones, seq):
    n, d = x2d.shape
    tm = TM_OUT
    tiles_per_batch = seq // tm
    half = d // 2
    row = lambda w: pl.BlockSpec((tm, w), lambda i: (i, 0))
    const = lambda shape: pl.BlockSpec(shape, lambda i: tuple(0 for _ in shape))
    per_batch = pl.BlockSpec((1, 1, d), lambda i: (i // tiles_per_batch, 0, 0))
    tok = pl.BlockSpec((TOP_K, tm), lambda i: (0, i))
    return pl.pallas_call(
        _outproj_kernel,
        grid=(n // tm,),
        in_specs=[row(half), row(half), const((half, d)), const((half, d)), row(d), per_batch,
                  const((1, d)), per_batch, per_batch,
                  const((N_EXPERTS, d)), const((N_EXPERTS, d)), const((N_EXPERTS, 1)),
                  const((tm, tm)), const((tm, LANES))],
        out_specs=[row(d), row(half), tok, tok, tok, const((N_EXPERTS, LANES))],
        out_shape=[jax.ShapeDtypeStruct((n, d), F32),
                   jax.ShapeDtypeStruct((n, half), U32),
                   jax.ShapeDtypeStruct((TOP_K, n), I32),
                   jax.ShapeDtypeStruct((TOP_K, n), I32),
                   jax.ShapeDtypeStruct((TOP_K, n), F32),
                   jax.ShapeDtypeStruct((N_EXPERTS, LANES), F32)],
        scratch_shapes=[pltpu.VMEM((N_EXPERTS, LANES), F32)],
        compiler_params=pltpu.CompilerParams(dimension_semantics=("arbitrary",),
                                             vmem_limit_bytes=VMEM_LIMIT),
        name="outproj",
    )(oa, oh, wa, wb, x2d, gt1, g_ffn, sc2, sh2, wr_hi, wr_lo, rbias, upper, ones)


def _scatter_kernel(pstart_ref, pfill_ref, nv_ref, e_ref, slot_ref, h_ref, xs_ref, zero_scr, sem, zsem):
    i = pl.program_id(0)
    ts = h_ref.shape[0]
    tb = zero_scr.shape[0]
    n_tail = xs_ref.shape[0] // tb - nv_ref[0]

    @pl.when(i == 0)
    def _():
        zero_scr[...] = jnp.zeros(zero_scr.shape, U32)

        def fill(e, carry):
            start = pl.multiple_of(pfill_ref[e], 8)
            pltpu.make_async_copy(zero_scr, xs_ref.at[pl.ds(start, tb)], zsem).start()
            return carry

        lax.fori_loop(0, N_EXPERTS, fill, 0)

        def fill_tail(j, carry):
            start = pl.multiple_of((nv_ref[0] + j) * tb, tb)
            pltpu.make_async_copy(zero_scr, xs_ref.at[pl.ds(start, tb)], zsem).start()
            return carry

        lax.fori_loop(0, n_tail, fill_tail, 0)

        def drain(e, carry):
            pltpu.make_async_copy(zero_scr, xs_ref.at[pl.ds(0, tb)], zsem).wait()
            return carry

        lax.fori_loop(0, N_EXPERTS + n_tail, drain, 0)

    def issue(t, carry):
        for k in range(TOP_K):
            dest = pstart_ref[e_ref[k, t]] + slot_ref[k, t]
            pltpu.make_async_copy(h_ref.at[pl.ds(t, 1)], xs_ref.at[pl.ds(dest, 1)], sem).start()
        return carry

    lax.fori_loop(0, ts, issue, 0)
    for k in range(TOP_K):
        pltpu.make_async_copy(h_ref, xs_ref.at[pl.ds(0, ts)], sem).wait()


def _scatter(pstart, pfill, n_valid, eidx, slot, hp, n_rows):
    n, w = hp.shape
    ts = T_ROW
    grid_spec = pltpu.PrefetchScalarGridSpec(
        num_scalar_prefetch=3,
        grid=(n // ts,),
        in_specs=[pl.BlockSpec((TOP_K, ts), lambda i, *_: (0, i), memory_space=pltpu.SMEM),
                  pl.BlockSpec((TOP_K, ts), lambda i, *_: (0, i), memory_space=pltpu.SMEM),
                  pl.BlockSpec((ts, w), lambda i, *_: (i, 0))],
        out_specs=pl.BlockSpec(memory_space=pl.ANY),
        scratch_shapes=[pltpu.VMEM((TB_EXP, w), U32), pltpu.SemaphoreType.DMA,
                        pltpu.SemaphoreType.DMA],
    )
    return pl.pallas_call(
        _scatter_kernel,
        grid_spec=grid_spec,
        out_shape=jax.ShapeDtypeStruct((n_rows, w), U32),
        compiler_params=pltpu.CompilerParams(dimension_semantics=("arbitrary",),
                                             vmem_limit_bytes=VMEM_LIMIT),
        name="scatter",
    )(pstart, pfill, n_valid, eidx, slot, hp)


def _experts_kernel(be_ref, nv_ref, xs_ref, wg_ref, wu_ref, wd_ref, y_ref, wg_bf, wu_bf, wd_bf):
    i = pl.program_id(0)
    prev = jnp.maximum(i - 1, 0)
    valid = i < nv_ref[0]
    fresh = valid & ((i == 0) | (be_ref[i] != be_ref[prev]))

    @pl.when(fresh)
    def _():
        wg_bf[...] = wg_ref[0].astype(BF16)
        wu_bf[...] = wu_ref[0].astype(BF16)
        wd_bf[...] = wd_ref[0].astype(BF16)

    @pl.when(valid)
    def _():
        lo, hi = _unpack_pair(xs_ref[...])
        x = jnp.concatenate([lo.astype(BF16), hi.astype(BF16)], axis=1)
        hg = jnp.dot(x, wg_bf[...], preferred_element_type=F32)
        hu = jnp.dot(x, wu_bf[...], preferred_element_type=F32)
        a = (_silu(hg) * hu).astype(BF16)
        y = jnp.dot(a, wd_bf[...], preferred_element_type=F32)
        half = y.shape[1] // 2
        y_ref[...] = _pack_pair(y[:, :half], y[:, half:])

    @pl.when(jnp.logical_not(valid))
    def _():
        y_ref[...] = jnp.zeros(y_ref.shape, U32)


def _experts(block_e, n_valid, xs, w_gate, w_up, w_down, n_blocks):
    tb = TB_EXP
    w = xs.shape[1]
    _, d, f = w_gate.shape

    def blk(i, be, nv):
        return jnp.minimum(i, nv[0] - 1)

    grid_spec = pltpu.PrefetchScalarGridSpec(
        num_scalar_prefetch=2,
        grid=(n_blocks,),
        in_specs=[pl.BlockSpec((tb, w), lambda i, be, nv: (blk(i, be, nv), 0)),
                  pl.BlockSpec((1, d, f), lambda i, be, nv: (be[blk(i, be, nv)], 0, 0)),
                  pl.BlockSpec((1, d, f), lambda i, be, nv: (be[blk(i, be, nv)], 0, 0)),
                  pl.BlockSpec((1, f, d), lambda i, be, nv: (be[blk(i, be, nv)], 0, 0))],
        out_specs=pl.BlockSpec((tb, w), lambda i, be, nv: (i, 0)),
        scratch_shapes=[pltpu.VMEM((d, f), BF16), pltpu.VMEM((d, f), BF16), pltpu.VMEM((f, d), BF16)],
    )
    return pl.pallas_call(
        _experts_kernel,
        grid_spec=grid_spec,
        out_shape=jax.ShapeDtypeStruct((n_blocks * tb, w), U32),
        compiler_params=pltpu.CompilerParams(dimension_semantics=("arbitrary",),
                                             vmem_limit_bytes=VMEM_LIMIT),
        name="experts",
    )(block_e, n_valid, xs, w_gate, w_up, w_down)


def _combine_kernel(pstart_ref, e_ref, slot_ref, x1_ref, hp_ref, gate_ref, gt_ref,
                    wsg_ref, wsu_ref, wsd_ref, y_ref, o_ref, buf, sem):
    tc = x1_ref.shape[0]

    def issue(t, carry):
        for k in range(TOP_K):
            src = pstart_ref[e_ref[k, t]] + slot_ref[k, t]
            pltpu.make_async_copy(y_ref.at[pl.ds(src, 1)], buf.at[k, pl.ds(t, 1)], sem).start()
        return carry

    lax.fori_loop(0, tc, issue, 0)

    lo, hi = _unpack_pair(hp_ref[...])
    x = jnp.concatenate([lo.astype(BF16), hi.astype(BF16)], axis=1)
    hg = jnp.dot(x, wsg_ref[...], preferred_element_type=F32)
    hu = jnp.dot(x, wsu_ref[...], preferred_element_type=F32)
    a = (_silu(hg) * hu).astype(BF16)
    shared = jnp.dot(a, wsd_ref[...], preferred_element_type=F32)

    for k in range(TOP_K):
        pltpu.make_async_copy(y_ref.at[pl.ds(0, tc)], buf.at[k], sem).wait()

    half = shared.shape[1] // 2
    gate = gate_ref[...]
    r_lo = shared[:, :half]
    r_hi = shared[:, half:]
    for k in range(TOP_K):
        lo, hi = _unpack_pair(buf[k])
        gk = gate[:, k:k + 1]
        r_lo = r_lo + gk * lo
        r_hi = r_hi + gk * hi
    gt = gt_ref[0]
    o_ref[:, :half] = x1_ref[:, :half] + gt[:, :half] * r_lo
    o_ref[:, half:] = x1_ref[:, half:] + gt[:, half:] * r_hi


def _combine(pstart, eidx, slot, x1, hp, gate_t, gt2, wsg, wsu, wsd, y, seq):
    n, d = x1.shape
    tc = T_ROW
    w = hp.shape[1]
    f = wsg.shape[1]
    tiles_per_batch = seq // tc
    grid_spec = pltpu.PrefetchScalarGridSpec(
        num_scalar_prefetch=1,
        grid=(n // tc,),
        in_specs=[pl.BlockSpec((TOP_K, tc), lambda i, *_: (0, i), memory_space=pltpu.SMEM),
                  pl.BlockSpec((TOP_K, tc), lambda i, *_: (0, i), memory_space=pltpu.SMEM),
                  pl.BlockSpec((tc, d), lambda i, *_: (i, 0)),
                  pl.BlockSpec((tc, w), lambda i, *_: (i, 0)),
                  pl.BlockSpec((tc, TOP_K), lambda i, *_: (i, 0)),
                  pl.BlockSpec((1, 1, d), lambda i, *_: (i // tiles_per_batch, 0, 0)),
                  pl.BlockSpec((d, f), lambda i, *_: (0, 0)),
                  pl.BlockSpec((d, f), lambda i, *_: (0, 0)),
                  pl.BlockSpec((f, d), lambda i, *_: (0, 0)),
                  pl.BlockSpec(memory_space=pl.ANY)],
        out_specs=pl.BlockSpec((tc, d), lambda i, *_: (i, 0)),
        scratch_shapes=[pltpu.VMEM((TOP_K, tc, w), U32), pltpu.SemaphoreType.DMA],
    )
    return pl.pallas_call(
        _combine_kernel,
        grid_spec=grid_spec,
        out_shape=jax.ShapeDtypeStruct((n, d), F32),
        compiler_params=pltpu.CompilerParams(dimension_semantics=("arbitrary",),
                                             vmem_limit_bytes=VMEM_LIMIT),
        name="combine",
    )(pstart, eidx, slot, x1, hp, gate_t, gt2, wsg, wsu, wsd, y)


def kernel(x, c, positions, rel_bias, hgrn_lb_logits, w_ada, b_ada, g_mix, w_in, g_q, g_k, lam_q1, lam_k1, lam_q2, lam_k2, g_sub, g_hgrn, w_out, g_ffn, w_router, router_bias, w_exp_gate, w_exp_up, w_exp_down, w_sh_gate, w_sh_up, w_sh_down):
    batch, seq, d = x.shape
    n = batch * seq
    layer = 0
    x2d = x.reshape(n, d)

    c_pad = jnp.zeros((8, d), F32).at[:batch].set(c.astype(F32))
    mod = _ada(c_pad, w_ada[layer], b_ada[layer][None, :])[:batch]
    sh1, sc1, gt1, sh2, sc2, gt2 = [m.reshape(batch, 1, d) for m in jnp.split(mod, 6, axis=-1)]

    lbs = jnp.cumsum(jax.nn.softmax(hgrn_lb_logits.astype(F32), axis=1), axis=1)[:, layer]
    lbs = lbs.reshape(2, 1, SEG)
    reps = SEG // QK_DIM
    qk_gain = jnp.stack([jnp.tile(g_q[layer].astype(F32), reps) * (QK_DIM ** -0.5),
                         jnp.tile(g_k[layer].astype(F32), reps)]).reshape(2, 1, SEG)
    lane = jnp.arange(LANES)
    g64 = jnp.where((lane[:, None] // QK_DIM) == (lane[None, :] // QK_DIM), 1.0 / QK_DIM, 0.0).astype(BF16)
    lam = (jnp.exp(jnp.sum(lam_q1[layer].astype(F32) * lam_k1[layer].astype(F32)))
           - jnp.exp(jnp.sum(lam_q2[layer].astype(F32) * lam_k2[layer].astype(F32)))
           + LAM_INIT).reshape(1)

    p, lf = _inproj(x2d, sc1, sh1, g_mix[layer][None, :], w_in[layer].astype(BF16), qk_gain, lbs, g64, seq)

    nt = seq // T_ATT
    pos_tiles = positions.astype(I32).reshape(batch * nt, T_ATT)
    tmin = jnp.min(pos_tiles, axis=1)
    tmax = jnp.max(pos_tiles, axis=1)
    posq = positions.astype(I32).reshape(batch, seq, 1)
    posk = positions.astype(I32).reshape(batch, nt, 1, T_ATT)
    oa = _attention(p, tmin, tmax, posq, posk, rel_bias.astype(F32).T, lam,
                    g_sub[layer][None, :].astype(F32), batch, seq)

    ci = jnp.arange(C_HGRN)
    tril = (ci[None, :] <= ci[:, None]).astype(BF16)
    triu = (ci[None, :] >= ci[:, None]).astype(BF16)
    oh = _hgrn(p, lf, tril, triu, g_hgrn[layer][None, :].astype(F32), batch, seq)

    half = d // 2
    w_out_bf = w_out[layer].astype(BF16)
    wr_t = w_router[layer].astype(F32).T
    wr_hi = wr_t.astype(BF16)
    wr_lo = (wr_t - wr_hi.astype(F32)).astype(BF16)
    ti = jnp.arange(TM_OUT)
    upper = (ti[:, None] < ti[None, :]).astype(BF16)
    ones = jnp.ones((TM_OUT, LANES), BF16)
    x1, hp, eidx, slot, gate, cnt = _outproj(
        oa, oh, w_out_bf[:half], w_out_bf[half:], x2d, gt1, g_ffn[layer][None, :], sc2, sh2,
        wr_hi, wr_lo, router_bias[layer].astype(F32)[:, None], upper, ones, seq)

    tb = TB_EXP
    counts = cnt[:, 0].astype(I32)
    padded = (counts + tb - 1) // tb * tb
    pends = jnp.cumsum(padded)
    pstart = (pends - padded).astype(I32)
    n_blocks = (n * TOP_K) // tb + N_EXPERTS
    n_valid = (pends[-1] // tb).astype(I32).reshape(1)
    block_e = jnp.minimum(jnp.searchsorted(pends, jnp.arange(n_blocks, dtype=I32) * tb, side='right'),
                          N_EXPERTS - 1).astype(I32)
    pfill = ((pstart + counts) // 8 * 8).astype(I32)

    xs = _scatter(pstart, pfill, n_valid, eidx, slot, hp, (n_blocks + 1) * tb)
    y = _experts(block_e, n_valid, xs, w_exp_gate[layer], w_exp_up[layer], w_exp_down[layer], n_blocks)
    out = _combine(pstart, eidx, slot, x1, hp, gate.T, gt2,
                   w_sh_gate[layer].astype(BF16), w_sh_up[layer].astype(BF16),
                   w_sh_down[layer].astype(BF16), y, seq)
    return out.reshape(batch, seq, d)
```

```python
import functools
import math

import jax
import jax.numpy as jnp
from jax import lax
from jax.experimental import pallas as pl
from jax.experimental.pallas import tpu as pltpu

F32 = jnp.float32
BF16 = jnp.bfloat16
I32 = jnp.int32
U32 = jnp.uint32

D_MODEL = 2048
N_HEADS = 8
QK_DIM = 64
HEAD_DIM = 128
SEG = 1024
N_SEG = 8
REL_BUCKETS = 32
REL_MAX_DIST = 128
N_EXPERTS = 256
TOP_K = 8
N_GROUPS = 8
TOPK_GROUPS = 4
GROUP_SIZE = N_EXPERTS // N_GROUPS
EXPERT_DIM = 512
ROUTED_SCALE = 2.5
EPS = 1e-6
LAM_INIT = 0.8 - 0.6 * math.exp(-0.3 * 0)
LOG2E = math.log2(math.e)

LANES = 128
VMEM_LIMIT = 56 * 1024 * 1024

TM_IN = 512
T_ATT = 512
T_SUB = 128
V_PAD = 16
C_HGRN = 64
HGRN_GROUP = 8
TM_OUT = 256
T_ROW = 256
TB_EXP = 256
NEG_BIG = -1e30
EXP_CLAMP = 80.0

NT_DIMS = (((1,), (1,)), ((), ()))


def _silu(x):
    return x * jax.nn.sigmoid(x)


def _pack_pair(lo_f32, hi_f32):
    lo = lax.bitcast_convert_type(lo_f32.astype(BF16).astype(F32), U32)
    hi = lax.bitcast_convert_type(hi_f32.astype(BF16).astype(F32), U32)
    return (hi & jnp.uint32(0xFFFF0000)) | (lo >> 16)


def _unpack_pair(word):
    lo = lax.bitcast_convert_type(word << 16, F32)
    hi = lax.bitcast_convert_type(word & jnp.uint32(0xFFFF0000), F32)
    return lo, hi


def _ada_kernel(c_ref, w_ref, b_ref, o_ref):
    a = _silu(c_ref[...]).astype(BF16)
    o_ref[...] = jnp.dot(a, w_ref[...].astype(BF16), preferred_element_type=F32) + b_ref[...]


def _ada(c_pad, w, b):
    d, n = w.shape
    tn = 1024
    return pl.pallas_call(
        _ada_kernel,
        grid=(n // tn,),
        in_specs=[pl.BlockSpec((8, d), lambda j: (0, 0)),
                  pl.BlockSpec((d, tn), lambda j: (0, j)),
                  pl.BlockSpec((1, tn), lambda j: (0, j))],
        out_specs=pl.BlockSpec((8, tn), lambda j: (0, j)),
        out_shape=jax.ShapeDtypeStruct((8, n), F32),
        compiler_params=pltpu.CompilerParams(dimension_semantics=("arbitrary",),
                                             vmem_limit_bytes=VMEM_LIMIT),
        name="ada",
    )(c_pad, w, b)


def _inproj_kernel(x_ref, sc_ref, sh_ref, gmix_ref, w_ref, qkg_ref, lb_ref, g64_ref,
                   p_ref, lf_ref, h_scr, acc_scr):
    j = pl.program_id(1)

    @pl.when(j == 0)
    def _():
        x = x_ref[...]
        ms = jnp.mean(x * x, axis=-1, keepdims=True)
        y = x * lax.rsqrt(ms + EPS) * gmix_ref[...]
        h_scr[...] = (y * (1.0 + sc_ref[0]) + sh_ref[0]).astype(BF16)

    acc_scr[...] = jnp.dot(h_scr[...], w_ref[...], preferred_element_type=F32)

    @pl.when(j < 2)
    def _():
        gain = qkg_ref[0]
        for c in range(SEG // LANES):
            sl = slice(c * LANES, (c + 1) * LANES)
            xs = acc_scr[:, sl]
            ms = jnp.dot((xs * xs).astype(BF16), g64_ref[...], preferred_element_type=F32)
            p_ref[:, sl] = (xs * lax.rsqrt(ms + EPS) * gain[:, sl]).astype(BF16)

    @pl.when((j == 2) | (j == 4))
    def _():
        p_ref[...] = acc_scr[...].astype(BF16)

    @pl.when((j == 3) | (j == 7))
    def _():
        p_ref[...] = _silu(acc_scr[...]).astype(BF16)

    @pl.when((j == 5) | (j == 6))
    def _():
        z = acc_scr[...]
        lb = lb_ref[0]
        f = lb + (1.0 - lb) * jax.nn.sigmoid(z)
        lf_ref[...] = jnp.log(f)
        p_ref[...] = z.astype(BF16)


def _inproj(x2d, sc1, sh1, g_mix, w_in_bf, qk_gain, lbs, g64, seq):
    n, d = x2d.shape
    tm = TM_IN
    tiles_per_batch = seq // tm
    return pl.pallas_call(
        _inproj_kernel,
        grid=(n // tm, N_SEG),
        in_specs=[
            pl.BlockSpec((tm, d), lambda i, j: (i, 0)),
            pl.BlockSpec((1, 1, d), lambda i, j: (i // tiles_per_batch, 0, 0)),
            pl.BlockSpec((1, 1, d), lambda i, j: (i // tiles_per_batch, 0, 0)),
            pl.BlockSpec((1, d), lambda i, j: (0, 0)),
            pl.BlockSpec((d, SEG), lambda i, j: (0, j)),
            pl.BlockSpec((1, 1, SEG), lambda i, j: (jnp.minimum(j, 1), 0, 0)),
            pl.BlockSpec((1, 1, SEG), lambda i, j: (jnp.clip(j - 5, 0, 1), 0, 0)),
            pl.BlockSpec((LANES, LANES), lambda i, j: (0, 0)),
        ],
        out_specs=[
            pl.BlockSpec((tm, SEG), lambda i, j: (i, j)),
            pl.BlockSpec((tm, SEG), lambda i, j: (i, jnp.clip(j - 5, 0, 1))),
        ],
        out_shape=[jax.ShapeDtypeStruct((n, N_SEG * SEG), BF16),
                   jax.ShapeDtypeStruct((n, 2 * SEG), F32)],
        scratch_shapes=[pltpu.VMEM((tm, d), BF16), pltpu.VMEM((tm, SEG), F32)],
        compiler_params=pltpu.CompilerParams(dimension_semantics=("arbitrary", "arbitrary"),
                                             vmem_limit_bytes=VMEM_LIMIT),
        name="inproj",
    )(x2d, sc1, sh1, g_mix, w_in_bf, qk_gain, lbs, g64)


def _t5_bias_tile(pos_q, pos_k, table):
    half = REL_BUCKETS // 2
    max_exact = half // 2
    rel = pos_k - pos_q
    n = jnp.abs(rel)
    nf = jnp.maximum(n, 1).astype(F32)
    large = max_exact + (jnp.log(nf / max_exact) / math.log(REL_MAX_DIST / max_exact)
                         * (half - max_exact)).astype(I32)
    large = jnp.minimum(large, half - 1)
    bucket = jnp.where(rel > 0, half, 0) + jnp.where(n < max_exact, n, large)
    rows = bucket.shape[0]
    tbl = jnp.broadcast_to(table, (rows, LANES))
    cols = [jnp.take_along_axis(tbl, bucket[:, c * LANES:(c + 1) * LANES], axis=1)
            for c in range(bucket.shape[1] // LANES)]
    return jnp.concatenate(cols, axis=1)


def _attn_kernel(smin_ref, smax_ref, q_ref, k_ref, v_ref, posq_ref, posk_ref, rbt_ref, rb_ref, lam_ref,
                 gsub_ref, o_ref, vt_scr, s_a, s_b, cm_a, cm_b, p_scr, m_scr, a_scr):
    b = pl.program_id(0)
    h = pl.program_id(1)
    i = pl.program_id(2)
    nt = pl.num_programs(2)
    t = T_ATT
    n_sub = t // T_SUB
    n_pairs = k_ref.shape[0] // (2 * t)

    @pl.when(i == 0)
    def _():
        ones_row = jnp.where(lax.broadcasted_iota(I32, (V_PAD, t), 0) == 0, 1.0, 0.0).astype(BF16)

        def tr(c, carry):
            r0 = pl.multiple_of(c * t, t)
            vt_scr[c, :HEAD_DIM, :] = v_ref[pl.ds(r0, t), :].astype(F32).T.astype(BF16)
            vt_scr[c, HEAD_DIM:, :] = ones_row
            return carry

        lax.fori_loop(0, nt, tr, 0)

    q = q_ref[...]
    qs = (q[:, :QK_DIM], q[:, QK_DIM:])
    m_scr[...] = jnp.full(m_scr.shape, NEG_BIG, F32)
    a_scr[...] = jnp.zeros(a_scr.shape, F32)

    sub0 = (b * nt + i) * n_sub
    q_lo = smin_ref[sub0]
    q_hi = smax_ref[sub0]
    for u in range(1, n_sub):
        q_lo = jnp.minimum(q_lo, smin_ref[sub0 + u])
        q_hi = jnp.maximum(q_hi, smax_ref[sub0 + u])
    c_pos = rb_ref[h, REL_BUCKETS - 1]
    c_neg = rb_ref[h, REL_BUCKETS // 2 - 1]
    pos_q = posq_ref[0, 0]

    def classify(j, u):
        ksub = (b * nt + j) * n_sub + u
        lo = smin_ref[ksub] - q_hi
        hi = smax_ref[ksub] - q_lo
        far = (lo >= REL_MAX_DIST) | (hi <= -REL_MAX_DIST)
        shift = jnp.where(lo >= REL_MAX_DIST, c_pos, jnp.where(hi <= -REL_MAX_DIST, c_neg, 0.0))
        return far, shift

    def scores(j, s_ref, cm_ref):
        kk = k_ref[pl.ds(pl.multiple_of(j * t, t), t), :]
        ks = (kk[:, :QK_DIM], kk[:, QK_DIM:])
        for mp in range(2):
            s = lax.dot_general(ks[mp], qs[mp], NT_DIMS, preferred_element_type=F32)
            s_ref[mp] = s
            for u in range(n_sub):
                _, shift = classify(j, u)
                cm_ref[mp, u] = jnp.max(s[u * T_SUB:(u + 1) * T_SUB], axis=0, keepdims=True) + shift

    def fixup(j, s_ref, cm_ref):
        for u in range(n_sub):
            far, _ = classify(j, u)

            @pl.when(jnp.logical_not(far))
            def _():
                rows = pl.ds(u * T_SUB, T_SUB)
                pos_k = posk_ref[0, pl.ds(pl.multiple_of(j * t + u * T_SUB, T_SUB), T_SUB), :]
                bias = _t5_bias_tile(pos_q, pos_k, rbt_ref[pl.ds(h, 1), :])
                for mp in range(2):
                    sb = s_ref[mp, rows, :] + bias
                    s_ref[mp, rows, :] = sb
                    cm_ref[mp, u] = jnp.max(sb, axis=0, keepdims=True)

    def consume(j, s_ref, cm_ref):
        vt = vt_scr[j]
        for mp in range(2):
            m_old = m_scr[mp]
            m_new = m_old
            for u in range(n_sub):
                m_new = jnp.maximum(m_new, cm_ref[mp, u])
            for u in range(n_sub):
                rows = pl.ds(u * T_SUB, T_SUB)
                _, shift = classify(j, u)
                p_scr[mp, rows, :] = jnp.exp2(s_ref[mp, rows, :] - (m_new - shift)).astype(BF16)
            m_scr[mp] = m_new
            a_scr[mp] = (jnp.exp2(m_old - m_new) * a_scr[mp]
                         + jnp.dot(vt, p_scr[mp], preferred_element_type=F32))

    def step(j, cur, nxt, prefetch):
        if prefetch:
            scores(j + 1, *nxt)
        consume(j, *cur)
        if prefetch:
            fixup(j + 1, *nxt)

    buf_a = (s_a, cm_a)
    buf_b = (s_b, cm_b)
    scores(0, *buf_a)
    fixup(0, *buf_a)

    def body(jj, carry):
        step(2 * jj, buf_a, buf_b, True)
        step(2 * jj + 1, buf_b, buf_a, True)
        return carry

    lax.fori_loop(0, n_pairs - 1, body, 0)
    last = 2 * (n_pairs - 1)
    step(last, buf_a, buf_b, True)
    step(last + 1, buf_b, buf_a, False)

    num = [a_scr[mp, :HEAD_DIM, :] / a_scr[mp, HEAD_DIM:HEAD_DIM + 1, :] for mp in range(2)]
    o = num[0] - lam_ref[0] * num[1]
    ms = jnp.mean(o * o, axis=0, keepdims=True)
    o = o * lax.rsqrt(ms + EPS) * (gsub_ref[...] * (1.0 - LAM_INIT))
    o_ref[...] = o.T.astype(BF16)


def _attention(p, smin, smax, posq, posk, rb_tab, rb_t, lam, g_sub, batch, seq):
    t = T_ATT
    nt = seq // t
    assert nt % 2 == 0
    n = batch * seq
    n_sub = t // T_SUB
    va = HEAD_DIM + V_PAD
    grid_spec = pltpu.PrefetchScalarGridSpec(
        num_scalar_prefetch=2,
        grid=(batch, N_HEADS, nt),
        in_specs=[
            pl.BlockSpec((t, HEAD_DIM), lambda b, h, i, *_: (b * nt + i, h)),
            pl.BlockSpec((seq, HEAD_DIM), lambda b, h, i, *_: (b, N_HEADS + h)),
            pl.BlockSpec((seq, HEAD_DIM), lambda b, h, i, *_: (b, 2 * N_HEADS + h)),
            pl.BlockSpec((1, 1, 1, t), lambda b, h, i, *_: (b, i, 0, 0)),
            pl.BlockSpec((1, seq, 1), lambda b, h, i, *_: (b, 0, 0)),
            pl.BlockSpec((N_HEADS, LANES), lambda b, h, i, *_: (0, 0)),
            pl.BlockSpec(memory_space=pltpu.SMEM),
            pl.BlockSpec(memory_space=pltpu.SMEM),
            pl.BlockSpec((HEAD_DIM, 1), lambda b, h, i, *_: (0, 0)),
        ],
        out_specs=pl.BlockSpec((t, HEAD_DIM), lambda b, h, i, *_: (b * nt + i, h)),
        scratch_shapes=[pltpu.VMEM((nt, va, t), BF16),
                        pltpu.VMEM((2, t, t), F32),
                        pltpu.VMEM((2, t, t), F32),
                        pltpu.VMEM((2, n_sub, 1, t), F32),
                        pltpu.VMEM((2, n_sub, 1, t), F32),
                        pltpu.VMEM((2, t, t), BF16),
                        pltpu.VMEM((2, 1, t), F32),
                        pltpu.VMEM((2, va, t), F32)],
    )
    return pl.pallas_call(
        _attn_kernel,
        grid_spec=grid_spec,
        out_shape=jax.ShapeDtypeStruct((n, N_HEADS * HEAD_DIM), BF16),
        compiler_params=pltpu.CompilerParams(
            dimension_semantics=("arbitrary", "arbitrary", "arbitrary"),
            vmem_limit_bytes=VMEM_LIMIT),
        name="attn",
    )(smin, smax, p, p, p, posq, posk, rb_tab, rb_t, lam, g_sub)


def _hgrn_group(r0, forward, q_ref, v_ref, lf_ref, tri_ref, st):
    c = C_HGRN
    order = range(HGRN_GROUP) if forward else range(HGRN_GROUP - 1, -1, -1)
    rows = [pl.ds(r0 + k * c, c) for k in order]
    tri = tri_ref[...]
    row = lax.broadcasted_iota(I32, (c, c), 0)
    col = lax.broadcasted_iota(I32, (c, c), 1)
    keep = (col <= row) if forward else (col >= row)

    gs = [lf_ref[r, :] for r in rows]
    bsums = []
    for g in gs:
        g_hi = g.astype(BF16)
        g_lo = (g - g_hi.astype(F32)).astype(BF16)
        bsums.append(jnp.dot(tri, g_hi, preferred_element_type=F32)
                     + jnp.dot(tri, g_lo, preferred_element_type=F32))

    q_in, q_t, k_t, k_st, v_t, vs, decay = [], [], [], [], [], [], []
    for r, g, bsum in zip(rows, gs, bsums):
        if forward:
            ref = bsum[c // 2 - 1:c // 2, :]
            b_end = bsum[c - 1:c, :]
        else:
            ref = bsum[c // 2:c // 2 + 1, :]
            b_end = bsum[0:1, :]
        q = q_ref[r, :].astype(F32)
        v = v_ref[r, :]
        kf = 1.0 - jnp.exp(g)
        q_in.append((q * jnp.exp(bsum)).astype(BF16))
        q_t.append((q * jnp.exp(jnp.minimum(bsum - ref, EXP_CLAMP))).astype(BF16))
        k_t.append((kf * jnp.exp(jnp.minimum(ref - bsum, EXP_CLAMP))).astype(BF16))
        k_st.append((kf * jnp.exp(b_end - bsum)).astype(BF16))
        v_t.append(v.astype(F32).T.astype(BF16))
        vs.append(v)
        decay.append(jnp.exp(b_end))

    scores = [lax.dot_general(a, b, NT_DIMS, preferred_element_type=F32) for a, b in zip(q_t, k_t)]
    st_add = [jnp.dot(a, b, preferred_element_type=F32) for a, b in zip(v_t, k_st)]
    intra = [jnp.dot(jnp.where(keep, s, 0.0).astype(BF16), v, preferred_element_type=F32)
             for s, v in zip(scores, vs)]

    outs = []
    for k in range(HGRN_GROUP):
        o = lax.dot_general(q_in[k], st.astype(BF16), NT_DIMS, preferred_element_type=F32) + intra[k]
        st = st * decay[k] + st_add[k]
        outs.append((rows[k], o))
    return outs, st


def _hgrn_kernel(q_ref, v_ref, g_ref, lff_ref, lfb_ref, tril_ref, triu_ref, gh_ref, o_ref, of_scr):
    rows_per_group = C_HGRN * HGRN_GROUP
    n_groups = q_ref.shape[0] // rows_per_group
    st0 = jnp.zeros((HEAD_DIM, HEAD_DIM), F32)

    def fwd(gi, st):
        r0 = pl.multiple_of(gi * rows_per_group, rows_per_group)
        outs, st = _hgrn_group(r0, True, q_ref, v_ref, lff_ref, tril_ref, st)
        for r, o in outs:
            of_scr[r, :] = o
        return st

    lax.fori_loop(0, n_groups, fwd, st0)

    def bwd(gi, st):
        r0 = pl.multiple_of((n_groups - 1 - gi) * rows_per_group, rows_per_group)
        outs, st = _hgrn_group(r0, False, q_ref, v_ref, lfb_ref, triu_ref, st)
        for r, o in outs:
            o = of_scr[r, :] + o
            ms = jnp.mean(o * o, axis=-1, keepdims=True)
            y = o * lax.rsqrt(ms + EPS) * gh_ref[...] * g_ref[r, :].astype(F32)
            o_ref[r, :] = y.astype(BF16)
        return st

    lax.fori_loop(0, n_groups, bwd, st0)


def _hgrn(p, lf, tril, triu, g_hgrn, batch, seq):
    n = batch * seq
    c = C_HGRN
    blk = lambda off: pl.BlockSpec((seq, HEAD_DIM), lambda b, h: (b, off + h))
    return pl.pallas_call(
        _hgrn_kernel,
        grid=(batch, N_HEADS),
        in_specs=[blk(3 * N_HEADS), blk(4 * N_HEADS), blk(7 * N_HEADS), blk(0), blk(N_HEADS),
                  pl.BlockSpec((c, c), lambda b, h: (0, 0)),
                  pl.BlockSpec((c, c), lambda b, h: (0, 0)),
                  pl.BlockSpec((1, HEAD_DIM), lambda b, h: (0, 0))],
        out_specs=pl.BlockSpec((seq, HEAD_DIM), lambda b, h: (b, h)),
        out_shape=jax.ShapeDtypeStruct((n, N_HEADS * HEAD_DIM), BF16),
        scratch_shapes=[pltpu.VMEM((seq, HEAD_DIM), F32)],
        compiler_params=pltpu.CompilerParams(dimension_semantics=("arbitrary", "arbitrary"),
                                             vmem_limit_bytes=VMEM_LIMIT),
        name="hgrn",
    )(p, p, p, lf, lf, tril, triu, g_hgrn)


def _col_max(x):
    return jnp.max(x, axis=0, keepdims=True)


def _outproj_kernel(oa_ref, oh_ref, wa_ref, wb_ref, x_ref, gt_ref, gffn_ref, sc_ref, sh_ref,
                    wr_hi_ref, wr_lo_ref, rbias_ref, upper_ref, ones_ref,
                    x1_ref, hp_ref, eidx_ref, slot_ref, gate_ref, cnt_ref, cnt_scr):
    i = pl.program_id(0)
    tm = x_ref.shape[0]

    @pl.when(i == 0)
    def _():
        cnt_scr[...] = jnp.zeros(cnt_scr.shape, F32)

    acc = (jnp.dot(oa_ref[...], wa_ref[...], preferred_element_type=F32)
           + jnp.dot(oh_ref[...], wb_ref[...], preferred_element_type=F32))
    x1 = x_ref[...] + gt_ref[0] * acc
    x1_ref[...] = x1
    ms = jnp.mean(x1 * x1, axis=-1, keepdims=True)
    h2 = x1 * lax.rsqrt(ms + EPS) * gffn_ref[...] * (1.0 + sc_ref[0]) + sh_ref[0]
    half = h2.shape[1] // 2
    hp_ref[...] = _pack_pair(h2[:, :half], h2[:, half:])

    h_hi = h2.astype(BF16)
    h_lo = (h2 - h_hi.astype(F32)).astype(BF16)
    wr_hi = wr_hi_ref[...]
    logits = (lax.dot_general(wr_hi, h_hi, NT_DIMS, preferred_element_type=F32)
              + lax.dot_general(wr_hi, h_lo, NT_DIMS, preferred_element_type=F32)
              + lax.dot_general(wr_lo_ref[...], h_hi, NT_DIMS, preferred_element_type=F32))
    scores = jax.nn.sigmoid(logits)
    biased = scores + rbias_ref[...]

    gs = []
    for g in range(N_GROUPS):
        blk = biased[g * GROUP_SIZE:(g + 1) * GROUP_SIZE, :]
        top1 = _col_max(blk)
        eq = blk == top1
        n_eq = jnp.sum(eq.astype(F32), axis=0, keepdims=True)
        second = _col_max(jnp.where(eq, -jnp.inf, blk))
        gs.append(top1 + jnp.where(n_eq > 1.0, top1, second))
    gsm = jnp.concatenate(gs, axis=0)
    giota = lax.broadcasted_iota(I32, gsm.shape, 0)
    gsel = jnp.zeros(gsm.shape, F32)
    for _ in range(TOPK_GROUPS):
        top = _col_max(gsm)
        idx = jnp.min(jnp.where(gsm == top, giota, N_GROUPS), axis=0, keepdims=True)
        pick = giota == idx
        gsel = jnp.where(pick, 1.0, gsel)
        gsm = jnp.where(pick, -jnp.inf, gsm)
    emask = jnp.concatenate(
        [jnp.broadcast_to(gsel[g:g + 1, :], (GROUP_SIZE, tm)) for g in range(N_GROUPS)], axis=0)
    masked = jnp.where(emask > 0.5, biased, -jnp.inf)

    eiota = lax.broadcasted_iota(I32, masked.shape, 0)
    idxs, gates = [], []
    for _ in range(TOP_K):
        top = _col_max(masked)
        idx = jnp.min(jnp.where(masked == top, eiota, N_EXPERTS), axis=0, keepdims=True)
        pick = eiota == idx
        gates.append(jnp.sum(jnp.where(pick, scores, 0.0), axis=0, keepdims=True))
        idxs.append(idx)
        masked = jnp.where(pick, -jnp.inf, masked)
    gate = jnp.concatenate(gates, axis=0)
    gate = gate / jnp.sum(gate, axis=0, keepdims=True) * ROUTED_SCALE
    eidx = jnp.concatenate(idxs, axis=0)
    eidx_ref[...] = eidx
    gate_ref[...] = gate

    sel = jnp.zeros(masked.shape, F32)
    for k in range(TOP_K):
        sel = jnp.where(eiota == idxs[k], 1.0, sel)
    sel_bf = sel.astype(BF16)
    rank = jnp.dot(sel_bf, upper_ref[...], preferred_element_type=F32)
    base = cnt_scr[...]
    posn = base[:, :1] + rank
    slots = [jnp.sum(jnp.where(eiota == idxs[k], posn, 0.0), axis=0, keepdims=True)
             for k in range(TOP_K)]
    slot_ref[...] = jnp.concatenate(slots, axis=0).astype(I32)
    new_cnt = base + jnp.dot(sel_bf, ones_ref[...], preferred_element_type=F32)
    cnt_scr[...] = new_cnt
    cnt_ref[...] = new_cnt


def _outproj(oa, oh, wa, wb, x2d, gt1, g_ffn, sc2, sh2, wr_hi, wr_lo, rbias, upper, ones, seq):
    n, d = x2d.shape
    tm = TM_OUT
    tiles_per_batch = seq // tm
    half = d // 2
    row = lambda w: pl.BlockSpec((tm, w), lambda i: (i, 0))
    const = lambda shape: pl.BlockSpec(shape, lambda i: tuple(0 for _ in shape))
    per_batch = pl.BlockSpec((1, 1, d), lambda i: (i // tiles_per_batch, 0, 0))
    tok = pl.BlockSpec((TOP_K, tm), lambda i: (0, i))
    return pl.pallas_call(
        _outproj_kernel,
        grid=(n // tm,),
        in_specs=[row(half), row(half), const((half, d)), const((half, d)), row(d), per_batch,
                  const((1, d)), per_batch, per_batch,
                  const((N_EXPERTS, d)), const((N_EXPERTS, d)), const((N_EXPERTS, 1)),
                  const((tm, tm)), const((tm, LANES))],
        out_specs=[row(d), row(half), tok, tok, tok, const((N_EXPERTS, LANES))],
        out_shape=[jax.ShapeDtypeStruct((n, d), F32),
                   jax.ShapeDtypeStruct((n, half), U32),
                   jax.ShapeDtypeStruct((TOP_K, n), I32),
                   jax.ShapeDtypeStruct((TOP_K, n), I32),
                   jax.ShapeDtypeStruct((TOP_K, n), F32),
                   jax.ShapeDtypeStruct((N_EXPERTS, LANES), F32)],
        scratch_shapes=[pltpu.VMEM((N_EXPERTS, LANES), F32)],
        compiler_params=pltpu.CompilerParams(dimension_semantics=("arbitrary",),
                                             vmem_limit_bytes=VMEM_LIMIT),
        name="outproj",
    )(oa, oh, wa, wb, x2d, gt1, g_ffn, sc2, sh2, wr_hi, wr_lo, rbias, upper, ones)


def _scatter_kernel(pstart_ref, pfill_ref, pend_ref, nv_ref, e_ref, slot_ref, h_ref, xs_ref,
                    zero_scr, sem, zsem):
    i = pl.program_id(0)
    ts = h_ref.shape[0]
    tb = zero_scr.shape[0]
    n_tail = xs_ref.shape[0] // tb - nv_ref[0]

    def pad_fill(e, wait):
        def go(src, dst):
            cp = pltpu.make_async_copy(src, dst, zsem)
            cp.wait() if wait else cp.start()

        start = pfill_ref[e]
        end = pend_ref[e]
        head = jnp.minimum((-start) & 7, end - start)
        for r in range(7):
            @pl.when(r < head)
            def _():
                go(zero_scr.at[pl.ds(0, 1)], xs_ref.at[pl.ds(start + r, 1)])
        off = start + head
        rem = end - off
        size = tb // 2
        while size >= 8:
            cond = (rem & size) != 0

            @pl.when(cond)
            def _():
                go(zero_scr.at[pl.ds(0, size)], xs_ref.at[pl.ds(pl.multiple_of(off, 8), size)])
            off = off + jnp.where(cond, size, 0)
            size //= 2

    @pl.when(i == 0)
    def _():
        zero_scr[...] = jnp.zeros(zero_scr.shape, U32)

        def fill(e, carry):
            pad_fill(e, False)
            return carry

        lax.fori_loop(0, N_EXPERTS, fill, 0)

        def fill_tail(j, carry):
            start = pl.multiple_of((nv_ref[0] + j) * tb, tb)
            pltpu.make_async_copy(zero_scr, xs_ref.at[pl.ds(start, tb)], zsem).start()
            return carry

        lax.fori_loop(0, n_tail, fill_tail, 0)

        def drain(e, carry):
            pad_fill(e, True)
            return carry

        lax.fori_loop(0, N_EXPERTS, drain, 0)

        def drain_tail(j, carry):
            pltpu.make_async_copy(zero_scr, xs_ref.at[pl.ds(0, tb)], zsem).wait()
            return carry

        lax.fori_loop(0, n_tail, drain_tail, 0)

    def issue(t, carry):
        for k in range(TOP_K):
            dest = pstart_ref[e_ref[k, t]] + slot_ref[k, t]
            pltpu.make_async_copy(h_ref.at[pl.ds(t, 1)], xs_ref.at[pl.ds(dest, 1)], sem).start()
        return carry

    lax.fori_loop(0, ts, issue, 0)
    for k in range(TOP_K):
        pltpu.make_async_copy(h_ref, xs_ref.at[pl.ds(0, ts)], sem).wait()


def _scatter(pstart, pfill, pend, n_valid, eidx, slot, hp, n_rows):
    n, w = hp.shape
    ts = T_ROW
    grid_spec = pltpu.PrefetchScalarGridSpec(
        num_scalar_prefetch=4,
        grid=(n // ts,),
        in_specs=[pl.BlockSpec((TOP_K, ts), lambda i, *_: (0, i), memory_space=pltpu.SMEM),
                  pl.BlockSpec((TOP_K, ts), lambda i, *_: (0, i), memory_space=pltpu.SMEM),
                  pl.BlockSpec((ts, w), lambda i, *_: (i, 0))],
        out_specs=pl.BlockSpec(memory_space=pl.ANY),
        scratch_shapes=[pltpu.VMEM((TB_EXP, w), U32), pltpu.SemaphoreType.DMA,
                        pltpu.SemaphoreType.DMA],
    )
    return pl.pallas_call(
        _scatter_kernel,
        grid_spec=grid_spec,
        out_shape=jax.ShapeDtypeStruct((n_rows, w), U32),
        compiler_params=pltpu.CompilerParams(dimension_semantics=("arbitrary",),
                                             vmem_limit_bytes=VMEM_LIMIT),
        name="scatter",
    )(pstart, pfill, pend, n_valid, eidx, slot, hp)


def _experts_kernel(be_ref, nv_ref, xs_ref, wg_ref, wu_ref, wd_ref, y_ref, wg_bf, wu_bf, wd_bf):
    i = pl.program_id(0)
    prev = jnp.maximum(i - 1, 0)
    valid = i < nv_ref[0]
    fresh = valid & ((i == 0) | (be_ref[i] != be_ref[prev]))

    @pl.when(fresh)
    def _():
        wg_bf[...] = wg_ref[0].astype(BF16)
        wu_bf[...] = wu_ref[0].astype(BF16)
        wd_bf[...] = wd_ref[0].astype(BF16)

    @pl.when(valid)
    def _():
        lo, hi = _unpack_pair(xs_ref[...])
        x = jnp.concatenate([lo.astype(BF16), hi.astype(BF16)], axis=1)
        hg = jnp.dot(x, wg_bf[...], preferred_element_type=F32)
        hu = jnp.dot(x, wu_bf[...], preferred_element_type=F32)
        a = (_silu(hg) * hu).astype(BF16)
        y = jnp.dot(a, wd_bf[...], preferred_element_type=F32)
        half = y.shape[1] // 2
        y_ref[...] = _pack_pair(y[:, :half], y[:, half:])

    @pl.when(jnp.logical_not(valid))
    def _():
        y_ref[...] = jnp.zeros(y_ref.shape, U32)


def _experts(block_e, n_valid, xs, w_gate, w_up, w_down, n_blocks):
    tb = TB_EXP
    w = xs.shape[1]
    _, d, f = w_gate.shape

    def blk(i, be, nv):
        return jnp.minimum(i, nv[0] - 1)

    grid_spec = pltpu.PrefetchScalarGridSpec(
        num_scalar_prefetch=2,
        grid=(n_blocks,),
        in_specs=[pl.BlockSpec((tb, w), lambda i, be, nv: (blk(i, be, nv), 0)),
                  pl.BlockSpec((1, d, f), lambda i, be, nv: (be[blk(i, be, nv)], 0, 0)),
                  pl.BlockSpec((1, d, f), lambda i, be, nv: (be[blk(i, be, nv)], 0, 0)),
                  pl.BlockSpec((1, f, d), lambda i, be, nv: (be[blk(i, be, nv)], 0, 0))],
        out_specs=pl.BlockSpec((tb, w), lambda i, be, nv: (i, 0)),
        scratch_shapes=[pltpu.VMEM((d, f), BF16), pltpu.VMEM((d, f), BF16), pltpu.VMEM((f, d), BF16)],
    )
    return pl.pallas_call(
        _experts_kernel,
        grid_spec=grid_spec,
        out_shape=jax.ShapeDtypeStruct((n_blocks * tb, w), U32),
        compiler_params=pltpu.CompilerParams(dimension_semantics=("arbitrary",),
                                             vmem_limit_bytes=VMEM_LIMIT),
        name="experts",
    )(block_e, n_valid, xs, w_gate, w_up, w_down)


def _combine_kernel(pstart_ref, e_ref, slot_ref, x1_ref, hp_ref, gate_ref, gt_ref,
                    wsg_ref, wsu_ref, wsd_ref, y_ref, o_ref, buf, sem):
    tc = x1_ref.shape[0]

    def issue(t, carry):
        for k in range(TOP_K):
            src = pstart_ref[e_ref[k, t]] + slot_ref[k, t]
            pltpu.make_async_copy(y_ref.at[pl.ds(src, 1)], buf.at[k, pl.ds(t, 1)], sem).start()
        return carry

    lax.fori_loop(0, tc, issue, 0)

    lo, hi = _unpack_pair(hp_ref[...])
    x = jnp.concatenate([lo.astype(BF16), hi.astype(BF16)], axis=1)
    hg = jnp.dot(x, wsg_ref[...], preferred_element_type=F32)
    hu = jnp.dot(x, wsu_ref[...], preferred_element_type=F32)
    a = (_silu(hg) * hu).astype(BF16)
    shared = jnp.dot(a, wsd_ref[...], preferred_element_type=F32)

    for k in range(TOP_K):
        pltpu.make_async_copy(y_ref.at[pl.ds(0, tc)], buf.at[k], sem).wait()

    half = shared.shape[1] // 2
    gate = gate_ref[...]
    r_lo = shared[:, :half]
    r_hi = shared[:, half:]
    for k in range(TOP_K):
        lo, hi = _unpack_pair(buf[k])
        gk = gate[:, k:k + 1]
        r_lo = r_lo + gk * lo
        r_hi = r_hi + gk * hi
    gt = gt_ref[0]
    o_ref[:, :half] = x1_ref[:, :half] + gt[:, :half] * r_lo
    o_ref[:, half:] = x1_ref[:, half:] + gt[:, half:] * r_hi


def _combine(pstart, eidx, slot, x1, hp, gate_t, gt2, wsg, wsu, wsd, y, seq):
    n, d = x1.shape
    tc = T_ROW
    w = hp.shape[1]
    f = wsg.shape[1]
    tiles_per_batch = seq // tc
    grid_spec = pltpu.PrefetchScalarGridSpec(
        num_scalar_prefetch=1,
        grid=(n // tc,),
        in_specs=[pl.BlockSpec((TOP_K, tc), lambda i, *_: (0, i), memory_space=pltpu.SMEM),
                  pl.BlockSpec((TOP_K, tc), lambda i, *_: (0, i), memory_space=pltpu.SMEM),
                  pl.BlockSpec((tc, d), lambda i, *_: (i, 0)),
                  pl.BlockSpec((tc, w), lambda i, *_: (i, 0)),
                  pl.BlockSpec((tc, TOP_K), lambda i, *_: (i, 0)),
                  pl.BlockSpec((1, 1, d), lambda i, *_: (i // tiles_per_batch, 0, 0)),
                  pl.BlockSpec((d, f), lambda i, *_: (0, 0)),
                  pl.BlockSpec((d, f), lambda i, *_: (0, 0)),
                  pl.BlockSpec((f, d), lambda i, *_: (0, 0)),
                  pl.BlockSpec(memory_space=pl.ANY)],
        out_specs=pl.BlockSpec((tc, d), lambda i, *_: (i, 0)),
        scratch_shapes=[pltpu.VMEM((TOP_K, tc, w), U32), pltpu.SemaphoreType.DMA],
    )
    return pl.pallas_call(
        _combine_kernel,
        grid_spec=grid_spec,
        out_shape=jax.ShapeDtypeStruct((n, d), F32),
        compiler_params=pltpu.CompilerParams(dimension_semantics=("arbitrary",),
                                             vmem_limit_bytes=VMEM_LIMIT),
        name="combine",
    )(pstart, eidx, slot, x1, hp, gate_t, gt2, wsg, wsu, wsd, y)


def kernel(x, c, positions, rel_bias, hgrn_lb_logits, w_ada, b_ada, g_mix, w_in, g_q, g_k, lam_q1, lam_k1, lam_q2, lam_k2, g_sub, g_hgrn, w_out, g_ffn, w_router, router_bias, w_exp_gate, w_exp_up, w_exp_down, w_sh_gate, w_sh_up, w_sh_down):
    batch, seq, d = x.shape
    n = batch * seq
    layer = 0
    x2d = x.reshape(n, d)

    c_pad = jnp.zeros((8, d), F32).at[:batch].set(c.astype(F32))
    mod = _ada(c_pad, w_ada[layer], b_ada[layer][None, :])[:batch]
    sh1, sc1, gt1, sh2, sc2, gt2 = [m.reshape(batch, 1, d) for m in jnp.split(mod, 6, axis=-1)]

    lbs = jnp.cumsum(jax.nn.softmax(hgrn_lb_logits.astype(F32), axis=1), axis=1)[:, layer]
    lbs = lbs.reshape(2, 1, SEG)
    reps = SEG // QK_DIM
    qk_gain = jnp.stack([jnp.tile(g_q[layer].astype(F32), reps) * (QK_DIM ** -0.5 * LOG2E),
                         jnp.tile(g_k[layer].astype(F32), reps)]).reshape(2, 1, SEG)
    lane = jnp.arange(LANES)
    g64 = jnp.where((lane[:, None] // QK_DIM) == (lane[None, :] // QK_DIM), 1.0 / QK_DIM, 0.0).astype(BF16)
    lam = (jnp.exp(jnp.sum(lam_q1[layer].astype(F32) * lam_k1[layer].astype(F32)))
           - jnp.exp(jnp.sum(lam_q2[layer].astype(F32) * lam_k2[layer].astype(F32)))
           + LAM_INIT).reshape(1)

    p, lf = _inproj(x2d, sc1, sh1, g_mix[layer][None, :], w_in[layer].astype(BF16), qk_gain, lbs, g64, seq)

    nt = seq // T_ATT
    pos_sub = positions.astype(I32).reshape(batch * seq // T_SUB, T_SUB)
    smin = jnp.min(pos_sub, axis=1)
    smax = jnp.max(pos_sub, axis=1)
    posq = positions.astype(I32).reshape(batch, nt, 1, T_ATT)
    posk = positions.astype(I32).reshape(batch, seq, 1)
    rb_t = rel_bias.astype(F32).T * LOG2E
    rb_tab = jnp.zeros((N_HEADS, LANES), F32).at[:, :REL_BUCKETS].set(rb_t)
    oa = _attention(p, smin, smax, posq, posk, rb_tab, rb_t, lam,
                    g_sub[layer][:, None].astype(F32), batch, seq)

    ci = jnp.arange(C_HGRN)
    tril = (ci[None, :] <= ci[:, None]).astype(BF16)
    triu = (ci[None, :] >= ci[:, None]).astype(BF16)
    oh = _hgrn(p, lf, tril, triu, g_hgrn[layer][None, :].astype(F32), batch, seq)

    half = d // 2
    w_out_bf = w_out[layer].astype(BF16)
    wr_t = w_router[layer].astype(F32).T
    wr_hi = wr_t.astype(BF16)
    wr_lo = (wr_t - wr_hi.astype(F32)).astype(BF16)
    ti = jnp.arange(TM_OUT)
    upper = (ti[:, None] < ti[None, :]).astype(BF16)
    ones = jnp.ones((TM_OUT, LANES), BF16)
    x1, hp, eidx, slot, gate, cnt = _outproj(
        oa, oh, w_out_bf[:half], w_out_bf[half:], x2d, gt1, g_ffn[layer][None, :], sc2, sh2,
        wr_hi, wr_lo, router_bias[layer].astype(F32)[:, None], upper, ones, seq)

    tb = TB_EXP
    counts = cnt[:, 0].astype(I32)
    padded = (counts + tb - 1) // tb * tb
    pends = jnp.cumsum(padded)
    pstart = (pends - padded).astype(I32)
    n_blocks = (n * TOP_K) // tb + N_EXPERTS
    n_valid = (pends[-1] // tb).astype(I32).reshape(1)
    block_e = jnp.minimum(jnp.searchsorted(pends, jnp.arange(n_blocks, dtype=I32) * tb, side='right'),
                          N_EXPERTS - 1).astype(I32)
    pfill = (pstart + counts).astype(I32)
    pend = pends.astype(I32)

    xs = _scatter(pstart, pfill, pend, n_valid, eidx, slot, hp, n_blocks * tb)
    y = _experts(block_e, n_valid, xs, w_exp_gate[layer], w_exp_up[layer], w_exp_down[layer], n_blocks)
    out = _combine(pstart, eidx, slot, x1, hp, gate.T, gt2,
                   w_sh_gate[layer].astype(BF16), w_sh_up[layer].astype(BF16),
                   w_sh_down[layer].astype(BF16), y, seq)
    return out.reshape(batch, seq, d)
```

```python
import functools
import math

import jax
import jax.numpy as jnp
from jax import lax
from jax.experimental import pallas as pl
from jax.experimental.pallas import tpu as pltpu

F32 = jnp.float32
BF16 = jnp.bfloat16
I32 = jnp.int32
U32 = jnp.uint32

D_MODEL = 2048
N_HEADS = 8
QK_DIM = 64
HEAD_DIM = 128
SEG = 1024
N_SEG = 8
REL_BUCKETS = 32
REL_MAX_DIST = 128
N_EXPERTS = 256
TOP_K = 8
N_GROUPS = 8
TOPK_GROUPS = 4
GROUP_SIZE = N_EXPERTS // N_GROUPS
EXPERT_DIM = 512
ROUTED_SCALE = 2.5
EPS = 1e-6
LAM_INIT = 0.8 - 0.6 * math.exp(-0.3 * 0)
LOG2E = math.log2(math.e)

LANES = 128
VMEM_LIMIT = 56 * 1024 * 1024

TM_IN = 512
T_ATT = 512
T_KEY = 1024
T_SUB = 128
V_PAD = 16
C_HGRN = 64
HGRN_GROUP = 8
TM_OUT = 256
T_ROW = 256
TB_EXP = 256
W_SLOTS = 2
H_SLOTS = 3
NEG_BIG = -1e30
EXP_CLAMP = 80.0

NT_DIMS = (((1,), (1,)), ((), ()))


def _silu(x):
    return x * jax.nn.sigmoid(x)


def _pack_pair(lo_f32, hi_f32):
    lo = lax.bitcast_convert_type(lo_f32.astype(BF16).astype(F32), U32)
    hi = lax.bitcast_convert_type(hi_f32.astype(BF16).astype(F32), U32)
    return (hi & jnp.uint32(0xFFFF0000)) | (lo >> 16)


def _unpack_pair(word):
    lo = lax.bitcast_convert_type(word << 16, F32)
    hi = lax.bitcast_convert_type(word & jnp.uint32(0xFFFF0000), F32)
    return lo, hi


def _ada_kernel(c_ref, w_ref, b_ref, o_ref):
    a = _silu(c_ref[...]).astype(BF16)
    o_ref[...] = jnp.dot(a, w_ref[...].astype(BF16), preferred_element_type=F32) + b_ref[...]


def _ada(c_pad, w, b):
    d, n = w.shape
    tn = 1024
    return pl.pallas_call(
        _ada_kernel,
        grid=(n // tn,),
        in_specs=[pl.BlockSpec((8, d), lambda j: (0, 0)),
                  pl.BlockSpec((d, tn), lambda j: (0, j)),
                  pl.BlockSpec((1, tn), lambda j: (0, j))],
        out_specs=pl.BlockSpec((8, tn), lambda j: (0, j)),
        out_shape=jax.ShapeDtypeStruct((8, n), F32),
        compiler_params=pltpu.CompilerParams(dimension_semantics=("arbitrary",),
                                             vmem_limit_bytes=VMEM_LIMIT),
        name="ada",
    )(c_pad, w, b)


def _inproj_kernel(x_ref, sc_ref, sh_ref, gmix_ref, w_ref, qkg_ref, lb_ref, g64_ref,
                   p_ref, lf_ref, h_scr, acc_scr):
    j = pl.program_id(1)

    @pl.when(j == 0)
    def _():
        x = x_ref[...]
        ms = jnp.mean(x * x, axis=-1, keepdims=True)
        y = x * lax.rsqrt(ms + EPS) * gmix_ref[...]
        h_scr[...] = (y * (1.0 + sc_ref[0]) + sh_ref[0]).astype(BF16)

    acc_scr[...] = jnp.dot(h_scr[...], w_ref[...], preferred_element_type=F32)

    @pl.when(j < 2)
    def _():
        gain = qkg_ref[0]
        for c in range(SEG // LANES):
            sl = slice(c * LANES, (c + 1) * LANES)
            xs = acc_scr[:, sl]
            ms = jnp.dot((xs * xs).astype(BF16), g64_ref[...], preferred_element_type=F32)
            p_ref[:, sl] = (xs * lax.rsqrt(ms + EPS) * gain[:, sl]).astype(BF16)

    @pl.when((j == 2) | (j == 4))
    def _():
        p_ref[...] = acc_scr[...].astype(BF16)

    @pl.when((j == 3) | (j == 7))
    def _():
        p_ref[...] = _silu(acc_scr[...]).astype(BF16)

    @pl.when((j == 5) | (j == 6))
    def _():
        z = acc_scr[...]
        lb = lb_ref[0]
        f = lb + (1.0 - lb) * jax.nn.sigmoid(z)
        lf_ref[...] = jnp.log(f)
        p_ref[...] = z.astype(BF16)


def _inproj(x2d, sc1, sh1, g_mix, w_in_bf, qk_gain, lbs, g64, seq):
    n, d = x2d.shape
    tm = TM_IN
    tiles_per_batch = seq // tm
    return pl.pallas_call(
        _inproj_kernel,
        grid=(n // tm, N_SEG),
        in_specs=[
            pl.BlockSpec((tm, d), lambda i, j: (i, 0)),
            pl.BlockSpec((1, 1, d), lambda i, j: (i // tiles_per_batch, 0, 0)),
            pl.BlockSpec((1, 1, d), lambda i, j: (i // tiles_per_batch, 0, 0)),
            pl.BlockSpec((1, d), lambda i, j: (0, 0)),
            pl.BlockSpec((d, SEG), lambda i, j: (0, j)),
            pl.BlockSpec((1, 1, SEG), lambda i, j: (jnp.minimum(j, 1), 0, 0)),
            pl.BlockSpec((1, 1, SEG), lambda i, j: (jnp.clip(j - 5, 0, 1), 0, 0)),
            pl.BlockSpec((LANES, LANES), lambda i, j: (0, 0)),
        ],
        out_specs=[
            pl.BlockSpec((tm, SEG), lambda i, j: (i, j)),
            pl.BlockSpec((tm, SEG), lambda i, j: (i, jnp.clip(j - 5, 0, 1))),
        ],
        out_shape=[jax.ShapeDtypeStruct((n, N_SEG * SEG), BF16),
                   jax.ShapeDtypeStruct((n, 2 * SEG), F32)],
        scratch_shapes=[pltpu.VMEM((tm, d), BF16), pltpu.VMEM((tm, SEG), F32)],
        compiler_params=pltpu.CompilerParams(dimension_semantics=("arbitrary", "arbitrary"),
                                             vmem_limit_bytes=VMEM_LIMIT),
        name="inproj",
    )(x2d, sc1, sh1, g_mix, w_in_bf, qk_gain, lbs, g64)


def _t5_bias_tile(pos_q, pos_k, table):
    half = REL_BUCKETS // 2
    max_exact = half // 2
    rel = pos_k - pos_q
    n = jnp.abs(rel)
    nf = jnp.maximum(n, 1).astype(F32)
    large = max_exact + (jnp.log(nf / max_exact) / math.log(REL_MAX_DIST / max_exact)
                         * (half - max_exact)).astype(I32)
    large = jnp.minimum(large, half - 1)
    bucket = jnp.where(rel > 0, half, 0) + jnp.where(n < max_exact, n, large)
    rows = bucket.shape[0]
    tbl = jnp.broadcast_to(table, (rows, LANES))
    cols = [jnp.take_along_axis(tbl, bucket[:, c * LANES:(c + 1) * LANES], axis=1)
            for c in range(bucket.shape[1] // LANES)]
    return jnp.concatenate(cols, axis=1)


def _attn_kernel(smin_ref, smax_ref, q_ref, k_ref, v_ref, posq_ref, posk_ref, rbt_ref, rb_ref, lam_ref,
                 gsub_ref, o_ref, vt_scr, s_a, s_b, cm_a, cm_b, p_a, p_b, al_a, al_b, m_scr, a_scr):
    b = pl.program_id(0)
    h = pl.program_id(1)
    i = pl.program_id(2)
    tq = T_ATT
    tk = T_KEY
    n_sub = tk // T_SUB
    nq_sub = tq // T_SUB
    ntk = k_ref.shape[0] // tk
    n_pairs = ntk // 2
    subs_per_batch = k_ref.shape[0] // T_SUB

    @pl.when(i == 0)
    def _():
        ones_row = jnp.where(lax.broadcasted_iota(I32, (V_PAD, tk), 0) == 0, 1.0, 0.0).astype(BF16)

        def tr(c, carry):
            r0 = pl.multiple_of(c * tk, tk)
            vt_scr[c, :HEAD_DIM, :] = v_ref[pl.ds(r0, tk), :].astype(F32).T.astype(BF16)
            vt_scr[c, HEAD_DIM:, :] = ones_row
            return carry

        lax.fori_loop(0, ntk, tr, 0)

    q = q_ref[...]
    qs = (q[:, :QK_DIM], q[:, QK_DIM:])
    m_scr[...] = jnp.full(m_scr.shape, NEG_BIG, F32)
    a_scr[...] = jnp.zeros(a_scr.shape, F32)

    sub0 = b * subs_per_batch + i * nq_sub
    q_lo = smin_ref[sub0]
    q_hi = smax_ref[sub0]
    for u in range(1, nq_sub):
        q_lo = jnp.minimum(q_lo, smin_ref[sub0 + u])
        q_hi = jnp.maximum(q_hi, smax_ref[sub0 + u])
    c_pos = rb_ref[h, REL_BUCKETS - 1]
    c_neg = rb_ref[h, REL_BUCKETS // 2 - 1]
    pos_q = posq_ref[0, 0]

    def classify(j, u):
        ksub = b * subs_per_batch + j * n_sub + u
        lo = smin_ref[ksub] - q_hi
        hi = smax_ref[ksub] - q_lo
        far = (lo >= REL_MAX_DIST) | (hi <= -REL_MAX_DIST)
        shift = jnp.where(lo >= REL_MAX_DIST, c_pos, jnp.where(hi <= -REL_MAX_DIST, c_neg, 0.0))
        return far, shift

    def scores(j, s_ref, cm_ref):
        kk = k_ref[pl.ds(pl.multiple_of(j * tk, tk), tk), :]
        ks = (kk[:, :QK_DIM], kk[:, QK_DIM:])
        for mp in range(2):
            s = lax.dot_general(ks[mp], qs[mp], NT_DIMS, preferred_element_type=F32)
            s_ref[mp] = s
            for u in range(n_sub):
                _, shift = classify(j, u)
                cm_ref[mp, u] = jnp.max(s[u * T_SUB:(u + 1) * T_SUB], axis=0, keepdims=True) + shift

    def fixup(j, s_ref, cm_ref):
        for u in range(n_sub):
            far, _ = classify(j, u)

            @pl.when(jnp.logical_not(far))
            def _():
                rows = pl.ds(u * T_SUB, T_SUB)
                pos_k = posk_ref[0, pl.ds(pl.multiple_of(j * tk + u * T_SUB, T_SUB), T_SUB), :]
                bias = _t5_bias_tile(pos_q, pos_k, rbt_ref[pl.ds(h, 1), :])
                for mp in range(2):
                    sb = s_ref[mp, rows, :] + bias
                    s_ref[mp, rows, :] = sb
                    cm_ref[mp, u] = jnp.max(sb, axis=0, keepdims=True)

    def soft(j, s_ref, cm_ref, p_ref, al_ref):
        for mp in range(2):
            m_old = m_scr[mp]
            m_new = m_old
            for u in range(n_sub):
                m_new = jnp.maximum(m_new, cm_ref[mp, u])
            for u in range(n_sub):
                rows = pl.ds(u * T_SUB, T_SUB)
                _, shift = classify(j, u)
                p_ref[mp, rows, :] = jnp.exp2(s_ref[mp, rows, :] - (m_new - shift)).astype(BF16)
            m_scr[mp] = m_new
            al_ref[mp] = jnp.exp2(m_old - m_new)

    def pv(j, p_ref, al_ref):
        vt = vt_scr[j]
        for mp in range(2):
            a_scr[mp] = al_ref[mp] * a_scr[mp] + jnp.dot(vt, p_ref[mp], preferred_element_type=F32)

    sbuf = ((s_a, cm_a), (s_b, cm_b))
    pbuf = ((p_a, al_a), (p_b, al_b))

    def step(j, par):
        scores(j + 2, *sbuf[par])
        soft(j + 1, *sbuf[1 - par], *pbuf[1 - par])
        pv(j, *pbuf[par])
        fixup(j + 2, *sbuf[par])

    scores(0, *sbuf[0])
    fixup(0, *sbuf[0])
    scores(1, *sbuf[1])
    soft(0, *sbuf[0], *pbuf[0])
    fixup(1, *sbuf[1])

    def body(jj, carry):
        step(2 * jj, 0)
        step(2 * jj + 1, 1)
        return carry

    lax.fori_loop(0, n_pairs - 1, body, 0)
    last = 2 * (n_pairs - 1)
    soft(last + 1, *sbuf[1], *pbuf[1])
    pv(last, *pbuf[0])
    pv(last + 1, *pbuf[1])

    num = [a_scr[mp, :HEAD_DIM, :] / a_scr[mp, HEAD_DIM:HEAD_DIM + 1, :] for mp in range(2)]
    o = num[0] - lam_ref[0] * num[1]
    ms = jnp.mean(o * o, axis=0, keepdims=True)
    o = o * lax.rsqrt(ms + EPS) * (gsub_ref[...] * (1.0 - LAM_INIT))
    o_ref[...] = o.T.astype(BF16)


def _attention(p, smin, smax, posq, posk, rb_tab, rb_t, lam, g_sub, batch, seq):
    t = T_ATT
    tk = T_KEY
    nt = seq // t
    ntk = seq // tk
    assert seq % (2 * tk) == 0
    n = batch * seq
    n_sub = tk // T_SUB
    va = HEAD_DIM + V_PAD
    grid_spec = pltpu.PrefetchScalarGridSpec(
        num_scalar_prefetch=2,
        grid=(batch, N_HEADS, nt),
        in_specs=[
            pl.BlockSpec((t, HEAD_DIM), lambda b, h, i, *_: (b * nt + i, h)),
            pl.BlockSpec((seq, HEAD_DIM), lambda b, h, i, *_: (b, N_HEADS + h)),
            pl.BlockSpec((seq, HEAD_DIM), lambda b, h, i, *_: (b, 2 * N_HEADS + h)),
            pl.BlockSpec((1, 1, 1, t), lambda b, h, i, *_: (b, i, 0, 0)),
            pl.BlockSpec((1, seq, 1), lambda b, h, i, *_: (b, 0, 0)),
            pl.BlockSpec((N_HEADS, LANES), lambda b, h, i, *_: (0, 0)),
            pl.BlockSpec(memory_space=pltpu.SMEM),
            pl.BlockSpec(memory_space=pltpu.SMEM),
            pl.BlockSpec((HEAD_DIM, 1), lambda b, h, i, *_: (0, 0)),
        ],
        out_specs=pl.BlockSpec((t, HEAD_DIM), lambda b, h, i, *_: (b * nt + i, h)),
        scratch_shapes=[pltpu.VMEM((ntk, va, tk), BF16),
                        pltpu.VMEM((2, tk, t), F32),
                        pltpu.VMEM((2, tk, t), F32),
                        pltpu.VMEM((2, n_sub, 1, t), F32),
                        pltpu.VMEM((2, n_sub, 1, t), F32),
                        pltpu.VMEM((2, tk, t), BF16),
                        pltpu.VMEM((2, tk, t), BF16),
                        pltpu.VMEM((2, 1, t), F32),
                        pltpu.VMEM((2, 1, t), F32),
                        pltpu.VMEM((2, 1, t), F32),
                        pltpu.VMEM((2, va, t), F32)],
    )
    return pl.pallas_call(
        _attn_kernel,
        grid_spec=grid_spec,
        out_shape=jax.ShapeDtypeStruct((n, N_HEADS * HEAD_DIM), BF16),
        compiler_params=pltpu.CompilerParams(
            dimension_semantics=("arbitrary", "arbitrary", "arbitrary"),
            vmem_limit_bytes=VMEM_LIMIT),
        name="attn",
    )(smin, smax, p, p, p, posq, posk, rb_tab, rb_t, lam, g_sub)


def _hgrn_group(r0, forward, q_ref, v_ref, lf_ref, tri_ref, st):
    c = C_HGRN
    order = range(HGRN_GROUP) if forward else range(HGRN_GROUP - 1, -1, -1)
    rows = [pl.ds(r0 + k * c, c) for k in order]
    tri = tri_ref[...]
    row = lax.broadcasted_iota(I32, (c, c), 0)
    col = lax.broadcasted_iota(I32, (c, c), 1)
    keep = (col <= row) if forward else (col >= row)

    gs = [lf_ref[r, :] for r in rows]
    bsums = []
    for g in gs:
        g_hi = g.astype(BF16)
        g_lo = (g - g_hi.astype(F32)).astype(BF16)
        bsums.append(jnp.dot(tri, g_hi, preferred_element_type=F32)
                     + jnp.dot(tri, g_lo, preferred_element_type=F32))

    q_in, q_t, k_t, k_st, v_t, vs, decay = [], [], [], [], [], [], []
    for r, g, bsum in zip(rows, gs, bsums):
        if forward:
            ref = bsum[c // 2 - 1:c // 2, :]
            b_end = bsum[c - 1:c, :]
        else:
            ref = bsum[c // 2:c // 2 + 1, :]
            b_end = bsum[0:1, :]
        q = q_ref[r, :].astype(F32)
        v = v_ref[r, :]
        kf = 1.0 - jnp.exp(g)
        q_in.append((q * jnp.exp(bsum)).astype(BF16))
        q_t.append((q * jnp.exp(jnp.minimum(bsum - ref, EXP_CLAMP))).astype(BF16))
        k_t.append((kf * jnp.exp(jnp.minimum(ref - bsum, EXP_CLAMP))).astype(BF16))
        k_st.append((kf * jnp.exp(b_end - bsum)).astype(BF16))
        v_t.append(v.astype(F32).T.astype(BF16))
        vs.append(v)
        decay.append(jnp.exp(b_end))

    scores = [lax.dot_general(a, b, NT_DIMS, preferred_element_type=F32) for a, b in zip(q_t, k_t)]
    st_add = [jnp.dot(a, b, preferred_element_type=F32) for a, b in zip(v_t, k_st)]
    intra = [jnp.dot(jnp.where(keep, s, 0.0).astype(BF16), v, preferred_element_type=F32)
             for s, v in zip(scores, vs)]

    outs = []
    for k in range(HGRN_GROUP):
        o = lax.dot_general(q_in[k], st.astype(BF16), NT_DIMS, preferred_element_type=F32) + intra[k]
        st = st * decay[k] + st_add[k]
        outs.append((rows[k], o))
    return outs, st


def _hgrn_kernel(q_ref, v_ref, g_ref, lff_ref, lfb_ref, tril_ref, triu_ref, gh_ref, o_ref, of_scr):
    rows_per_group = C_HGRN * HGRN_GROUP
    n_groups = q_ref.shape[0] // rows_per_group
    st0 = jnp.zeros((HEAD_DIM, HEAD_DIM), F32)

    def fwd(gi, st):
        r0 = pl.multiple_of(gi * rows_per_group, rows_per_group)
        outs, st = _hgrn_group(r0, True, q_ref, v_ref, lff_ref, tril_ref, st)
        for r, o in outs:
            of_scr[r, :] = o
        return st

    lax.fori_loop(0, n_groups, fwd, st0)

    def bwd(gi, st):
        r0 = pl.multiple_of((n_groups - 1 - gi) * rows_per_group, rows_per_group)
        outs, st = _hgrn_group(r0, False, q_ref, v_ref, lfb_ref, triu_ref, st)
        for r, o in outs:
            o = of_scr[r, :] + o
            ms = jnp.mean(o * o, axis=-1, keepdims=True)
            y = o * lax.rsqrt(ms + EPS) * gh_ref[...] * g_ref[r, :].astype(F32)
            o_ref[r, :] = y.astype(BF16)
        return st

    lax.fori_loop(0, n_groups, bwd, st0)


def _hgrn(p, lf, tril, triu, g_hgrn, batch, seq):
    n = batch * seq
    c = C_HGRN
    blk = lambda off: pl.BlockSpec((seq, HEAD_DIM), lambda b, h: (b, off + h))
    return pl.pallas_call(
        _hgrn_kernel,
        grid=(batch, N_HEADS),
        in_specs=[blk(3 * N_HEADS), blk(4 * N_HEADS), blk(7 * N_HEADS), blk(0), blk(N_HEADS),
                  pl.BlockSpec((c, c), lambda b, h: (0, 0)),
                  pl.BlockSpec((c, c), lambda b, h: (0, 0)),
                  pl.BlockSpec((1, HEAD_DIM), lambda b, h: (0, 0))],
        out_specs=pl.BlockSpec((seq, HEAD_DIM), lambda b, h: (b, h)),
        out_shape=jax.ShapeDtypeStruct((n, N_HEADS * HEAD_DIM), BF16),
        scratch_shapes=[pltpu.VMEM((seq, HEAD_DIM), F32)],
        compiler_params=pltpu.CompilerParams(dimension_semantics=("arbitrary", "arbitrary"),
                                             vmem_limit_bytes=VMEM_LIMIT),
        name="hgrn",
    )(p, p, p, lf, lf, tril, triu, g_hgrn)


def _col_max(x):
    return jnp.max(x, axis=0, keepdims=True)


def _outproj_kernel(oa_ref, oh_ref, wa_ref, wb_ref, x_ref, gt_ref, gffn_ref, sc_ref, sh_ref,
                    wr_hi_ref, wr_lo_ref, rbias_ref, upper_ref, ones_ref,
                    x1_ref, hp_ref, eidx_ref, slot_ref, gate_ref, cnt_ref, cnt_scr):
    i = pl.program_id(0)
    tm = x_ref.shape[0]

    @pl.when(i == 0)
    def _():
        cnt_scr[...] = jnp.zeros(cnt_scr.shape, F32)

    acc = (jnp.dot(oa_ref[...], wa_ref[...], preferred_element_type=F32)
           + jnp.dot(oh_ref[...], wb_ref[...], preferred_element_type=F32))
    x1 = x_ref[...] + gt_ref[0] * acc
    x1_ref[...] = x1
    ms = jnp.mean(x1 * x1, axis=-1, keepdims=True)
    h2 = x1 * lax.rsqrt(ms + EPS) * gffn_ref[...] * (1.0 + sc_ref[0]) + sh_ref[0]
    half = h2.shape[1] // 2
    hp_ref[...] = _pack_pair(h2[:, :half], h2[:, half:])

    h_hi = h2.astype(BF16)
    h_lo = (h2 - h_hi.astype(F32)).astype(BF16)
    wr_hi = wr_hi_ref[...]
    logits = (lax.dot_general(wr_hi, h_hi, NT_DIMS, preferred_element_type=F32)
              + lax.dot_general(wr_hi, h_lo, NT_DIMS, preferred_element_type=F32)
              + lax.dot_general(wr_lo_ref[...], h_hi, NT_DIMS, preferred_element_type=F32))
    scores = jax.nn.sigmoid(logits)
    biased = scores + rbias_ref[...]

    gs = []
    for g in range(N_GROUPS):
        blk = biased[g * GROUP_SIZE:(g + 1) * GROUP_SIZE, :]
        top1 = _col_max(blk)
        eq = blk == top1
        n_eq = jnp.sum(eq.astype(F32), axis=0, keepdims=True)
        second = _col_max(jnp.where(eq, -jnp.inf, blk))
        gs.append(top1 + jnp.where(n_eq > 1.0, top1, second))
    gsm = jnp.concatenate(gs, axis=0)
    giota = lax.broadcasted_iota(I32, gsm.shape, 0)
    gsel = jnp.zeros(gsm.shape, F32)
    for _ in range(TOPK_GROUPS):
        top = _col_max(gsm)
        idx = jnp.min(jnp.where(gsm == top, giota, N_GROUPS), axis=0, keepdims=True)
        pick = giota == idx
        gsel = jnp.where(pick, 1.0, gsel)
        gsm = jnp.where(pick, -jnp.inf, gsm)
    emask = jnp.concatenate(
        [jnp.broadcast_to(gsel[g:g + 1, :], (GROUP_SIZE, tm)) for g in range(N_GROUPS)], axis=0)
    masked = jnp.where(emask > 0.5, biased, -jnp.inf)

    eiota = lax.broadcasted_iota(I32, masked.shape, 0)
    idxs, gates = [], []
    for _ in range(TOP_K):
        top = _col_max(masked)
        idx = jnp.min(jnp.where(masked == top, eiota, N_EXPERTS), axis=0, keepdims=True)
        pick = eiota == idx
        gates.append(jnp.sum(jnp.where(pick, scores, 0.0), axis=0, keepdims=True))
        idxs.append(idx)
        masked = jnp.where(pick, -jnp.inf, masked)
    gate = jnp.concatenate(gates, axis=0)
    gate = gate / jnp.sum(gate, axis=0, keepdims=True) * ROUTED_SCALE
    eidx = jnp.concatenate(idxs, axis=0)
    eidx_ref[...] = eidx
    gate_ref[...] = gate

    sel = jnp.zeros(masked.shape, F32)
    for k in range(TOP_K):
        sel = jnp.where(eiota == idxs[k], 1.0, sel)
    sel_bf = sel.astype(BF16)
    rank = jnp.dot(sel_bf, upper_ref[...], preferred_element_type=F32)
    base = cnt_scr[...]
    posn = base[:, :1] + rank
    slots = [jnp.sum(jnp.where(eiota == idxs[k], posn, 0.0), axis=0, keepdims=True)
             for k in range(TOP_K)]
    slot_ref[...] = jnp.concatenate(slots, axis=0).astype(I32)
    new_cnt = base + jnp.dot(sel_bf, ones_ref[...], preferred_element_type=F32)
    cnt_scr[...] = new_cnt
    cnt_ref[...] = new_cnt


def _outproj(oa, oh, wa, wb, x2d, gt1, g_ffn, sc2, sh2, wr_hi, wr_lo, rbias, upper, ones, seq):
    n, d = x2d.shape
    tm = TM_OUT
    tiles_per_batch = seq // tm
    half = d // 2
    row = lambda w: pl.BlockSpec((tm, w), lambda i: (i, 0))
    const = lambda shape: pl.BlockSpec(shape, lambda i: tuple(0 for _ in shape))
    per_batch = pl.BlockSpec((1, 1, d), lambda i: (i // tiles_per_batch, 0, 0))
    tok = pl.BlockSpec((TOP_K, tm), lambda i: (0, i))
    return pl.pallas_call(
        _outproj_kernel,
        grid=(n // tm,),
        in_specs=[row(half), row(half), const((half, d)), const((half, d)), row(d), per_batch,
                  const((1, d)), per_batch, per_batch,
                  const((N_EXPERTS, d)), const((N_EXPERTS, d)), const((N_EXPERTS, 1)),
                  const((tm, tm)), const((tm, LANES))],
        out_specs=[row(d), row(half), tok, tok, tok, const((N_EXPERTS, LANES))],
        out_shape=[jax.ShapeDtypeStruct((n, d), F32),
                   jax.ShapeDtypeStruct((n, half), U32),
                   jax.ShapeDtypeStruct((TOP_K, n), I32),
                   jax.ShapeDtypeStruct((TOP_K, n), I32),
                   jax.ShapeDtypeStruct((TOP_K, n), F32),
                   jax.ShapeDtypeStruct((N_EXPERTS, LANES), F32)],
        scratch_shapes=[pltpu.VMEM((N_EXPERTS, LANES), F32)],
        compiler_params=pltpu.CompilerParams(dimension_semantics=("arbitrary",),
                                             vmem_limit_bytes=VMEM_LIMIT),
        name="outproj",
    )(oa, oh, wa, wb, x2d, gt1, g_ffn, sc2, sh2, wr_hi, wr_lo, rbias, upper, ones)


def _scatter_kernel(pfill_ref, pend_ref, nv_ref, dest_ref, h_hbm, xs_ref, zero_scr, h_buf, sems, lsem,
                    zsem):
    i = pl.program_id(0)
    n_steps = pl.num_programs(0)
    ts = h_buf.shape[1]
    tb = zero_scr.shape[0]
    n_tail = xs_ref.shape[0] // tb - nv_ref[0]

    def pad_fill(e, wait):
        def go(src, dst):
            cp = pltpu.make_async_copy(src, dst, zsem)
            cp.wait() if wait else cp.start()

        start = pfill_ref[e]
        end = pend_ref[e]
        head = jnp.minimum((-start) & 7, end - start)
        for r in range(7):
            @pl.when(r < head)
            def _():
                go(zero_scr.at[pl.ds(0, 1)], xs_ref.at[pl.ds(start + r, 1)])
        off = start + head
        rem = end - off
        size = tb // 2
        while size >= 8:
            cond = (rem & size) != 0

            @pl.when(cond)
            def _():
                go(zero_scr.at[pl.ds(0, size)], xs_ref.at[pl.ds(pl.multiple_of(off, 8), size)])
            off = off + jnp.where(cond, size, 0)
            size //= 2

    @pl.when(i == 0)
    def _():
        zero_scr[...] = jnp.zeros(zero_scr.shape, U32)

        def fill(e, carry):
            pad_fill(e, False)
            return carry

        lax.fori_loop(0, N_EXPERTS, fill, 0)

        def fill_tail(j, carry):
            start = pl.multiple_of((nv_ref[0] + j) * tb, tb)
            pltpu.make_async_copy(zero_scr, xs_ref.at[pl.ds(start, tb)], zsem).start()
            return carry

        lax.fori_loop(0, n_tail, fill_tail, 0)

        def drain(e, carry):
            pad_fill(e, True)
            return carry

        lax.fori_loop(0, N_EXPERTS, drain, 0)

        def drain_tail(j, carry):
            pltpu.make_async_copy(zero_scr, xs_ref.at[pl.ds(0, tb)], zsem).wait()
            return carry

        lax.fori_loop(0, n_tail, drain_tail, 0)

    def load(j):
        return pltpu.make_async_copy(h_hbm.at[pl.ds(pl.multiple_of(j * ts, ts), ts)],
                                     h_buf.at[lax.rem(j, H_SLOTS)], lsem.at[lax.rem(j, H_SLOTS)])

    def drain_scatters(j):
        sl = lax.rem(j, H_SLOTS)
        for k in range(TOP_K):
            pltpu.make_async_copy(h_buf.at[sl], xs_ref.at[pl.ds(0, ts)], sems.at[sl]).wait()

    @pl.when(i == 0)
    def _():
        load(0).start()

    @pl.when(i + 1 < n_steps)
    def _():
        load(i + 1).start()

    load(i).wait()
    cur = lax.rem(i, H_SLOTS)

    def issue(t, carry):
        for k in range(TOP_K):
            pltpu.make_async_copy(h_buf.at[cur, pl.ds(t, 1)], xs_ref.at[pl.ds(dest_ref[k, t], 1)],
                                  sems.at[cur]).start(priority=k % 2)
        return carry

    lax.fori_loop(0, ts, issue, 0)

    @pl.when(i >= 1)
    def _():
        drain_scatters(i - 1)

    @pl.when(i == n_steps - 1)
    def _():
        drain_scatters(i)


def _scatter(pfill, pend, n_valid, dest, hp, n_rows):
    n, w = hp.shape
    ts = T_ROW
    grid_spec = pltpu.PrefetchScalarGridSpec(
        num_scalar_prefetch=3,
        grid=(n // ts,),
        in_specs=[pl.BlockSpec((TOP_K, ts), lambda i, *_: (0, i), memory_space=pltpu.SMEM),
                  pl.BlockSpec(memory_space=pl.ANY)],
        out_specs=pl.BlockSpec(memory_space=pl.ANY),
        scratch_shapes=[pltpu.VMEM((TB_EXP, w), U32), pltpu.VMEM((H_SLOTS, ts, w), U32),
                        pltpu.SemaphoreType.DMA((H_SLOTS,)), pltpu.SemaphoreType.DMA((H_SLOTS,)),
                        pltpu.SemaphoreType.DMA],
    )
    return pl.pallas_call(
        _scatter_kernel,
        grid_spec=grid_spec,
        out_shape=jax.ShapeDtypeStruct((n_rows, w), U32),
        compiler_params=pltpu.CompilerParams(dimension_semantics=("arbitrary",),
                                             vmem_limit_bytes=VMEM_LIMIT),
        name="scatter",
    )(pfill, pend, n_valid, dest, hp)


def _experts_kernel(be_ref, nv_ref, ge_ref, ng_ref, xs_ref, wg_hbm, wu_hbm, wd_hbm, y_ref,
                    wg_f, wu_f, wd_f, wg_bf, wu_bf, wd_bf, sems, gctr):
    i = pl.program_id(0)
    prev = jnp.maximum(i - 1, 0)
    valid = i < nv_ref[0]
    fresh = valid & ((i == 0) | (be_ref[i] != be_ref[prev]))

    def weight_copies(g, slot):
        e = ge_ref[g]
        return (pltpu.make_async_copy(wg_hbm.at[e], wg_f.at[slot], sems.at[slot, 0]),
                pltpu.make_async_copy(wu_hbm.at[e], wu_f.at[slot], sems.at[slot, 1]),
                pltpu.make_async_copy(wd_hbm.at[e], wd_f.at[slot], sems.at[slot, 2]))

    @pl.when(i == 0)
    def _():
        gctr[0] = 0
        for g in range(W_SLOTS):
            @pl.when(g < ng_ref[0])
            def _():
                for cp in weight_copies(g, g):
                    cp.start()

    @pl.when(fresh)
    def _():
        g = gctr[0]
        slot = lax.rem(g, W_SLOTS)
        for cp in weight_copies(g, slot):
            cp.wait()
        wg_bf[...] = wg_f[slot].astype(BF16)
        wu_bf[...] = wu_f[slot].astype(BF16)
        wd_bf[...] = wd_f[slot].astype(BF16)

        @pl.when(g + W_SLOTS < ng_ref[0])
        def _():
            for cp in weight_copies(g + W_SLOTS, slot):
                cp.start()

        gctr[0] = g + 1

    @pl.when(valid)
    def _():
        lo, hi = _unpack_pair(xs_ref[...])
        x = jnp.concatenate([lo.astype(BF16), hi.astype(BF16)], axis=1)
        hg = jnp.dot(x, wg_bf[...], preferred_element_type=F32)
        hu = jnp.dot(x, wu_bf[...], preferred_element_type=F32)
        a = (_silu(hg) * hu).astype(BF16)
        y = jnp.dot(a, wd_bf[...], preferred_element_type=F32)
        half = y.shape[1] // 2
        y_ref[...] = _pack_pair(y[:, :half], y[:, half:])

    @pl.when(jnp.logical_not(valid))
    def _():
        y_ref[...] = jnp.zeros(y_ref.shape, U32)


def _experts(block_e, n_valid, group_e, n_groups, xs, w_gate, w_up, w_down, n_blocks):
    tb = TB_EXP
    w = xs.shape[1]
    _, d, f = w_gate.shape
    grid_spec = pltpu.PrefetchScalarGridSpec(
        num_scalar_prefetch=4,
        grid=(n_blocks,),
        in_specs=[pl.BlockSpec((tb, w), lambda i, be, nv, ge, ng: (jnp.minimum(i, nv[0] - 1), 0)),
                  pl.BlockSpec(memory_space=pl.ANY),
                  pl.BlockSpec(memory_space=pl.ANY),
                  pl.BlockSpec(memory_space=pl.ANY)],
        out_specs=pl.BlockSpec((tb, w), lambda i, be, nv, ge, ng: (i, 0)),
        scratch_shapes=[pltpu.VMEM((W_SLOTS, d, f), F32), pltpu.VMEM((W_SLOTS, d, f), F32),
                        pltpu.VMEM((W_SLOTS, f, d), F32),
                        pltpu.VMEM((d, f), BF16), pltpu.VMEM((d, f), BF16), pltpu.VMEM((f, d), BF16),
                        pltpu.SemaphoreType.DMA((W_SLOTS, 3)), pltpu.SMEM((1,), I32)],
    )
    return pl.pallas_call(
        _experts_kernel,
        grid_spec=grid_spec,
        out_shape=jax.ShapeDtypeStruct((n_blocks * tb, w), U32),
        compiler_params=pltpu.CompilerParams(dimension_semantics=("arbitrary",),
                                             vmem_limit_bytes=VMEM_LIMIT),
        name="experts",
    )(block_e, n_valid, group_e, n_groups, xs, w_gate, w_up, w_down)


def _combine_kernel(dcur_ref, dnxt_ref, x1_ref, hp_ref, gate_ref, gt_ref,
                    wsg_ref, wsu_ref, wsd_ref, y_ref, o_ref, buf, sems):
    i = pl.program_id(0)
    tc = x1_ref.shape[0]
    slot = lax.rem(i, 2)

    def issue(d_ref, sl):
        def body(t, carry):
            for k in range(TOP_K):
                pltpu.make_async_copy(y_ref.at[pl.ds(d_ref[k, t], 1)], buf.at[sl, k, pl.ds(t, 1)],
                                      sems.at[sl]).start(priority=k % 2)
            return carry

        lax.fori_loop(0, tc, body, 0)

    @pl.when(i == 0)
    def _():
        issue(dcur_ref, 0)

    @pl.when(i + 1 < pl.num_programs(0))
    def _():
        issue(dnxt_ref, 1 - slot)

    lo, hi = _unpack_pair(hp_ref[...])
    x = jnp.concatenate([lo.astype(BF16), hi.astype(BF16)], axis=1)
    hg = jnp.dot(x, wsg_ref[...], preferred_element_type=F32)
    hu = jnp.dot(x, wsu_ref[...], preferred_element_type=F32)
    a = (_silu(hg) * hu).astype(BF16)
    shared = jnp.dot(a, wsd_ref[...], preferred_element_type=F32)

    for k in range(TOP_K):
        pltpu.make_async_copy(y_ref.at[pl.ds(0, tc)], buf.at[slot, k], sems.at[slot]).wait()

    half = shared.shape[1] // 2
    gate = gate_ref[...]
    r_lo = shared[:, :half]
    r_hi = shared[:, half:]
    for k in range(TOP_K):
        lo, hi = _unpack_pair(buf[slot, k])
        gk = gate[:, k:k + 1]
        r_lo = r_lo + gk * lo
        r_hi = r_hi + gk * hi
    gt = gt_ref[0]
    o_ref[:, :half] = x1_ref[:, :half] + gt[:, :half] * r_lo
    o_ref[:, half:] = x1_ref[:, half:] + gt[:, half:] * r_hi


def _combine(dest, x1, hp, gate_t, gt2, wsg, wsu, wsd, y, seq):
    n, d = x1.shape
    tc = T_ROW
    w = hp.shape[1]
    f = wsg.shape[1]
    tiles_per_batch = seq // tc
    last = n // tc - 1
    return pl.pallas_call(
        _combine_kernel,
        grid=(n // tc,),
        in_specs=[pl.BlockSpec((TOP_K, tc), lambda i: (0, i), memory_space=pltpu.SMEM),
                  pl.BlockSpec((TOP_K, tc), lambda i: (0, jnp.minimum(i + 1, last)),
                               memory_space=pltpu.SMEM),
                  pl.BlockSpec((tc, d), lambda i: (i, 0)),
                  pl.BlockSpec((tc, w), lambda i: (i, 0)),
                  pl.BlockSpec((tc, TOP_K), lambda i: (i, 0)),
                  pl.BlockSpec((1, 1, d), lambda i: (i // tiles_per_batch, 0, 0)),
                  pl.BlockSpec((d, f), lambda i: (0, 0)),
                  pl.BlockSpec((d, f), lambda i: (0, 0)),
                  pl.BlockSpec((f, d), lambda i: (0, 0)),
                  pl.BlockSpec(memory_space=pl.ANY)],
        out_specs=pl.BlockSpec((tc, d), lambda i: (i, 0)),
        scratch_shapes=[pltpu.VMEM((2, TOP_K, tc, w), U32), pltpu.SemaphoreType.DMA((2,))],
        out_shape=jax.ShapeDtypeStruct((n, d), F32),
        compiler_params=pltpu.CompilerParams(dimension_semantics=("arbitrary",),
                                             vmem_limit_bytes=VMEM_LIMIT),
        name="combine",
    )(dest, dest, x1, hp, gate_t, gt2, wsg, wsu, wsd, y)


def kernel(x, c, positions, rel_bias, hgrn_lb_logits, w_ada, b_ada, g_mix, w_in, g_q, g_k, lam_q1, lam_k1, lam_q2, lam_k2, g_sub, g_hgrn, w_out, g_ffn, w_router, router_bias, w_exp_gate, w_exp_up, w_exp_down, w_sh_gate, w_sh_up, w_sh_down):
    batch, seq, d = x.shape
    n = batch * seq
    layer = 0
    x2d = x.reshape(n, d)

    c_pad = jnp.zeros((8, d), F32).at[:batch].set(c.astype(F32))
    mod = _ada(c_pad, w_ada[layer], b_ada[layer][None, :])[:batch]
    sh1, sc1, gt1, sh2, sc2, gt2 = [m.reshape(batch, 1, d) for m in jnp.split(mod, 6, axis=-1)]

    lbs = jnp.cumsum(jax.nn.softmax(hgrn_lb_logits.astype(F32), axis=1), axis=1)[:, layer]
    lbs = lbs.reshape(2, 1, SEG)
    reps = SEG // QK_DIM
    qk_gain = jnp.stack([jnp.tile(g_q[layer].astype(F32), reps) * (QK_DIM ** -0.5 * LOG2E),
                         jnp.tile(g_k[layer].astype(F32), reps)]).reshape(2, 1, SEG)
    lane = jnp.arange(LANES)
    g64 = jnp.where((lane[:, None] // QK_DIM) == (lane[None, :] // QK_DIM), 1.0 / QK_DIM, 0.0).astype(BF16)
    lam = (jnp.exp(jnp.sum(lam_q1[layer].astype(F32) * lam_k1[layer].astype(F32)))
           - jnp.exp(jnp.sum(lam_q2[layer].astype(F32) * lam_k2[layer].astype(F32)))
           + LAM_INIT).reshape(1)

    p, lf = _inproj(x2d, sc1, sh1, g_mix[layer][None, :], w_in[layer].astype(BF16), qk_gain, lbs, g64, seq)

    nt = seq // T_ATT
    pos_sub = positions.astype(I32).reshape(batch * seq // T_SUB, T_SUB)
    smin = jnp.min(pos_sub, axis=1)
    smax = jnp.max(pos_sub, axis=1)
    posq = positions.astype(I32).reshape(batch, nt, 1, T_ATT)
    posk = positions.astype(I32).reshape(batch, seq, 1)
    rb_t = rel_bias.astype(F32).T * LOG2E
    rb_tab = jnp.zeros((N_HEADS, LANES), F32).at[:, :REL_BUCKETS].set(rb_t)
    oa = _attention(p, smin, smax, posq, posk, rb_tab, rb_t, lam,
                    g_sub[layer][:, None].astype(F32), batch, seq)

    ci = jnp.arange(C_HGRN)
    tril = (ci[None, :] <= ci[:, None]).astype(BF16)
    triu = (ci[None, :] >= ci[:, None]).astype(BF16)
    oh = _hgrn(p, lf, tril, triu, g_hgrn[layer][None, :].astype(F32), batch, seq)

    half = d // 2
    w_out_bf = w_out[layer].astype(BF16)
    wr_t = w_router[layer].astype(F32).T
    wr_hi = wr_t.astype(BF16)
    wr_lo = (wr_t - wr_hi.astype(F32)).astype(BF16)
    ti = jnp.arange(TM_OUT)
    upper = (ti[:, None] < ti[None, :]).astype(BF16)
    ones = jnp.ones((TM_OUT, LANES), BF16)
    x1, hp, eidx, slot, gate, cnt = _outproj(
        oa, oh, w_out_bf[:half], w_out_bf[half:], x2d, gt1, g_ffn[layer][None, :], sc2, sh2,
        wr_hi, wr_lo, router_bias[layer].astype(F32)[:, None], upper, ones, seq)

    tb = TB_EXP
    counts = cnt[:, 0].astype(I32)
    padded = (counts + tb - 1) // tb * tb
    pends = jnp.cumsum(padded)
    pstart = (pends - padded).astype(I32)
    n_blocks = (n * TOP_K) // tb + N_EXPERTS
    n_valid = (pends[-1] // tb).astype(I32).reshape(1)
    block_e = jnp.minimum(jnp.searchsorted(pends, jnp.arange(n_blocks, dtype=I32) * tb, side='right'),
                          N_EXPERTS - 1).astype(I32)
    pfill = (pstart + counts).astype(I32)
    pend = pends.astype(I32)
    dest = (pstart[eidx] + slot).astype(I32)
    has_rows = counts > 0
    group_e = jnp.nonzero(has_rows, size=N_EXPERTS, fill_value=0)[0].astype(I32)
    n_groups = jnp.sum(has_rows).astype(I32).reshape(1)

    xs = _scatter(pfill, pend, n_valid, dest, hp, n_blocks * tb)
    y = _experts(block_e, n_valid, group_e, n_groups, xs,
                 w_exp_gate[layer], w_exp_up[layer], w_exp_down[layer], n_blocks)
    out = _combine(dest, x1, hp, gate.T, gt2,
                   w_sh_gate[layer].astype(BF16), w_sh_up[layer].astype(BF16),
                   w_sh_down[layer].astype(BF16), y, seq)
    return out.reshape(batch, seq, d)
```

```python
import functools
import math

import jax
import jax.numpy as jnp
from jax import lax
from jax.experimental import pallas as pl
from jax.experimental.pallas import tpu as pltpu

F32 = jnp.float32
BF16 = jnp.bfloat16
I32 = jnp.int32
U32 = jnp.uint32

D_MODEL = 2048
N_HEADS = 8
QK_DIM = 64
HEAD_DIM = 128
SEG = 1024
N_SEG = 8
REL_BUCKETS = 32
REL_MAX_DIST = 128
N_EXPERTS = 256
TOP_K = 8
N_GROUPS = 8
TOPK_GROUPS = 4
GROUP_SIZE = N_EXPERTS // N_GROUPS
EXPERT_DIM = 512
ROUTED_SCALE = 2.5
EPS = 1e-6
LAM_INIT = 0.8 - 0.6 * math.exp(-0.3 * 0)
LOG2E = math.log2(math.e)

LANES = 128
VMEM_LIMIT = 56 * 1024 * 1024

TM_IN = 512
IN_CHUNK = 256
T_ATT = 512
T_KEY = 1024
T_SUB = 128
V_PAD = 16
C_HGRN = 64
HGRN_GROUP = 8
TM_OUT = 256
T_ROW = 256
TB_EXP = 256
W_SLOTS = 2
H_SLOTS = 3
NEG_BIG = -1e30
EXP_CLAMP = 80.0

NT_DIMS = (((1,), (1,)), ((), ()))


def _silu(x):
    return x * jax.nn.sigmoid(x)


def _pack_pair(lo_f32, hi_f32):
    lo = lax.bitcast_convert_type(lo_f32.astype(BF16).astype(F32), U32)
    hi = lax.bitcast_convert_type(hi_f32.astype(BF16).astype(F32), U32)
    return (hi & jnp.uint32(0xFFFF0000)) | (lo >> 16)


def _unpack_pair(word):
    lo = lax.bitcast_convert_type(word << 16, F32)
    hi = lax.bitcast_convert_type(word & jnp.uint32(0xFFFF0000), F32)
    return lo, hi


def _ada_kernel(c_ref, w_ref, b_ref, o_ref):
    a = _silu(c_ref[...]).astype(BF16)
    o_ref[...] = jnp.dot(a, w_ref[...].astype(BF16), preferred_element_type=F32) + b_ref[...]


def _ada(c_pad, w, b):
    d, n = w.shape
    tn = 1024
    return pl.pallas_call(
        _ada_kernel,
        grid=(n // tn,),
        in_specs=[pl.BlockSpec((8, d), lambda j: (0, 0)),
                  pl.BlockSpec((d, tn), lambda j: (0, j)),
                  pl.BlockSpec((1, tn), lambda j: (0, j))],
        out_specs=pl.BlockSpec((8, tn), lambda j: (0, j)),
        out_shape=jax.ShapeDtypeStruct((8, n), F32),
        compiler_params=pltpu.CompilerParams(dimension_semantics=("arbitrary",),
                                             vmem_limit_bytes=VMEM_LIMIT),
        name="ada",
    )(c_pad, w, b)


def _inproj_kernel(x_ref, sc_ref, sh_ref, gmix_ref, w_ref, qkg_ref, lb_ref, g64_ref,
                   p_ref, lf_ref, h_scr):
    j = pl.program_id(1)

    @pl.when(j == 0)
    def _():
        x = x_ref[...]
        ms = jnp.mean(x * x, axis=-1, keepdims=True)
        y = x * lax.rsqrt(ms + EPS) * gmix_ref[...]
        h_scr[...] = (y * (1.0 + sc_ref[0]) + sh_ref[0]).astype(BF16)

    def chunks(epilogue):
        for c in range(SEG // IN_CHUNK):
            sl = slice(c * IN_CHUNK, (c + 1) * IN_CHUNK)
            epilogue(sl, jnp.dot(h_scr[...], w_ref[:, sl], preferred_element_type=F32))

    @pl.when(j < 2)
    def _():
        acc = jnp.dot(h_scr[...], w_ref[...], preferred_element_type=F32)
        gain = qkg_ref[0]
        for c in range(SEG // LANES):
            sl = slice(c * LANES, (c + 1) * LANES)
            xs = acc[:, sl]
            ms = jnp.dot((xs * xs).astype(BF16), g64_ref[...], preferred_element_type=F32)
            p_ref[:, sl] = (xs * lax.rsqrt(ms + EPS) * gain[:, sl]).astype(BF16)

    @pl.when((j == 2) | (j == 4))
    def _():
        def plain(sl, acc):
            p_ref[:, sl] = acc.astype(BF16)

        chunks(plain)

    @pl.when((j == 3) | (j == 7))
    def _():
        def silu(sl, acc):
            p_ref[:, sl] = _silu(acc).astype(BF16)

        chunks(silu)

    @pl.when((j == 5) | (j == 6))
    def _():
        def log_gate(sl, z):
            lb = lb_ref[0, :, sl]
            f = lb + (1.0 - lb) * jax.nn.sigmoid(z)
            lf_ref[:, sl] = jnp.log(f)
            p_ref[:, sl] = z.astype(BF16)

        chunks(log_gate)


def _inproj(x2d, sc1, sh1, g_mix, w_in_bf, qk_gain, lbs, g64, seq):
    n, d = x2d.shape
    tm = TM_IN
    tiles_per_batch = seq // tm
    return pl.pallas_call(
        _inproj_kernel,
        grid=(n // tm, N_SEG),
        in_specs=[
            pl.BlockSpec((tm, d), lambda i, j: (i, 0)),
            pl.BlockSpec((1, 1, d), lambda i, j: (i // tiles_per_batch, 0, 0)),
            pl.BlockSpec((1, 1, d), lambda i, j: (i // tiles_per_batch, 0, 0)),
            pl.BlockSpec((1, d), lambda i, j: (0, 0)),
            pl.BlockSpec((d, SEG), lambda i, j: (0, j)),
            pl.BlockSpec((1, 1, SEG), lambda i, j: (jnp.minimum(j, 1), 0, 0)),
            pl.BlockSpec((1, 1, SEG), lambda i, j: (jnp.clip(j - 5, 0, 1), 0, 0)),
            pl.BlockSpec((LANES, LANES), lambda i, j: (0, 0)),
        ],
        out_specs=[
            pl.BlockSpec((tm, SEG), lambda i, j: (i, j)),
            pl.BlockSpec((tm, SEG), lambda i, j: (i, jnp.clip(j - 5, 0, 1))),
        ],
        out_shape=[jax.ShapeDtypeStruct((n, N_SEG * SEG), BF16),
                   jax.ShapeDtypeStruct((n, 2 * SEG), F32)],
        scratch_shapes=[pltpu.VMEM((tm, d), BF16)],
        compiler_params=pltpu.CompilerParams(dimension_semantics=("arbitrary", "arbitrary"),
                                             vmem_limit_bytes=VMEM_LIMIT),
        name="inproj",
    )(x2d, sc1, sh1, g_mix, w_in_bf, qk_gain, lbs, g64)


def _t5_bias_tile(pos_q, pos_k, table):
    half = REL_BUCKETS // 2
    max_exact = half // 2
    rel = pos_k - pos_q
    n = jnp.abs(rel)
    nf = jnp.maximum(n, 1).astype(F32)
    large = max_exact + (jnp.log(nf / max_exact) / math.log(REL_MAX_DIST / max_exact)
                         * (half - max_exact)).astype(I32)
    large = jnp.minimum(large, half - 1)
    bucket = jnp.where(rel > 0, half, 0) + jnp.where(n < max_exact, n, large)
    rows = bucket.shape[0]
    tbl = jnp.broadcast_to(table, (rows, LANES))
    cols = [jnp.take_along_axis(tbl, bucket[:, c * LANES:(c + 1) * LANES], axis=1)
            for c in range(bucket.shape[1] // LANES)]
    return jnp.concatenate(cols, axis=1)


def _attn_kernel(smin_ref, smax_ref, q_ref, k_ref, v_ref, posq_ref, posk_ref, rbt_ref, rb_ref, lam_ref,
                 gsub_ref, o_ref, vt_scr, s_a, s_b, cm_a, cm_b, p_a, p_b, al_a, al_b, m_scr, a_scr):
    b = pl.program_id(0)
    h = pl.program_id(1)
    i = pl.program_id(2)
    tq = T_ATT
    tk = T_KEY
    n_sub = tk // T_SUB
    nq_sub = tq // T_SUB
    ntk = k_ref.shape[0] // tk
    n_pairs = ntk // 2
    subs_per_batch = k_ref.shape[0] // T_SUB

    @pl.when(i == 0)
    def _():
        ones_row = jnp.where(lax.broadcasted_iota(I32, (V_PAD, tk), 0) == 0, 1.0, 0.0).astype(BF16)

        def tr(c, carry):
            r0 = pl.multiple_of(c * tk, tk)
            vt_scr[c, :HEAD_DIM, :] = v_ref[pl.ds(r0, tk), :].astype(F32).T.astype(BF16)
            vt_scr[c, HEAD_DIM:, :] = ones_row
            return carry

        lax.fori_loop(0, ntk, tr, 0)

    q = q_ref[...]
    qs = (q[:, :QK_DIM], q[:, QK_DIM:])
    m_scr[...] = jnp.full(m_scr.shape, NEG_BIG, F32)
    a_scr[...] = jnp.zeros(a_scr.shape, F32)

    sub0 = b * subs_per_batch + i * nq_sub
    q_lo = smin_ref[sub0]
    q_hi = smax_ref[sub0]
    for u in range(1, nq_sub):
        q_lo = jnp.minimum(q_lo, smin_ref[sub0 + u])
        q_hi = jnp.maximum(q_hi, smax_ref[sub0 + u])
    c_pos = rb_ref[h, REL_BUCKETS - 1]
    c_neg = rb_ref[h, REL_BUCKETS // 2 - 1]
    pos_q = posq_ref[0, 0]

    def classify(j, u):
        ksub = b * subs_per_batch + j * n_sub + u
        lo = smin_ref[ksub] - q_hi
        hi = smax_ref[ksub] - q_lo
        far = (lo >= REL_MAX_DIST) | (hi <= -REL_MAX_DIST)
        shift = jnp.where(lo >= REL_MAX_DIST, c_pos, jnp.where(hi <= -REL_MAX_DIST, c_neg, 0.0))
        return far, shift

    def scores(j, s_ref, cm_ref):
        kk = k_ref[pl.ds(pl.multiple_of(j * tk, tk), tk), :]
        ks = (kk[:, :QK_DIM], kk[:, QK_DIM:])
        for mp in range(2):
            s = lax.dot_general(ks[mp], qs[mp], NT_DIMS, preferred_element_type=F32)
            s_ref[mp] = s
            for u in range(n_sub):
                _, shift = classify(j, u)
                cm_ref[mp, u] = jnp.max(s[u * T_SUB:(u + 1) * T_SUB], axis=0, keepdims=True) + shift

    def fixup(j, s_ref, cm_ref):
        for u in range(n_sub):
            far, _ = classify(j, u)

            @pl.when(jnp.logical_not(far))
            def _():
                rows = pl.ds(u * T_SUB, T_SUB)
                pos_k = posk_ref[0, pl.ds(pl.multiple_of(j * tk + u * T_SUB, T_SUB), T_SUB), :]
                bias = _t5_bias_tile(pos_q, pos_k, rbt_ref[pl.ds(h, 1), :])
                for mp in range(2):
                    sb = s_ref[mp, rows, :] + bias
                    s_ref[mp, rows, :] = sb
                    cm_ref[mp, u] = jnp.max(sb, axis=0, keepdims=True)

    def soft(j, s_ref, cm_ref, p_ref, al_ref):
        for mp in range(2):
            m_old = m_scr[mp]
            m_new = m_old
            for u in range(n_sub):
                m_new = jnp.maximum(m_new, cm_ref[mp, u])
            for u in range(n_sub):
                rows = pl.ds(u * T_SUB, T_SUB)
                _, shift = classify(j, u)
                p_ref[mp, rows, :] = jnp.exp2(s_ref[mp, rows, :] - (m_new - shift)).astype(BF16)
            m_scr[mp] = m_new
            al_ref[mp] = jnp.exp2(m_old - m_new)

    def pv(j, p_ref, al_ref):
        vt = vt_scr[j]
        for mp in range(2):
            a_scr[mp] = al_ref[mp] * a_scr[mp] + jnp.dot(vt, p_ref[mp], preferred_element_type=F32)

    sbuf = ((s_a, cm_a), (s_b, cm_b))
    pbuf = ((p_a, al_a), (p_b, al_b))

    def step(j, par):
        scores(j + 2, *sbuf[par])
        soft(j + 1, *sbuf[1 - par], *pbuf[1 - par])
        pv(j, *pbuf[par])
        fixup(j + 2, *sbuf[par])

    scores(0, *sbuf[0])
    fixup(0, *sbuf[0])
    scores(1, *sbuf[1])
    soft(0, *sbuf[0], *pbuf[0])
    fixup(1, *sbuf[1])

    def body(jj, carry):
        step(2 * jj, 0)
        step(2 * jj + 1, 1)
        return carry

    lax.fori_loop(0, n_pairs - 1, body, 0)
    last = 2 * (n_pairs - 1)
    soft(last + 1, *sbuf[1], *pbuf[1])
    pv(last, *pbuf[0])
    pv(last + 1, *pbuf[1])

    num = [a_scr[mp, :HEAD_DIM, :] / a_scr[mp, HEAD_DIM:HEAD_DIM + 1, :] for mp in range(2)]
    o = num[0] - lam_ref[0] * num[1]
    ms = jnp.mean(o * o, axis=0, keepdims=True)
    o = o * lax.rsqrt(ms + EPS) * (gsub_ref[...] * (1.0 - LAM_INIT))
    o_ref[...] = o.T.astype(BF16)


def _attention(p, smin, smax, posq, posk, rb_tab, rb_t, lam, g_sub, batch, seq):
    t = T_ATT
    tk = T_KEY
    nt = seq // t
    ntk = seq // tk
    assert seq % (2 * tk) == 0
    n = batch * seq
    n_sub = tk // T_SUB
    va = HEAD_DIM + V_PAD
    grid_spec = pltpu.PrefetchScalarGridSpec(
        num_scalar_prefetch=2,
        grid=(batch, N_HEADS, nt),
        in_specs=[
            pl.BlockSpec((t, HEAD_DIM), lambda b, h, i, *_: (b * nt + i, h)),
            pl.BlockSpec((seq, HEAD_DIM), lambda b, h, i, *_: (b, N_HEADS + h)),
            pl.BlockSpec((seq, HEAD_DIM), lambda b, h, i, *_: (b, 2 * N_HEADS + h)),
            pl.BlockSpec((1, 1, 1, t), lambda b, h, i, *_: (b, i, 0, 0)),
            pl.BlockSpec((1, seq, 1), lambda b, h, i, *_: (b, 0, 0)),
            pl.BlockSpec((N_HEADS, LANES), lambda b, h, i, *_: (0, 0)),
            pl.BlockSpec(memory_space=pltpu.SMEM),
            pl.BlockSpec(memory_space=pltpu.SMEM),
            pl.BlockSpec((HEAD_DIM, 1), lambda b, h, i, *_: (0, 0)),
        ],
        out_specs=pl.BlockSpec((t, HEAD_DIM), lambda b, h, i, *_: (b * nt + i, h)),
        scratch_shapes=[pltpu.VMEM((ntk, va, tk), BF16),
                        pltpu.VMEM((2, tk, t), F32),
                        pltpu.VMEM((2, tk, t), F32),
                        pltpu.VMEM((2, n_sub, 1, t), F32),
                        pltpu.VMEM((2, n_sub, 1, t), F32),
                        pltpu.VMEM((2, tk, t), BF16),
                        pltpu.VMEM((2, tk, t), BF16),
                        pltpu.VMEM((2, 1, t), F32),
                        pltpu.VMEM((2, 1, t), F32),
                        pltpu.VMEM((2, 1, t), F32),
                        pltpu.VMEM((2, va, t), F32)],
    )
    return pl.pallas_call(
        _attn_kernel,
        grid_spec=grid_spec,
        out_shape=jax.ShapeDtypeStruct((n, N_HEADS * HEAD_DIM), BF16),
        compiler_params=pltpu.CompilerParams(
            dimension_semantics=("arbitrary", "arbitrary", "arbitrary"),
            vmem_limit_bytes=VMEM_LIMIT),
        name="attn",
    )(smin, smax, p, p, p, posq, posk, rb_tab, rb_t, lam, g_sub)


def _hgrn_group(r0, forward, q_ref, v_ref, lf_ref, tri_ref, st):
    c = C_HGRN
    order = range(HGRN_GROUP) if forward else range(HGRN_GROUP - 1, -1, -1)
    rows = [pl.ds(r0 + k * c, c) for k in order]
    tri = tri_ref[...]
    row = lax.broadcasted_iota(I32, (c, c), 0)
    col = lax.broadcasted_iota(I32, (c, c), 1)
    keep = (col <= row) if forward else (col >= row)

    gs = [lf_ref[r, :] for r in rows]
    bsums = []
    for g in gs:
        g_hi = g.astype(BF16)
        g_lo = (g - g_hi.astype(F32)).astype(BF16)
        bsums.append(jnp.dot(tri, g_hi, preferred_element_type=F32)
                     + jnp.dot(tri, g_lo, preferred_element_type=F32))

    q_in, q_t, k_t, k_st, v_t, vs, decay = [], [], [], [], [], [], []
    for r, g, bsum in zip(rows, gs, bsums):
        if forward:
            ref = bsum[c // 2 - 1:c // 2, :]
            b_end = bsum[c - 1:c, :]
        else:
            ref = bsum[c // 2:c // 2 + 1, :]
            b_end = bsum[0:1, :]
        q = q_ref[r, :].astype(F32)
        v = v_ref[r, :]
        kf = 1.0 - jnp.exp(g)
        q_in.append((q * jnp.exp(bsum)).astype(BF16))
        q_t.append((q * jnp.exp(jnp.minimum(bsum - ref, EXP_CLAMP))).astype(BF16))
        k_t.append((kf * jnp.exp(jnp.minimum(ref - bsum, EXP_CLAMP))).astype(BF16))
        k_st.append((kf * jnp.exp(b_end - bsum)).astype(BF16))
        v_t.append(v.astype(F32).T.astype(BF16))
        vs.append(v)
        decay.append(jnp.exp(b_end))

    scores = [lax.dot_general(a, b, NT_DIMS, preferred_element_type=F32) for a, b in zip(q_t, k_t)]
    st_add = [jnp.dot(a, b, preferred_element_type=F32) for a, b in zip(v_t, k_st)]
    intra = [jnp.dot(jnp.where(keep, s, 0.0).astype(BF16), v, preferred_element_type=F32)
             for s, v in zip(scores, vs)]

    outs = []
    for k in range(HGRN_GROUP):
        o = lax.dot_general(q_in[k], st.astype(BF16), NT_DIMS, preferred_element_type=F32) + intra[k]
        st = st * decay[k] + st_add[k]
        outs.append((rows[k], o))
    return outs, st


def _hgrn_kernel(q_ref, v_ref, g_ref, lff_ref, lfb_ref, tril_ref, triu_ref, gh_ref, o_ref, of_scr):
    rows_per_group = C_HGRN * HGRN_GROUP
    n_groups = q_ref.shape[0] // rows_per_group
    st0 = jnp.zeros((HEAD_DIM, HEAD_DIM), F32)

    def fwd(gi, st):
        r0 = pl.multiple_of(gi * rows_per_group, rows_per_group)
        outs, st = _hgrn_group(r0, True, q_ref, v_ref, lff_ref, tril_ref, st)
        for r, o in outs:
            of_scr[r, :] = o
        return st

    lax.fori_loop(0, n_groups, fwd, st0)

    def bwd(gi, st):
        r0 = pl.multiple_of((n_groups - 1 - gi) * rows_per_group, rows_per_group)
        outs, st = _hgrn_group(r0, False, q_ref, v_ref, lfb_ref, triu_ref, st)
        for r, o in outs:
            o = of_scr[r, :] + o
            ms = jnp.mean(o * o, axis=-1, keepdims=True)
            y = o * lax.rsqrt(ms + EPS) * gh_ref[...] * g_ref[r, :].astype(F32)
            o_ref[r, :] = y.astype(BF16)
        return st

    lax.fori_loop(0, n_groups, bwd, st0)


def _hgrn(p, lf, tril, triu, g_hgrn, batch, seq):
    n = batch * seq
    c = C_HGRN
    blk = lambda off: pl.BlockSpec((seq, HEAD_DIM), lambda b, h: (b, off + h))
    return pl.pallas_call(
        _hgrn_kernel,
        grid=(batch, N_HEADS),
        in_specs=[blk(3 * N_HEADS), blk(4 * N_HEADS), blk(7 * N_HEADS), blk(0), blk(N_HEADS),
                  pl.BlockSpec((c, c), lambda b, h: (0, 0)),
                  pl.BlockSpec((c, c), lambda b, h: (0, 0)),
                  pl.BlockSpec((1, HEAD_DIM), lambda b, h: (0, 0))],
        out_specs=pl.BlockSpec((seq, HEAD_DIM), lambda b, h: (b, h)),
        out_shape=jax.ShapeDtypeStruct((n, N_HEADS * HEAD_DIM), BF16),
        scratch_shapes=[pltpu.VMEM((seq, HEAD_DIM), F32)],
        compiler_params=pltpu.CompilerParams(dimension_semantics=("arbitrary", "arbitrary"),
                                             vmem_limit_bytes=VMEM_LIMIT),
        name="hgrn",
    )(p, p, p, lf, lf, tril, triu, g_hgrn)


def _col_max(x):
    return jnp.max(x, axis=0, keepdims=True)


def _outproj_kernel(oa_ref, oh_ref, wa_ref, wb_ref, x_ref, gt_ref, gffn_ref, sc_ref, sh_ref,
                    wr_hi_ref, wr_lo_ref, rbias_ref, upper_ref, ones_ref,
                    x1_ref, hp_ref, eidx_ref, slot_ref, gate_ref, cnt_ref, cnt_scr):
    i = pl.program_id(0)
    tm = x_ref.shape[0]

    @pl.when(i == 0)
    def _():
        cnt_scr[...] = jnp.zeros(cnt_scr.shape, F32)

    acc = (jnp.dot(oa_ref[...], wa_ref[...], preferred_element_type=F32)
           + jnp.dot(oh_ref[...], wb_ref[...], preferred_element_type=F32))
    x1 = x_ref[...] + gt_ref[0] * acc
    x1_ref[...] = x1
    ms = jnp.mean(x1 * x1, axis=-1, keepdims=True)
    h2 = x1 * lax.rsqrt(ms + EPS) * gffn_ref[...] * (1.0 + sc_ref[0]) + sh_ref[0]
    half = h2.shape[1] // 2
    hp_ref[...] = _pack_pair(h2[:, :half], h2[:, half:])

    h_hi = h2.astype(BF16)
    h_lo = (h2 - h_hi.astype(F32)).astype(BF16)
    wr_hi = wr_hi_ref[...]
    logits = (lax.dot_general(wr_hi, h_hi, NT_DIMS, preferred_element_type=F32)
              + lax.dot_general(wr_hi, h_lo, NT_DIMS, preferred_element_type=F32)
              + lax.dot_general(wr_lo_ref[...], h_hi, NT_DIMS, preferred_element_type=F32))
    scores = jax.nn.sigmoid(logits)
    biased = scores + rbias_ref[...]

    gs = []
    for g in range(N_GROUPS):
        blk = biased[g * GROUP_SIZE:(g + 1) * GROUP_SIZE, :]
        top1 = _col_max(blk)
        eq = blk == top1
        n_eq = jnp.sum(eq.astype(F32), axis=0, keepdims=True)
        second = _col_max(jnp.where(eq, -jnp.inf, blk))
        gs.append(top1 + jnp.where(n_eq > 1.0, top1, second))
    gsm = jnp.concatenate(gs, axis=0)
    giota = lax.broadcasted_iota(I32, gsm.shape, 0)
    gsel = jnp.zeros(gsm.shape, F32)
    for _ in range(TOPK_GROUPS):
        top = _col_max(gsm)
        idx = jnp.min(jnp.where(gsm == top, giota, N_GROUPS), axis=0, keepdims=True)
        pick = giota == idx
        gsel = jnp.where(pick, 1.0, gsel)
        gsm = jnp.where(pick, -jnp.inf, gsm)
    emask = jnp.concatenate(
        [jnp.broadcast_to(gsel[g:g + 1, :], (GROUP_SIZE, tm)) for g in range(N_GROUPS)], axis=0)
    masked = jnp.where(emask > 0.5, biased, -jnp.inf)

    eiota = lax.broadcasted_iota(I32, masked.shape, 0)
    idxs, gates = [], []
    for _ in range(TOP_K):
        top = _col_max(masked)
        idx = jnp.min(jnp.where(masked == top, eiota, N_EXPERTS), axis=0, keepdims=True)
        pick = eiota == idx
        gates.append(jnp.sum(jnp.where(pick, scores, 0.0), axis=0, keepdims=True))
        idxs.append(idx)
        masked = jnp.where(pick, -jnp.inf, masked)
    gate = jnp.concatenate(gates, axis=0)
    gate = gate / jnp.sum(gate, axis=0, keepdims=True) * ROUTED_SCALE
    eidx = jnp.concatenate(idxs, axis=0)
    eidx_ref[...] = eidx
    gate_ref[...] = gate

    sel = jnp.zeros(masked.shape, F32)
    for k in range(TOP_K):
        sel = jnp.where(eiota == idxs[k], 1.0, sel)
    sel_bf = sel.astype(BF16)
    rank = jnp.dot(sel_bf, upper_ref[...], preferred_element_type=F32)
    base = cnt_scr[...]
    posn = base[:, :1] + rank
    slots = [jnp.sum(jnp.where(eiota == idxs[k], posn, 0.0), axis=0, keepdims=True)
             for k in range(TOP_K)]
    slot_ref[...] = jnp.concatenate(slots, axis=0).astype(I32)
    new_cnt = base + jnp.dot(sel_bf, ones_ref[...], preferred_element_type=F32)
    cnt_scr[...] = new_cnt
    cnt_ref[...] = new_cnt


def _outproj(oa, oh, wa, wb, x2d, gt1, g_ffn, sc2, sh2, wr_hi, wr_lo, rbias, upper, ones, seq):
    n, d = x2d.shape
    tm = TM_OUT
    tiles_per_batch = seq // tm
    half = d // 2
    row = lambda w: pl.BlockSpec((tm, w), lambda i: (i, 0))
    const = lambda shape: pl.BlockSpec(shape, lambda i: tuple(0 for _ in shape))
    per_batch = pl.BlockSpec((1, 1, d), lambda i: (i // tiles_per_batch, 0, 0))
    tok = pl.BlockSpec((TOP_K, tm), lambda i: (0, i))
    return pl.pallas_call(
        _outproj_kernel,
        grid=(n // tm,),
        in_specs=[row(half), row(half), const((half, d)), const((half, d)), row(d), per_batch,
                  const((1, d)), per_batch, per_batch,
                  const((N_EXPERTS, d)), const((N_EXPERTS, d)), const((N_EXPERTS, 1)),
                  const((tm, tm)), const((tm, LANES))],
        out_specs=[row(d), row(half), tok, tok, tok, const((N_EXPERTS, LANES))],
        out_shape=[jax.ShapeDtypeStruct((n, d), F32),
                   jax.ShapeDtypeStruct((n, half), U32),
                   jax.ShapeDtypeStruct((TOP_K, n), I32),
                   jax.ShapeDtypeStruct((TOP_K, n), I32),
                   jax.ShapeDtypeStruct((TOP_K, n), F32),
                   jax.ShapeDtypeStruct((N_EXPERTS, LANES), F32)],
        scratch_shapes=[pltpu.VMEM((N_EXPERTS, LANES), F32)],
        compiler_params=pltpu.CompilerParams(dimension_semantics=("arbitrary",),
                                             vmem_limit_bytes=VMEM_LIMIT),
        name="outproj",
    )(oa, oh, wa, wb, x2d, gt1, g_ffn, sc2, sh2, wr_hi, wr_lo, rbias, upper, ones)


def _dest_kernel(pstart_ref, e_ref, slot_ref, o_ref):
    e = e_ref[...]

    def body(x, acc):
        return acc + jnp.where(e == x, pstart_ref[x], 0)

    o_ref[...] = lax.fori_loop(0, N_EXPERTS, body, slot_ref[...], unroll=8)


def _dest(pstart, eidx, slot):
    k, n = eidx.shape
    tn = min(n, 2048)
    grid_spec = pltpu.PrefetchScalarGridSpec(
        num_scalar_prefetch=1,
        grid=(n // tn,),
        in_specs=[pl.BlockSpec((k, tn), lambda i, *_: (0, i)),
                  pl.BlockSpec((k, tn), lambda i, *_: (0, i))],
        out_specs=pl.BlockSpec((k, tn), lambda i, *_: (0, i)),
    )
    return pl.pallas_call(
        _dest_kernel,
        grid_spec=grid_spec,
        out_shape=jax.ShapeDtypeStruct((k, n), I32),
        compiler_params=pltpu.CompilerParams(dimension_semantics=("arbitrary",)),
        name="dest",
    )(pstart, eidx, slot)


def _scatter_kernel(pfill_ref, pend_ref, nv_ref, dest_ref, h_hbm, xs_ref, zero_scr, h_buf, sems, lsem,
                    zsem):
    i = pl.program_id(0)
    n_steps = pl.num_programs(0)
    ts = h_buf.shape[1]
    tb = zero_scr.shape[0]
    n_tail = xs_ref.shape[0] // tb - nv_ref[0]

    def pad_fill(e, wait):
        def go(src, dst):
            cp = pltpu.make_async_copy(src, dst, zsem)
            cp.wait() if wait else cp.start()

        start = pfill_ref[e]
        end = pend_ref[e]
        head = jnp.minimum((-start) & 7, end - start)
        for r in range(7):
            @pl.when(r < head)
            def _():
                go(zero_scr.at[pl.ds(0, 1)], xs_ref.at[pl.ds(start + r, 1)])
        off = start + head
        rem = end - off
        size = tb // 2
        while size >= 8:
            cond = (rem & size) != 0

            @pl.when(cond)
            def _():
                go(zero_scr.at[pl.ds(0, size)], xs_ref.at[pl.ds(pl.multiple_of(off, 8), size)])
            off = off + jnp.where(cond, size, 0)
            size //= 2

    @pl.when(i == 0)
    def _():
        zero_scr[...] = jnp.zeros(zero_scr.shape, U32)

        def fill(e, carry):
            pad_fill(e, False)
            return carry

        lax.fori_loop(0, N_EXPERTS, fill, 0)

        def fill_tail(j, carry):
            start = pl.multiple_of((nv_ref[0] + j) * tb, tb)
            pltpu.make_async_copy(zero_scr, xs_ref.at[pl.ds(start, tb)], zsem).start()
            return carry

        lax.fori_loop(0, n_tail, fill_tail, 0)

        def drain(e, carry):
            pad_fill(e, True)
            return carry

        lax.fori_loop(0, N_EXPERTS, drain, 0)

        def drain_tail(j, carry):
            pltpu.make_async_copy(zero_scr, xs_ref.at[pl.ds(0, tb)], zsem).wait()
            return carry

        lax.fori_loop(0, n_tail, drain_tail, 0)

    def load(j):
        return pltpu.make_async_copy(h_hbm.at[pl.ds(pl.multiple_of(j * ts, ts), ts)],
                                     h_buf.at[lax.rem(j, H_SLOTS)], lsem.at[lax.rem(j, H_SLOTS)])

    def drain_scatters(j):
        sl = lax.rem(j, H_SLOTS)
        for k in range(TOP_K):
            pltpu.make_async_copy(h_buf.at[sl], xs_ref.at[pl.ds(0, ts)], sems.at[sl]).wait()

    @pl.when(i == 0)
    def _():
        load(0).start()

    @pl.when(i + 1 < n_steps)
    def _():
        load(i + 1).start()

    load(i).wait()
    cur = lax.rem(i, H_SLOTS)

    def issue(t, carry):
        for k in range(TOP_K):
            pltpu.make_async_copy(h_buf.at[cur, pl.ds(t, 1)], xs_ref.at[pl.ds(dest_ref[k, t], 1)],
                                  sems.at[cur]).start(priority=k % 2)
        return carry

    lax.fori_loop(0, ts, issue, 0)

    @pl.when(i >= 1)
    def _():
        drain_scatters(i - 1)

    @pl.when(i == n_steps - 1)
    def _():
        drain_scatters(i)


def _scatter(pfill, pend, n_valid, dest, hp, n_rows):
    n, w = hp.shape
    ts = T_ROW
    grid_spec = pltpu.PrefetchScalarGridSpec(
        num_scalar_prefetch=3,
        grid=(n // ts,),
        in_specs=[pl.BlockSpec((TOP_K, ts), lambda i, *_: (0, i), memory_space=pltpu.SMEM),
                  pl.BlockSpec(memory_space=pl.ANY)],
        out_specs=pl.BlockSpec(memory_space=pl.ANY),
        scratch_shapes=[pltpu.VMEM((TB_EXP, w), U32), pltpu.VMEM((H_SLOTS, ts, w), U32),
                        pltpu.SemaphoreType.DMA((H_SLOTS,)), pltpu.SemaphoreType.DMA((H_SLOTS,)),
                        pltpu.SemaphoreType.DMA],
    )
    return pl.pallas_call(
        _scatter_kernel,
        grid_spec=grid_spec,
        out_shape=jax.ShapeDtypeStruct((n_rows, w), U32),
        compiler_params=pltpu.CompilerParams(dimension_semantics=("arbitrary",),
                                             vmem_limit_bytes=VMEM_LIMIT),
        name="scatter",
    )(pfill, pend, n_valid, dest, hp)


def _experts_kernel(be_ref, nv_ref, ge_ref, ng_ref, xs_ref, wg_hbm, wu_hbm, wd_hbm, y_ref,
                    wg_f, wu_f, wd_f, wg_bf, wu_bf, wd_bf, sems, gctr):
    i = pl.program_id(0)
    prev = jnp.maximum(i - 1, 0)
    valid = i < nv_ref[0]
    fresh = valid & ((i == 0) | (be_ref[i] != be_ref[prev]))

    def weight_copies(g, slot):
        e = ge_ref[g]
        return (pltpu.make_async_copy(wg_hbm.at[e], wg_f.at[slot], sems.at[slot, 0]),
                pltpu.make_async_copy(wu_hbm.at[e], wu_f.at[slot], sems.at[slot, 1]),
                pltpu.make_async_copy(wd_hbm.at[e], wd_f.at[slot], sems.at[slot, 2]))

    @pl.when(i == 0)
    def _():
        gctr[0] = 0
        for g in range(W_SLOTS):
            @pl.when(g < ng_ref[0])
            def _():
                for cp in weight_copies(g, g):
                    cp.start()

    @pl.when(fresh)
    def _():
        g = gctr[0]
        slot = lax.rem(g, W_SLOTS)
        for cp in weight_copies(g, slot):
            cp.wait()
        wg_bf[...] = wg_f[slot].astype(BF16)
        wu_bf[...] = wu_f[slot].astype(BF16)
        wd_bf[...] = wd_f[slot].astype(BF16)

        @pl.when(g + W_SLOTS < ng_ref[0])
        def _():
            for cp in weight_copies(g + W_SLOTS, slot):
                cp.start()

        gctr[0] = g + 1

    @pl.when(valid)
    def _():
        lo, hi = _unpack_pair(xs_ref[...])
        x = jnp.concatenate([lo.astype(BF16), hi.astype(BF16)], axis=1)
        hg = jnp.dot(x, wg_bf[...], preferred_element_type=F32)
        hu = jnp.dot(x, wu_bf[...], preferred_element_type=F32)
        a = (_silu(hg) * hu).astype(BF16)
        y = jnp.dot(a, wd_bf[...], preferred_element_type=F32)
        half = y.shape[1] // 2
        y_ref[...] = _pack_pair(y[:, :half], y[:, half:])

    @pl.when(jnp.logical_not(valid))
    def _():
        y_ref[...] = jnp.zeros(y_ref.shape, U32)


def _experts(block_e, n_valid, group_e, n_groups, xs, w_gate, w_up, w_down, n_blocks):
    tb = TB_EXP
    w = xs.shape[1]
    _, d, f = w_gate.shape
    grid_spec = pltpu.PrefetchScalarGridSpec(
        num_scalar_prefetch=4,
        grid=(n_blocks,),
        in_specs=[pl.BlockSpec((tb, w), lambda i, be, nv, ge, ng: (jnp.minimum(i, nv[0] - 1), 0)),
                  pl.BlockSpec(memory_space=pl.ANY),
                  pl.BlockSpec(memory_space=pl.ANY),
                  pl.BlockSpec(memory_space=pl.ANY)],
        out_specs=pl.BlockSpec((tb, w), lambda i, be, nv, ge, ng: (i, 0)),
        scratch_shapes=[pltpu.VMEM((W_SLOTS, d, f), F32), pltpu.VMEM((W_SLOTS, d, f), F32),
                        pltpu.VMEM((W_SLOTS, f, d), F32),
                        pltpu.VMEM((d, f), BF16), pltpu.VMEM((d, f), BF16), pltpu.VMEM((f, d), BF16),
                        pltpu.SemaphoreType.DMA((W_SLOTS, 3)), pltpu.SMEM((1,), I32)],
    )
    return pl.pallas_call(
        _experts_kernel,
        grid_spec=grid_spec,
        out_shape=jax.ShapeDtypeStruct((n_blocks * tb, w), U32),
        compiler_params=pltpu.CompilerParams(dimension_semantics=("arbitrary",),
                                             vmem_limit_bytes=VMEM_LIMIT),
        name="experts",
    )(block_e, n_valid, group_e, n_groups, xs, w_gate, w_up, w_down)


def _combine_kernel(dcur_ref, dnxt_ref, x1_ref, hp_ref, gate_ref, gt_ref,
                    wsg_ref, wsu_ref, wsd_ref, y_ref, o_ref, buf, sems):
    i = pl.program_id(0)
    tc = x1_ref.shape[0]
    slot = lax.rem(i, 2)

    def issue(d_ref, sl):
        def body(t, carry):
            for k in range(TOP_K):
                pltpu.make_async_copy(y_ref.at[pl.ds(d_ref[k, t], 1)], buf.at[sl, k, pl.ds(t, 1)],
                                      sems.at[sl]).start(priority=k % 2)
            return carry

        lax.fori_loop(0, tc, body, 0)

    @pl.when(i == 0)
    def _():
        issue(dcur_ref, 0)

    @pl.when(i + 1 < pl.num_programs(0))
    def _():
        issue(dnxt_ref, 1 - slot)

    lo, hi = _unpack_pair(hp_ref[...])
    x = jnp.concatenate([lo.astype(BF16), hi.astype(BF16)], axis=1)
    hg = jnp.dot(x, wsg_ref[...], preferred_element_type=F32)
    hu = jnp.dot(x, wsu_ref[...], preferred_element_type=F32)
    a = (_silu(hg) * hu).astype(BF16)
    shared = jnp.dot(a, wsd_ref[...], preferred_element_type=F32)

    for k in range(TOP_K):
        pltpu.make_async_copy(y_ref.at[pl.ds(0, tc)], buf.at[slot, k], sems.at[slot]).wait()

    half = shared.shape[1] // 2
    gate = gate_ref[...]
    r_lo = shared[:, :half]
    r_hi = shared[:, half:]
    for k in range(TOP_K):
        lo, hi = _unpack_pair(buf[slot, k])
        gk = gate[:, k:k + 1]
        r_lo = r_lo + gk * lo
        r_hi = r_hi + gk * hi
    gt = gt_ref[0]
    o_ref[:, :half] = x1_ref[:, :half] + gt[:, :half] * r_lo
    o_ref[:, half:] = x1_ref[:, half:] + gt[:, half:] * r_hi


def _combine(dest, x1, hp, gate_t, gt2, wsg, wsu, wsd, y, seq):
    n, d = x1.shape
    tc = T_ROW
    w = hp.shape[1]
    f = wsg.shape[1]
    tiles_per_batch = seq // tc
    last = n // tc - 1
    return pl.pallas_call(
        _combine_kernel,
        grid=(n // tc,),
        in_specs=[pl.BlockSpec((TOP_K, tc), lambda i: (0, i), memory_space=pltpu.SMEM),
                  pl.BlockSpec((TOP_K, tc), lambda i: (0, jnp.minimum(i + 1, last)),
                               memory_space=pltpu.SMEM),
                  pl.BlockSpec((tc, d), lambda i: (i, 0)),
                  pl.BlockSpec((tc, w), lambda i: (i, 0)),
                  pl.BlockSpec((tc, TOP_K), lambda i: (i, 0)),
                  pl.BlockSpec((1, 1, d), lambda i: (i // tiles_per_batch, 0, 0)),
                  pl.BlockSpec((d, f), lambda i: (0, 0)),
                  pl.BlockSpec((d, f), lambda i: (0, 0)),
                  pl.BlockSpec((f, d), lambda i: (0, 0)),
                  pl.BlockSpec(memory_space=pl.ANY)],
        out_specs=pl.BlockSpec((tc, d), lambda i: (i, 0)),
        scratch_shapes=[pltpu.VMEM((2, TOP_K, tc, w), U32), pltpu.SemaphoreType.DMA((2,))],
        out_shape=jax.ShapeDtypeStruct((n, d), F32),
        compiler_params=pltpu.CompilerParams(dimension_semantics=("arbitrary",),
                                             vmem_limit_bytes=VMEM_LIMIT),
        name="combine",
    )(dest, dest, x1, hp, gate_t, gt2, wsg, wsu, wsd, y)


def kernel(x, c, positions, rel_bias, hgrn_lb_logits, w_ada, b_ada, g_mix, w_in, g_q, g_k, lam_q1, lam_k1, lam_q2, lam_k2, g_sub, g_hgrn, w_out, g_ffn, w_router, router_bias, w_exp_gate, w_exp_up, w_exp_down, w_sh_gate, w_sh_up, w_sh_down):
    batch, seq, d = x.shape
    n = batch * seq
    layer = 0
    x2d = x.reshape(n, d)

    c_pad = jnp.zeros((8, d), F32).at[:batch].set(c.astype(F32))
    mod = _ada(c_pad, w_ada[layer], b_ada[layer][None, :])[:batch]
    sh1, sc1, gt1, sh2, sc2, gt2 = [m.reshape(batch, 1, d) for m in jnp.split(mod, 6, axis=-1)]

    lbs = jnp.cumsum(jax.nn.softmax(hgrn_lb_logits.astype(F32), axis=1), axis=1)[:, layer]
    lbs = lbs.reshape(2, 1, SEG)
    reps = SEG // QK_DIM
    qk_gain = jnp.stack([jnp.tile(g_q[layer].astype(F32), reps) * (QK_DIM ** -0.5 * LOG2E),
                         jnp.tile(g_k[layer].astype(F32), reps)]).reshape(2, 1, SEG)
    lane = jnp.arange(LANES)
    g64 = jnp.where((lane[:, None] // QK_DIM) == (lane[None, :] // QK_DIM), 1.0 / QK_DIM, 0.0).astype(BF16)
    lam = (jnp.exp(jnp.sum(lam_q1[layer].astype(F32) * lam_k1[layer].astype(F32)))
           - jnp.exp(jnp.sum(lam_q2[layer].astype(F32) * lam_k2[layer].astype(F32)))
           + LAM_INIT).reshape(1)

    p, lf = _inproj(x2d, sc1, sh1, g_mix[layer][None, :], w_in[layer].astype(BF16), qk_gain, lbs, g64, seq)

    nt = seq // T_ATT
    pos_sub = positions.astype(I32).reshape(batch * seq // T_SUB, T_SUB)
    smin = jnp.min(pos_sub, axis=1)
    smax = jnp.max(pos_sub, axis=1)
    posq = positions.astype(I32).reshape(batch, nt, 1, T_ATT)
    posk = positions.astype(I32).reshape(batch, seq, 1)
    rb_t = rel_bias.astype(F32).T * LOG2E
    rb_tab = jnp.zeros((N_HEADS, LANES), F32).at[:, :REL_BUCKETS].set(rb_t)
    oa = _attention(p, smin, smax, posq, posk, rb_tab, rb_t, lam,
                    g_sub[layer][:, None].astype(F32), batch, seq)

    ci = jnp.arange(C_HGRN)
    tril = (ci[None, :] <= ci[:, None]).astype(BF16)
    triu = (ci[None, :] >= ci[:, None]).astype(BF16)
    oh = _hgrn(p, lf, tril, triu, g_hgrn[layer][None, :].astype(F32), batch, seq)

    half = d // 2
    w_out_bf = w_out[layer].astype(BF16)
    wr_t = w_router[layer].astype(F32).T
    wr_hi = wr_t.astype(BF16)
    wr_lo = (wr_t - wr_hi.astype(F32)).astype(BF16)
    ti = jnp.arange(TM_OUT)
    upper = (ti[:, None] < ti[None, :]).astype(BF16)
    ones = jnp.ones((TM_OUT, LANES), BF16)
    x1, hp, eidx, slot, gate, cnt = _outproj(
        oa, oh, w_out_bf[:half], w_out_bf[half:], x2d, gt1, g_ffn[layer][None, :], sc2, sh2,
        wr_hi, wr_lo, router_bias[layer].astype(F32)[:, None], upper, ones, seq)

    tb = TB_EXP
    counts = cnt[:, 0].astype(I32)
    padded = (counts + tb - 1) // tb * tb
    pends = jnp.cumsum(padded)
    pstart = (pends - padded).astype(I32)
    n_blocks = (n * TOP_K) // tb + N_EXPERTS
    n_valid = (pends[-1] // tb).astype(I32).reshape(1)
    blk_start = jnp.arange(n_blocks, dtype=I32) * tb
    block_e = jnp.minimum(jnp.sum(pends[None, :] <= blk_start[:, None], axis=1), N_EXPERTS - 1).astype(I32)
    pfill = (pstart + counts).astype(I32)
    pend = pends.astype(I32)
    dest = _dest(pstart, eidx, slot)
    has_rows = counts > 0
    group_e = jnp.nonzero(has_rows, size=N_EXPERTS, fill_value=0)[0].astype(I32)
    n_groups = jnp.sum(has_rows).astype(I32).reshape(1)

    xs = _scatter(pfill, pend, n_valid, dest, hp, n_blocks * tb)
    y = _experts(block_e, n_valid, group_e, n_groups, xs,
                 w_exp_gate[layer], w_exp_up[layer], w_exp_down[layer], n_blocks)
    out = _combine(dest, x1, hp, gate.T, gt2,
                   w_sh_gate[layer].astype(BF16), w_sh_up[layer].astype(BF16),
                   w_sh_down[layer].astype(BF16), y, seq)
    return out.reshape(batch, seq, d)
```

```python
import functools
import math

import jax
import jax.numpy as jnp
from jax import lax
from jax.experimental import pallas as pl
from jax.experimental.pallas import tpu as pltpu

F32 = jnp.float32
BF16 = jnp.bfloat16
I32 = jnp.int32
U32 = jnp.uint32

D_MODEL = 2048
N_HEADS = 8
QK_DIM = 64
HEAD_DIM = 128
SEG = 1024
N_SEG = 8
REL_BUCKETS = 32
REL_MAX_DIST = 128
N_EXPERTS = 256
TOP_K = 8
N_GROUPS = 8
TOPK_GROUPS = 4
GROUP_SIZE = N_EXPERTS // N_GROUPS
EXPERT_DIM = 512
ROUTED_SCALE = 2.5
EPS = 1e-6
LAM_INIT = 0.8 - 0.6 * math.exp(-0.3 * 0)
LOG2E = math.log2(math.e)

LANES = 128
VMEM_LIMIT = 56 * 1024 * 1024

TM_IN = 512
IN_CHUNK = 256
T_ATT = 512
T_KEY = 1024
T_SUB = 128
V_PAD = 16
C_HGRN = 64
HGRN_GROUP = 8
TM_OUT = 256
T_ROW = 256
TB_EXP = 256
W_SLOTS = 2
H_SLOTS = 3
NEG_BIG = -1e30
EXP_CLAMP = 80.0
FAST_BOUND = 60.0
BOUND_SLACK = 1.02
BOUND_PAD = 0.01

NT_DIMS = (((1,), (1,)), ((), ()))


def _silu(x):
    return x * jax.nn.sigmoid(x)


def _pack_pair(lo_f32, hi_f32):
    lo = lax.bitcast_convert_type(lo_f32.astype(BF16).astype(F32), U32)
    hi = lax.bitcast_convert_type(hi_f32.astype(BF16).astype(F32), U32)
    return (hi & jnp.uint32(0xFFFF0000)) | (lo >> 16)


def _unpack_pair(word):
    lo = lax.bitcast_convert_type(word << 16, F32)
    hi = lax.bitcast_convert_type(word & jnp.uint32(0xFFFF0000), F32)
    return lo, hi


def _ada_kernel(c_ref, w_ref, b_ref, o_ref):
    a = _silu(c_ref[...]).astype(BF16)
    o_ref[...] = jnp.dot(a, w_ref[...].astype(BF16), preferred_element_type=F32) + b_ref[...]


def _ada(c_pad, w, b):
    d, n = w.shape
    tn = 1024
    return pl.pallas_call(
        _ada_kernel,
        grid=(n // tn,),
        in_specs=[pl.BlockSpec((8, d), lambda j: (0, 0)),
                  pl.BlockSpec((d, tn), lambda j: (0, j)),
                  pl.BlockSpec((1, tn), lambda j: (0, j))],
        out_specs=pl.BlockSpec((8, tn), lambda j: (0, j)),
        out_shape=jax.ShapeDtypeStruct((8, n), F32),
        compiler_params=pltpu.CompilerParams(dimension_semantics=("arbitrary",),
                                             vmem_limit_bytes=VMEM_LIMIT),
        name="ada",
    )(c_pad, w, b)


def _inproj_kernel(x_ref, sc_ref, sh_ref, gmix_ref, w_ref, qkg_ref, lb_ref, g64_ref,
                   p_ref, lf_ref, h_scr):
    j = pl.program_id(1)

    @pl.when(j == 0)
    def _():
        x = x_ref[...]
        ms = jnp.mean(x * x, axis=-1, keepdims=True)
        y = x * lax.rsqrt(ms + EPS) * gmix_ref[...]
        h_scr[...] = (y * (1.0 + sc_ref[0]) + sh_ref[0]).astype(BF16)

    def chunks(epilogue):
        for c in range(SEG // IN_CHUNK):
            sl = slice(c * IN_CHUNK, (c + 1) * IN_CHUNK)
            epilogue(sl, jnp.dot(h_scr[...], w_ref[:, sl], preferred_element_type=F32))

    @pl.when(j < 2)
    def _():
        acc = jnp.dot(h_scr[...], w_ref[...], preferred_element_type=F32)
        gain = qkg_ref[0]
        for c in range(SEG // LANES):
            sl = slice(c * LANES, (c + 1) * LANES)
            xs = acc[:, sl]
            ms = jnp.dot((xs * xs).astype(BF16), g64_ref[...], preferred_element_type=F32)
            p_ref[:, sl] = (xs * lax.rsqrt(ms + EPS) * gain[:, sl]).astype(BF16)

    @pl.when((j == 2) | (j == 4))
    def _():
        def plain(sl, acc):
            p_ref[:, sl] = acc.astype(BF16)

        chunks(plain)

    @pl.when((j == 3) | (j == 7))
    def _():
        def silu(sl, acc):
            p_ref[:, sl] = _silu(acc).astype(BF16)

        chunks(silu)

    @pl.when((j == 5) | (j == 6))
    def _():
        def log_gate(sl, z):
            lb = lb_ref[0, :, sl]
            f = lb + (1.0 - lb) * jax.nn.sigmoid(z)
            lf_ref[:, sl] = jnp.log(f)
            p_ref[:, sl] = z.astype(BF16)

        chunks(log_gate)


def _inproj(x2d, sc1, sh1, g_mix, w_in_bf, qk_gain, lbs, g64, seq):
    n, d = x2d.shape
    tm = TM_IN
    tiles_per_batch = seq // tm
    return pl.pallas_call(
        _inproj_kernel,
        grid=(n // tm, N_SEG),
        in_specs=[
            pl.BlockSpec((tm, d), lambda i, j: (i, 0)),
            pl.BlockSpec((1, 1, d), lambda i, j: (i // tiles_per_batch, 0, 0)),
            pl.BlockSpec((1, 1, d), lambda i, j: (i // tiles_per_batch, 0, 0)),
            pl.BlockSpec((1, d), lambda i, j: (0, 0)),
            pl.BlockSpec((d, SEG), lambda i, j: (0, j)),
            pl.BlockSpec((1, 1, SEG), lambda i, j: (jnp.minimum(j, 1), 0, 0)),
            pl.BlockSpec((1, 1, SEG), lambda i, j: (jnp.clip(j - 5, 0, 1), 0, 0)),
            pl.BlockSpec((LANES, LANES), lambda i, j: (0, 0)),
        ],
        out_specs=[
            pl.BlockSpec((tm, SEG), lambda i, j: (i, j)),
            pl.BlockSpec((tm, SEG), lambda i, j: (i, jnp.clip(j - 5, 0, 1))),
        ],
        out_shape=[jax.ShapeDtypeStruct((n, N_SEG * SEG), BF16),
                   jax.ShapeDtypeStruct((n, 2 * SEG), F32)],
        scratch_shapes=[pltpu.VMEM((tm, d), BF16)],
        compiler_params=pltpu.CompilerParams(dimension_semantics=("arbitrary", "arbitrary"),
                                             vmem_limit_bytes=VMEM_LIMIT),
        name="inproj",
    )(x2d, sc1, sh1, g_mix, w_in_bf, qk_gain, lbs, g64)


def _t5_bias_tile(pos_q, pos_k, table):
    half = REL_BUCKETS // 2
    max_exact = half // 2
    rel = pos_k - pos_q
    n = jnp.abs(rel)
    nf = jnp.maximum(n, 1).astype(F32)
    large = max_exact + (jnp.log(nf / max_exact) / math.log(REL_MAX_DIST / max_exact)
                         * (half - max_exact)).astype(I32)
    large = jnp.minimum(large, half - 1)
    bucket = jnp.where(rel > 0, half, 0) + jnp.where(n < max_exact, n, large)
    rows = bucket.shape[0]
    tbl = jnp.broadcast_to(table, (rows, LANES))
    cols = [jnp.take_along_axis(tbl, bucket[:, c * LANES:(c + 1) * LANES], axis=1)
            for c in range(bucket.shape[1] // LANES)]
    return jnp.concatenate(cols, axis=1)


def _attn_kernel(smin_ref, smax_ref, q_ref, k_ref, v_ref, posq_ref, posk_ref, rbt_ref, rb_ref, lam_ref,
                 gsub_ref, o_ref, vt_scr, s_a, s_b, cm_a, cm_b, p_a, p_b, al_a, al_b, m_scr, a_scr,
                 kmax_scr):
    b = pl.program_id(0)
    h = pl.program_id(1)
    i = pl.program_id(2)
    tq = T_ATT
    tk = T_KEY
    n_sub = tk // T_SUB
    nq_sub = tq // T_SUB
    ntk = k_ref.shape[0] // tk
    n_pairs = ntk // 2
    subs_per_batch = k_ref.shape[0] // T_SUB

    @pl.when(i == 0)
    def _():
        ones_row = jnp.where(lax.broadcasted_iota(I32, (V_PAD, tk), 0) == 0, 1.0, 0.0).astype(BF16)
        kmax_scr[...] = jnp.zeros(kmax_scr.shape, F32)

        def tr(c, carry):
            r0 = pl.multiple_of(c * tk, tk)
            vt_scr[c, :HEAD_DIM, :] = v_ref[pl.ds(r0, tk), :].astype(F32).T.astype(BF16)
            vt_scr[c, HEAD_DIM:, :] = ones_row
            kf = k_ref[pl.ds(r0, tk), :].astype(F32)
            for mp in range(2):
                km = kf[:, mp * QK_DIM:(mp + 1) * QK_DIM]
                nk = jnp.dot(km * km, jnp.ones((QK_DIM, LANES), F32), preferred_element_type=F32)
                kmax_scr[mp] = jnp.maximum(kmax_scr[mp], jnp.max(nk, axis=0, keepdims=True))
            return carry

        lax.fori_loop(0, ntk, tr, 0)

    q = q_ref[...]
    qs = (q[:, :QK_DIM], q[:, QK_DIM:])
    m_scr[...] = jnp.full(m_scr.shape, NEG_BIG, F32)
    a_scr[...] = jnp.zeros(a_scr.shape, F32)

    sub0 = b * subs_per_batch + i * nq_sub
    q_lo = smin_ref[sub0]
    q_hi = smax_ref[sub0]
    for u in range(1, nq_sub):
        q_lo = jnp.minimum(q_lo, smin_ref[sub0 + u])
        q_hi = jnp.maximum(q_hi, smax_ref[sub0 + u])
    c_pos = rb_ref[h, REL_BUCKETS - 1]
    c_neg = rb_ref[h, REL_BUCKETS // 2 - 1]
    pos_q = posq_ref[0, 0]

    def classify(j, u):
        ksub = b * subs_per_batch + j * n_sub + u
        lo = smin_ref[ksub] - q_hi
        hi = smax_ref[ksub] - q_lo
        far = (lo >= REL_MAX_DIST) | (hi <= -REL_MAX_DIST)
        shift = jnp.where(lo >= REL_MAX_DIST, c_pos, jnp.where(hi <= -REL_MAX_DIST, c_neg, 0.0))
        return far, shift

    def scores(j, s_ref, cm_ref):
        kk = k_ref[pl.ds(pl.multiple_of(j * tk, tk), tk), :]
        ks = (kk[:, :QK_DIM], kk[:, QK_DIM:])
        for mp in range(2):
            s = lax.dot_general(ks[mp], qs[mp], NT_DIMS, preferred_element_type=F32)
            s_ref[mp] = s
            for u in range(n_sub):
                _, shift = classify(j, u)
                cm_ref[mp, u] = jnp.max(s[u * T_SUB:(u + 1) * T_SUB], axis=0, keepdims=True) + shift

    def fixup(j, s_ref, cm_ref):
        for u in range(n_sub):
            far, _ = classify(j, u)

            @pl.when(jnp.logical_not(far))
            def _():
                rows = pl.ds(u * T_SUB, T_SUB)
                pos_k = posk_ref[0, pl.ds(pl.multiple_of(j * tk + u * T_SUB, T_SUB), T_SUB), :]
                bias = _t5_bias_tile(pos_q, pos_k, rbt_ref[pl.ds(h, 1), :])
                for mp in range(2):
                    sb = s_ref[mp, rows, :] + bias
                    s_ref[mp, rows, :] = sb
                    cm_ref[mp, u] = jnp.max(sb, axis=0, keepdims=True)

    def soft(j, s_ref, cm_ref, p_ref, al_ref):
        for mp in range(2):
            m_old = m_scr[mp]
            m_new = m_old
            for u in range(n_sub):
                m_new = jnp.maximum(m_new, cm_ref[mp, u])
            for u in range(n_sub):
                rows = pl.ds(u * T_SUB, T_SUB)
                _, shift = classify(j, u)
                p_ref[mp, rows, :] = jnp.exp2(s_ref[mp, rows, :] - (m_new - shift)).astype(BF16)
            m_scr[mp] = m_new
            al_ref[mp] = jnp.exp2(m_old - m_new)

    def pv(j, p_ref, al_ref):
        vt = vt_scr[j]
        for mp in range(2):
            a_scr[mp] = al_ref[mp] * a_scr[mp] + jnp.dot(vt, p_ref[mp], preferred_element_type=F32)

    sbuf = ((s_a, cm_a), (s_b, cm_b))
    pbuf = ((p_a, al_a), (p_b, al_b))
    last = 2 * (n_pairs - 1)

    def online_path():
        def step(j, par):
            scores(j + 2, *sbuf[par])
            soft(j + 1, *sbuf[1 - par], *pbuf[1 - par])
            pv(j, *pbuf[par])
            fixup(j + 2, *sbuf[par])

        scores(0, *sbuf[0])
        fixup(0, *sbuf[0])
        scores(1, *sbuf[1])
        soft(0, *sbuf[0], *pbuf[0])
        fixup(1, *sbuf[1])

        def body(jj, carry):
            step(2 * jj, 0)
            step(2 * jj + 1, 1)
            return carry

        lax.fori_loop(0, n_pairs - 1, body, 0)
        soft(last + 1, *sbuf[1], *pbuf[1])
        pv(last, *pbuf[0])
        pv(last + 1, *pbuf[1])

    def col_bound(mp):
        qf = qs[mp].astype(F32)
        nq = lax.dot_general(jnp.ones((8, QK_DIM), F32), qf * qf, NT_DIMS,
                             preferred_element_type=F32)[0:1, :]
        return jnp.sqrt(nq * kmax_scr[mp, :, 0:1]) * BOUND_SLACK

    b_max = rb_ref[h, 0]
    for e in range(1, REL_BUCKETS):
        b_max = jnp.maximum(b_max, rb_ref[h, e])
    bounds = [col_bound(mp) + (b_max + BOUND_PAD) for mp in range(2)]
    bound_max = jnp.max(jnp.maximum(bounds[0], bounds[1]))

    def fast_scores(j, p_ref):
        kk = k_ref[pl.ds(pl.multiple_of(j * tk, tk), tk), :]
        ks = (kk[:, :QK_DIM], kk[:, QK_DIM:])
        for mp in range(2):
            s = lax.dot_general(ks[mp], qs[mp], NT_DIMS, preferred_element_type=F32)
            for u in range(n_sub):
                _, shift = classify(j, u)
                p_ref[mp, pl.ds(u * T_SUB, T_SUB), :] = jnp.exp2(
                    s[u * T_SUB:(u + 1) * T_SUB] - (bounds[mp] - shift)).astype(BF16)

    def fast_fixup(j, p_ref):
        for u in range(n_sub):
            far, _ = classify(j, u)

            @pl.when(jnp.logical_not(far))
            def _():
                rows = pl.ds(u * T_SUB, T_SUB)
                pos_k = posk_ref[0, pl.ds(pl.multiple_of(j * tk + u * T_SUB, T_SUB), T_SUB), :]
                scale = jnp.exp2(_t5_bias_tile(pos_q, pos_k, rbt_ref[pl.ds(h, 1), :]))
                for mp in range(2):
                    p_ref[mp, rows, :] = (p_ref[mp, rows, :].astype(F32) * scale).astype(BF16)

    def fast_pv(j, p_ref):
        vt = vt_scr[j]
        for mp in range(2):
            a_scr[mp] = a_scr[mp] + jnp.dot(vt, p_ref[mp], preferred_element_type=F32)

    def fast_path():
        pb = (p_a, p_b)

        def step(j, par):
            fast_scores(j + 1, pb[1 - par])
            fast_pv(j, pb[par])
            fast_fixup(j + 1, pb[1 - par])

        fast_scores(0, pb[0])
        fast_fixup(0, pb[0])

        def body(jj, carry):
            step(2 * jj, 0)
            step(2 * jj + 1, 1)
            return carry

        lax.fori_loop(0, n_pairs - 1, body, 0)
        step(last, 0)
        fast_pv(last + 1, pb[1])

    use_fast = bound_max <= FAST_BOUND

    @pl.when(use_fast)
    def _():
        fast_path()

    @pl.when(jnp.logical_not(use_fast))
    def _():
        online_path()

    num = [a_scr[mp, :HEAD_DIM, :] / a_scr[mp, HEAD_DIM:HEAD_DIM + 1, :] for mp in range(2)]
    o = num[0] - lam_ref[0] * num[1]
    ms = jnp.mean(o * o, axis=0, keepdims=True)
    o = o * lax.rsqrt(ms + EPS) * (gsub_ref[...] * (1.0 - LAM_INIT))
    o_ref[...] = o.T.astype(BF16)


def _attention(p, smin, smax, posq, posk, rb_tab, rb_t, lam, g_sub, batch, seq):
    t = T_ATT
    tk = T_KEY
    nt = seq // t
    ntk = seq // tk
    assert seq % (2 * tk) == 0
    n = batch * seq
    n_sub = tk // T_SUB
    va = HEAD_DIM + V_PAD
    grid_spec = pltpu.PrefetchScalarGridSpec(
        num_scalar_prefetch=2,
        grid=(batch, N_HEADS, nt),
        in_specs=[
            pl.BlockSpec((t, HEAD_DIM), lambda b, h, i, *_: (b * nt + i, h)),
            pl.BlockSpec((seq, HEAD_DIM), lambda b, h, i, *_: (b, N_HEADS + h)),
            pl.BlockSpec((seq, HEAD_DIM), lambda b, h, i, *_: (b, 2 * N_HEADS + h)),
            pl.BlockSpec((1, 1, 1, t), lambda b, h, i, *_: (b, i, 0, 0)),
            pl.BlockSpec((1, seq, 1), lambda b, h, i, *_: (b, 0, 0)),
            pl.BlockSpec((N_HEADS, LANES), lambda b, h, i, *_: (0, 0)),
            pl.BlockSpec(memory_space=pltpu.SMEM),
            pl.BlockSpec(memory_space=pltpu.SMEM),
            pl.BlockSpec((HEAD_DIM, 1), lambda b, h, i, *_: (0, 0)),
        ],
        out_specs=pl.BlockSpec((t, HEAD_DIM), lambda b, h, i, *_: (b * nt + i, h)),
        scratch_shapes=[pltpu.VMEM((ntk, va, tk), BF16),
                        pltpu.VMEM((2, tk, t), F32),
                        pltpu.VMEM((2, tk, t), F32),
                        pltpu.VMEM((2, n_sub, 1, t), F32),
                        pltpu.VMEM((2, n_sub, 1, t), F32),
                        pltpu.VMEM((2, tk, t), BF16),
                        pltpu.VMEM((2, tk, t), BF16),
                        pltpu.VMEM((2, 1, t), F32),
                        pltpu.VMEM((2, 1, t), F32),
                        pltpu.VMEM((2, 1, t), F32),
                        pltpu.VMEM((2, va, t), F32),
                        pltpu.VMEM((2, 1, LANES), F32)],
    )
    return pl.pallas_call(
        _attn_kernel,
        grid_spec=grid_spec,
        out_shape=jax.ShapeDtypeStruct((n, N_HEADS * HEAD_DIM), BF16),
        compiler_params=pltpu.CompilerParams(
            dimension_semantics=("arbitrary", "arbitrary", "arbitrary"),
            vmem_limit_bytes=VMEM_LIMIT),
        name="attn",
    )(smin, smax, p, p, p, posq, posk, rb_tab, rb_t, lam, g_sub)


def _hgrn_group(r0, forward, q_ref, v_ref, lf_ref, tri_ref, st):
    c = C_HGRN
    order = range(HGRN_GROUP) if forward else range(HGRN_GROUP - 1, -1, -1)
    rows = [pl.ds(r0 + k * c, c) for k in order]
    tri = tri_ref[...]
    row = lax.broadcasted_iota(I32, (c, c), 0)
    col = lax.broadcasted_iota(I32, (c, c), 1)
    keep = (col <= row) if forward else (col >= row)

    gs = [lf_ref[r, :] for r in rows]
    bsums = []
    for g in gs:
        g_hi = g.astype(BF16)
        g_lo = (g - g_hi.astype(F32)).astype(BF16)
        bsums.append(jnp.dot(tri, g_hi, preferred_element_type=F32)
                     + jnp.dot(tri, g_lo, preferred_element_type=F32))

    q_in, q_t, k_t, k_st, v_t, vs, decay = [], [], [], [], [], [], []
    for r, g, bsum in zip(rows, gs, bsums):
        if forward:
            ref = bsum[c // 2 - 1:c // 2, :]
            b_end = bsum[c - 1:c, :]
        else:
            ref = bsum[c // 2:c // 2 + 1, :]
            b_end = bsum[0:1, :]
        q = q_ref[r, :].astype(F32)
        v = v_ref[r, :]
        kf = 1.0 - jnp.exp(g)
        q_in.append((q * jnp.exp(bsum)).astype(BF16))
        q_t.append((q * jnp.exp(jnp.minimum(bsum - ref, EXP_CLAMP))).astype(BF16))
        k_t.append((kf * jnp.exp(jnp.minimum(ref - bsum, EXP_CLAMP))).astype(BF16))
        k_st.append((kf * jnp.exp(b_end - bsum)).astype(BF16))
        v_t.append(v.astype(F32).T.astype(BF16))
        vs.append(v)
        decay.append(jnp.exp(b_end))

    scores = [lax.dot_general(a, b, NT_DIMS, preferred_element_type=F32) for a, b in zip(q_t, k_t)]
    st_add = [jnp.dot(a, b, preferred_element_type=F32) for a, b in zip(v_t, k_st)]
    intra = [jnp.dot(jnp.where(keep, s, 0.0).astype(BF16), v, preferred_element_type=F32)
             for s, v in zip(scores, vs)]

    outs = []
    for k in range(HGRN_GROUP):
        o = lax.dot_general(q_in[k], st.astype(BF16), NT_DIMS, preferred_element_type=F32) + intra[k]
        st = st * decay[k] + st_add[k]
        outs.append((rows[k], o))
    return outs, st


def _hgrn_kernel(q_ref, v_ref, g_ref, lff_ref, lfb_ref, tril_ref, triu_ref, gh_ref, o_ref, of_scr):
    rows_per_group = C_HGRN * HGRN_GROUP
    n_groups = q_ref.shape[0] // rows_per_group
    st0 = jnp.zeros((HEAD_DIM, HEAD_DIM), F32)

    def fwd(gi, st):
        r0 = pl.multiple_of(gi * rows_per_group, rows_per_group)
        outs, st = _hgrn_group(r0, True, q_ref, v_ref, lff_ref, tril_ref, st)
        for r, o in outs:
            of_scr[r, :] = o
        return st

    lax.fori_loop(0, n_groups, fwd, st0)

    def bwd(gi, st):
        r0 = pl.multiple_of((n_groups - 1 - gi) * rows_per_group, rows_per_group)
        outs, st = _hgrn_group(r0, False, q_ref, v_ref, lfb_ref, triu_ref, st)
        for r, o in outs:
            o = of_scr[r, :] + o
            ms = jnp.mean(o * o, axis=-1, keepdims=True)
            y = o * lax.rsqrt(ms + EPS) * gh_ref[...] * g_ref[r, :].astype(F32)
            o_ref[r, :] = y.astype(BF16)
        return st

    lax.fori_loop(0, n_groups, bwd, st0)


def _hgrn(p, lf, tril, triu, g_hgrn, batch, seq):
    n = batch * seq
    c = C_HGRN
    blk = lambda off: pl.BlockSpec((seq, HEAD_DIM), lambda b, h: (b, off + h))
    return pl.pallas_call(
        _hgrn_kernel,
        grid=(batch, N_HEADS),
        in_specs=[blk(3 * N_HEADS), blk(4 * N_HEADS), blk(7 * N_HEADS), blk(0), blk(N_HEADS),
                  pl.BlockSpec((c, c), lambda b, h: (0, 0)),
                  pl.BlockSpec((c, c), lambda b, h: (0, 0)),
                  pl.BlockSpec((1, HEAD_DIM), lambda b, h: (0, 0))],
        out_specs=pl.BlockSpec((seq, HEAD_DIM), lambda b, h: (b, h)),
        out_shape=jax.ShapeDtypeStruct((n, N_HEADS * HEAD_DIM), BF16),
        scratch_shapes=[pltpu.VMEM((seq, HEAD_DIM), F32)],
        compiler_params=pltpu.CompilerParams(dimension_semantics=("arbitrary", "arbitrary"),
                                             vmem_limit_bytes=VMEM_LIMIT),
        name="hgrn",
    )(p, p, p, lf, lf, tril, triu, g_hgrn)


def _col_max(x):
    return jnp.max(x, axis=0, keepdims=True)


def _outproj_kernel(oa_ref, oh_ref, wa_ref, wb_ref, x_ref, gt_ref, gffn_ref, sc_ref, sh_ref,
                    wr_hi_ref, wr_lo_ref, rbias_ref, upper_ref, ones_ref,
                    x1_ref, hp_ref, eidx_ref, slot_ref, gate_ref, cnt_ref, cnt_scr):
    i = pl.program_id(0)
    tm = x_ref.shape[0]

    @pl.when(i == 0)
    def _():
        cnt_scr[...] = jnp.zeros(cnt_scr.shape, F32)

    acc = (jnp.dot(oa_ref[...], wa_ref[...], preferred_element_type=F32)
           + jnp.dot(oh_ref[...], wb_ref[...], preferred_element_type=F32))
    x1 = x_ref[...] + gt_ref[0] * acc
    x1_ref[...] = x1
    ms = jnp.mean(x1 * x1, axis=-1, keepdims=True)
    h2 = x1 * lax.rsqrt(ms + EPS) * gffn_ref[...] * (1.0 + sc_ref[0]) + sh_ref[0]
    half = h2.shape[1] // 2
    hp_ref[...] = _pack_pair(h2[:, :half], h2[:, half:])

    h_hi = h2.astype(BF16)
    h_lo = (h2 - h_hi.astype(F32)).astype(BF16)
    wr_hi = wr_hi_ref[...]
    logits = (lax.dot_general(wr_hi, h_hi, NT_DIMS, preferred_element_type=F32)
              + lax.dot_general(wr_hi, h_lo, NT_DIMS, preferred_element_type=F32)
              + lax.dot_general(wr_lo_ref[...], h_hi, NT_DIMS, preferred_element_type=F32))
    scores = jax.nn.sigmoid(logits)
    biased = scores + rbias_ref[...]

    gs = []
    for g in range(N_GROUPS):
        blk = biased[g * GROUP_SIZE:(g + 1) * GROUP_SIZE, :]
        top1 = _col_max(blk)
        eq = blk == top1
        n_eq = jnp.sum(eq.astype(F32), axis=0, keepdims=True)
        second = _col_max(jnp.where(eq, -jnp.inf, blk))
        gs.append(top1 + jnp.where(n_eq > 1.0, top1, second))
    gsm = jnp.concatenate(gs, axis=0)
    giota = lax.broadcasted_iota(I32, gsm.shape, 0)
    gsel = jnp.zeros(gsm.shape, F32)
    for _ in range(TOPK_GROUPS):
        top = _col_max(gsm)
        idx = jnp.min(jnp.where(gsm == top, giota, N_GROUPS), axis=0, keepdims=True)
        pick = giota == idx
        gsel = jnp.where(pick, 1.0, gsel)
        gsm = jnp.where(pick, -jnp.inf, gsm)
    emask = jnp.concatenate(
        [jnp.broadcast_to(gsel[g:g + 1, :], (GROUP_SIZE, tm)) for g in range(N_GROUPS)], axis=0)
    masked = jnp.where(emask > 0.5, biased, -jnp.inf)

    eiota = lax.broadcasted_iota(I32, masked.shape, 0)
    idxs, gates = [], []
    for _ in range(TOP_K):
        top = _col_max(masked)
        idx = jnp.min(jnp.where(masked == top, eiota, N_EXPERTS), axis=0, keepdims=True)
        pick = eiota == idx
        gates.append(jnp.sum(jnp.where(pick, scores, 0.0), axis=0, keepdims=True))
        idxs.append(idx)
        masked = jnp.where(pick, -jnp.inf, masked)
    gate = jnp.concatenate(gates, axis=0)
    gate = gate / jnp.sum(gate, axis=0, keepdims=True) * ROUTED_SCALE
    eidx = jnp.concatenate(idxs, axis=0)
    eidx_ref[...] = eidx
    gate_ref[...] = gate

    sel = jnp.zeros(masked.shape, F32)
    for k in range(TOP_K):
        sel = jnp.where(eiota == idxs[k], 1.0, sel)
    sel_bf = sel.astype(BF16)
    rank = jnp.dot(sel_bf, upper_ref[...], preferred_element_type=F32)
    base = cnt_scr[...]
    posn = base[:, :1] + rank
    slots = [jnp.sum(jnp.where(eiota == idxs[k], posn, 0.0), axis=0, keepdims=True)
             for k in range(TOP_K)]
    slot_ref[...] = jnp.concatenate(slots, axis=0).astype(I32)
    new_cnt = base + jnp.dot(sel_bf, ones_ref[...], preferred_element_type=F32)
    cnt_scr[...] = new_cnt
    cnt_ref[...] = new_cnt


def _outproj(oa, oh, wa, wb, x2d, gt1, g_ffn, sc2, sh2, wr_hi, wr_lo, rbias, upper, ones, seq):
    n, d = x2d.shape
    tm = TM_OUT
    tiles_per_batch = seq // tm
    half = d // 2
    row = lambda w: pl.BlockSpec((tm, w), lambda i: (i, 0))
    const = lambda shape: pl.BlockSpec(shape, lambda i: tuple(0 for _ in shape))
    per_batch = pl.BlockSpec((1, 1, d), lambda i: (i // tiles_per_batch, 0, 0))
    tok = pl.BlockSpec((TOP_K, tm), lambda i: (0, i))
    return pl.pallas_call(
        _outproj_kernel,
        grid=(n // tm,),
        in_specs=[row(half), row(half), const((half, d)), const((half, d)), row(d), per_batch,
                  const((1, d)), per_batch, per_batch,
                  const((N_EXPERTS, d)), const((N_EXPERTS, d)), const((N_EXPERTS, 1)),
                  const((tm, tm)), const((tm, LANES))],
        out_specs=[row(d), row(half), tok, tok, tok, const((N_EXPERTS, LANES))],
        out_shape=[jax.ShapeDtypeStruct((n, d), F32),
                   jax.ShapeDtypeStruct((n, half), U32),
                   jax.ShapeDtypeStruct((TOP_K, n), I32),
                   jax.ShapeDtypeStruct((TOP_K, n), I32),
                   jax.ShapeDtypeStruct((TOP_K, n), F32),
                   jax.ShapeDtypeStruct((N_EXPERTS, LANES), F32)],
        scratch_shapes=[pltpu.VMEM((N_EXPERTS, LANES), F32)],
        compiler_params=pltpu.CompilerParams(dimension_semantics=("arbitrary",),
                                             vmem_limit_bytes=VMEM_LIMIT),
        name="outproj",
    )(oa, oh, wa, wb, x2d, gt1, g_ffn, sc2, sh2, wr_hi, wr_lo, rbias, upper, ones)


def _dest_kernel(pstart_ref, e_ref, slot_ref, o_ref):
    e = e_ref[...]

    def body(x, acc):
        return acc + jnp.where(e == x, pstart_ref[x], 0)

    o_ref[...] = lax.fori_loop(0, N_EXPERTS, body, slot_ref[...], unroll=8)


def _dest(pstart, eidx, slot):
    k, n = eidx.shape
    tn = min(n, 2048)
    grid_spec = pltpu.PrefetchScalarGridSpec(
        num_scalar_prefetch=1,
        grid=(n // tn,),
        in_specs=[pl.BlockSpec((k, tn), lambda i, *_: (0, i)),
                  pl.BlockSpec((k, tn), lambda i, *_: (0, i))],
        out_specs=pl.BlockSpec((k, tn), lambda i, *_: (0, i)),
    )
    return pl.pallas_call(
        _dest_kernel,
        grid_spec=grid_spec,
        out_shape=jax.ShapeDtypeStruct((k, n), I32),
        compiler_params=pltpu.CompilerParams(dimension_semantics=("arbitrary",)),
        name="dest",
    )(pstart, eidx, slot)


def _scatter_kernel(pfill_ref, pend_ref, nv_ref, dest_ref, h_hbm, xs_ref, zero_scr, h_buf, sems, lsem,
                    zsem):
    i = pl.program_id(0)
    n_steps = pl.num_programs(0)
    ts = h_buf.shape[1]
    tb = zero_scr.shape[0]
    n_tail = xs_ref.shape[0] // tb - nv_ref[0]

    def pad_fill(e, wait):
        def go(src, dst):
            cp = pltpu.make_async_copy(src, dst, zsem)
            cp.wait() if wait else cp.start()

        start = pfill_ref[e]
        end = pend_ref[e]
        head = jnp.minimum((-start) & 7, end - start)
        for r in range(7):
            @pl.when(r < head)
            def _():
                go(zero_scr.at[pl.ds(0, 1)], xs_ref.at[pl.ds(start + r, 1)])
        off = start + head
        rem = end - off
        size = tb // 2
        while size >= 8:
            cond = (rem & size) != 0

            @pl.when(cond)
            def _():
                go(zero_scr.at[pl.ds(0, size)], xs_ref.at[pl.ds(pl.multiple_of(off, 8), size)])
            off = off + jnp.where(cond, size, 0)
            size //= 2

    @pl.when(i == 0)
    def _():
        zero_scr[...] = jnp.zeros(zero_scr.shape, U32)

        def fill(e, carry):
            pad_fill(e, False)
            return carry

        lax.fori_loop(0, N_EXPERTS, fill, 0)

        def fill_tail(j, carry):
            start = pl.multiple_of((nv_ref[0] + j) * tb, tb)
            pltpu.make_async_copy(zero_scr, xs_ref.at[pl.ds(start, tb)], zsem).start()
            return carry

        lax.fori_loop(0, n_tail, fill_tail, 0)

        def drain(e, carry):
            pad_fill(e, True)
            return carry

        lax.fori_loop(0, N_EXPERTS, drain, 0)

        def drain_tail(j, carry):
            pltpu.make_async_copy(zero_scr, xs_ref.at[pl.ds(0, tb)], zsem).wait()
            return carry

        lax.fori_loop(0, n_tail, drain_tail, 0)

    def load(j):
        return pltpu.make_async_copy(h_hbm.at[pl.ds(pl.multiple_of(j * ts, ts), ts)],
                                     h_buf.at[lax.rem(j, H_SLOTS)], lsem.at[lax.rem(j, H_SLOTS)])

    def drain_scatters(j):
        sl = lax.rem(j, H_SLOTS)
        for k in range(TOP_K):
            pltpu.make_async_copy(h_buf.at[sl], xs_ref.at[pl.ds(0, ts)], sems.at[sl]).wait()

    @pl.when(i == 0)
    def _():
        load(0).start()

    @pl.when(i + 1 < n_steps)
    def _():
        load(i + 1).start()

    load(i).wait()
    cur = lax.rem(i, H_SLOTS)

    def issue(t, carry):
        for k in range(TOP_K):
            pltpu.make_async_copy(h_buf.at[cur, pl.ds(t, 1)], xs_ref.at[pl.ds(dest_ref[k, t], 1)],
                                  sems.at[cur]).start(priority=k % 2)
        return carry

    lax.fori_loop(0, ts, issue, 0)

    @pl.when(i >= 1)
    def _():
        drain_scatters(i - 1)

    @pl.when(i == n_steps - 1)
    def _():
        drain_scatters(i)


def _scatter(pfill, pend, n_valid, dest, hp, n_rows):
    n, w = hp.shape
    ts = T_ROW
    grid_spec = pltpu.PrefetchScalarGridSpec(
        num_scalar_prefetch=3,
        grid=(n // ts,),
        in_specs=[pl.BlockSpec((TOP_K, ts), lambda i, *_: (0, i), memory_space=pltpu.SMEM),
                  pl.BlockSpec(memory_space=pl.ANY)],
        out_specs=pl.BlockSpec(memory_space=pl.ANY),
        scratch_shapes=[pltpu.VMEM((TB_EXP, w), U32), pltpu.VMEM((H_SLOTS, ts, w), U32),
                        pltpu.SemaphoreType.DMA((H_SLOTS,)), pltpu.SemaphoreType.DMA((H_SLOTS,)),
                        pltpu.SemaphoreType.DMA],
    )
    return pl.pallas_call(
        _scatter_kernel,
        grid_spec=grid_spec,
        out_shape=jax.ShapeDtypeStruct((n_rows, w), U32),
        compiler_params=pltpu.CompilerParams(dimension_semantics=("arbitrary",),
                                             vmem_limit_bytes=VMEM_LIMIT),
        name="scatter",
    )(pfill, pend, n_valid, dest, hp)


def _experts_kernel(be_ref, nv_ref, ge_ref, ng_ref, xs_ref, wg_hbm, wu_hbm, wd_hbm, y_ref,
                    wg_f, wu_f, wd_f, wg_bf, wu_bf, wd_bf, sems, gctr):
    i = pl.program_id(0)
    prev = jnp.maximum(i - 1, 0)
    valid = i < nv_ref[0]
    fresh = valid & ((i == 0) | (be_ref[i] != be_ref[prev]))

    def weight_copies(g, slot):
        e = ge_ref[g]
        return (pltpu.make_async_copy(wg_hbm.at[e], wg_f.at[slot], sems.at[slot, 0]),
                pltpu.make_async_copy(wu_hbm.at[e], wu_f.at[slot], sems.at[slot, 1]),
                pltpu.make_async_copy(wd_hbm.at[e], wd_f.at[slot], sems.at[slot, 2]))

    @pl.when(i == 0)
    def _():
        gctr[0] = 0
        for g in range(W_SLOTS):
            @pl.when(g < ng_ref[0])
            def _():
                for cp in weight_copies(g, g):
                    cp.start()

    @pl.when(fresh)
    def _():
        g = gctr[0]
        slot = lax.rem(g, W_SLOTS)
        for cp in weight_copies(g, slot):
            cp.wait()
        wg_bf[...] = wg_f[slot].astype(BF16)
        wu_bf[...] = wu_f[slot].astype(BF16)
        wd_bf[...] = wd_f[slot].astype(BF16)

        @pl.when(g + W_SLOTS < ng_ref[0])
        def _():
            for cp in weight_copies(g + W_SLOTS, slot):
                cp.start()

        gctr[0] = g + 1

    @pl.when(valid)
    def _():
        lo, hi = _unpack_pair(xs_ref[...])
        x = jnp.concatenate([lo.astype(BF16), hi.astype(BF16)], axis=1)
        hg = jnp.dot(x, wg_bf[...], preferred_element_type=F32)
        hu = jnp.dot(x, wu_bf[...], preferred_element_type=F32)
        a = (_silu(hg) * hu).astype(BF16)
        y = jnp.dot(a, wd_bf[...], preferred_element_type=F32)
        half = y.shape[1] // 2
        y_ref[...] = _pack_pair(y[:, :half], y[:, half:])

    @pl.when(jnp.logical_not(valid))
    def _():
        y_ref[...] = jnp.zeros(y_ref.shape, U32)


def _experts(block_e, n_valid, group_e, n_groups, xs, w_gate, w_up, w_down, n_blocks):
    tb = TB_EXP
    w = xs.shape[1]
    _, d, f = w_gate.shape
    grid_spec = pltpu.PrefetchScalarGridSpec(
        num_scalar_prefetch=4,
        grid=(n_blocks,),
        in_specs=[pl.BlockSpec((tb, w), lambda i, be, nv, ge, ng: (jnp.minimum(i, nv[0] - 1), 0)),
                  pl.BlockSpec(memory_space=pl.ANY),
                  pl.BlockSpec(memory_space=pl.ANY),
                  pl.BlockSpec(memory_space=pl.ANY)],
        out_specs=pl.BlockSpec((tb, w), lambda i, be, nv, ge, ng: (i, 0)),
        scratch_shapes=[pltpu.VMEM((W_SLOTS, d, f), F32), pltpu.VMEM((W_SLOTS, d, f), F32),
                        pltpu.VMEM((W_SLOTS, f, d), F32),
                        pltpu.VMEM((d, f), BF16), pltpu.VMEM((d, f), BF16), pltpu.VMEM((f, d), BF16),
                        pltpu.SemaphoreType.DMA((W_SLOTS, 3)), pltpu.SMEM((1,), I32)],
    )
    return pl.pallas_call(
        _experts_kernel,
        grid_spec=grid_spec,
        out_shape=jax.ShapeDtypeStruct((n_blocks * tb, w), U32),
        compiler_params=pltpu.CompilerParams(dimension_semantics=("arbitrary",),
                                             vmem_limit_bytes=VMEM_LIMIT),
        name="experts",
    )(block_e, n_valid, group_e, n_groups, xs, w_gate, w_up, w_down)


def _combine_kernel(dcur_ref, dnxt_ref, x1_ref, hp_ref, gate_ref, gt_ref,
                    wsg_ref, wsu_ref, wsd_ref, y_ref, o_ref, buf, sems):
    i = pl.program_id(0)
    tc = x1_ref.shape[0]
    slot = lax.rem(i, 2)

    def issue(d_ref, sl):
        def body(t, carry):
            for k in range(TOP_K):
                pltpu.make_async_copy(y_ref.at[pl.ds(d_ref[k, t], 1)], buf.at[sl, k, pl.ds(t, 1)],
                                      sems.at[sl]).start(priority=k % 2)
            return carry

        lax.fori_loop(0, tc, body, 0)

    @pl.when(i == 0)
    def _():
        issue(dcur_ref, 0)

    @pl.when(i + 1 < pl.num_programs(0))
    def _():
        issue(dnxt_ref, 1 - slot)

    lo, hi = _unpack_pair(hp_ref[...])
    x = jnp.concatenate([lo.astype(BF16), hi.astype(BF16)], axis=1)
    hg = jnp.dot(x, wsg_ref[...], preferred_element_type=F32)
    hu = jnp.dot(x, wsu_ref[...], preferred_element_type=F32)
    a = (_silu(hg) * hu).astype(BF16)
    shared = jnp.dot(a, wsd_ref[...], preferred_element_type=F32)

    for k in range(TOP_K):
        pltpu.make_async_copy(y_ref.at[pl.ds(0, tc)], buf.at[slot, k], sems.at[slot]).wait()

    half = shared.shape[1] // 2
    gate = gate_ref[...]
    r_lo = shared[:, :half]
    r_hi = shared[:, half:]
    for k in range(TOP_K):
        lo, hi = _unpack_pair(buf[slot, k])
        gk = gate[:, k:k + 1]
        r_lo = r_lo + gk * lo
        r_hi = r_hi + gk * hi
    gt = gt_ref[0]
    o_ref[:, :half] = x1_ref[:, :half] + gt[:, :half] * r_lo
    o_ref[:, half:] = x1_ref[:, half:] + gt[:, half:] * r_hi


def _combine(dest, x1, hp, gate_t, gt2, wsg, wsu, wsd, y, seq):
    n, d = x1.shape
    tc = T_ROW
    w = hp.shape[1]
    f = wsg.shape[1]
    tiles_per_batch = seq // tc
    last = n // tc - 1
    return pl.pallas_call(
        _combine_kernel,
        grid=(n // tc,),
        in_specs=[pl.BlockSpec((TOP_K, tc), lambda i: (0, i), memory_space=pltpu.SMEM),
                  pl.BlockSpec((TOP_K, tc), lambda i: (0, jnp.minimum(i + 1, last)),
                               memory_space=pltpu.SMEM),
                  pl.BlockSpec((tc, d), lambda i: (i, 0)),
                  pl.BlockSpec((tc, w), lambda i: (i, 0)),
                  pl.BlockSpec((tc, TOP_K), lambda i: (i, 0)),
                  pl.BlockSpec((1, 1, d), lambda i: (i // tiles_per_batch, 0, 0)),
                  pl.BlockSpec((d, f), lambda i: (0, 0)),
                  pl.BlockSpec((d, f), lambda i: (0, 0)),
                  pl.BlockSpec((f, d), lambda i: (0, 0)),
                  pl.BlockSpec(memory_space=pl.ANY)],
        out_specs=pl.BlockSpec((tc, d), lambda i: (i, 0)),
        scratch_shapes=[pltpu.VMEM((2, TOP_K, tc, w), U32), pltpu.SemaphoreType.DMA((2,))],
        out_shape=jax.ShapeDtypeStruct((n, d), F32),
        compiler_params=pltpu.CompilerParams(dimension_semantics=("arbitrary",),
                                             vmem_limit_bytes=VMEM_LIMIT),
        name="combine",
    )(dest, dest, x1, hp, gate_t, gt2, wsg, wsu, wsd, y)


def kernel(x, c, positions, rel_bias, hgrn_lb_logits, w_ada, b_ada, g_mix, w_in, g_q, g_k, lam_q1, lam_k1, lam_q2, lam_k2, g_sub, g_hgrn, w_out, g_ffn, w_router, router_bias, w_exp_gate, w_exp_up, w_exp_down, w_sh_gate, w_sh_up, w_sh_down):
    batch, seq, d = x.shape
    n = batch * seq
    layer = 0
    x2d = x.reshape(n, d)

    c_pad = jnp.zeros((8, d), F32).at[:batch].set(c.astype(F32))
    mod = _ada(c_pad, w_ada[layer], b_ada[layer][None, :])[:batch]
    sh1, sc1, gt1, sh2, sc2, gt2 = [m.reshape(batch, 1, d) for m in jnp.split(mod, 6, axis=-1)]

    lbs = jnp.cumsum(jax.nn.softmax(hgrn_lb_logits.astype(F32), axis=1), axis=1)[:, layer]
    lbs = lbs.reshape(2, 1, SEG)
    reps = SEG // QK_DIM
    qk_gain = jnp.stack([jnp.tile(g_q[layer].astype(F32), reps) * (QK_DIM ** -0.5 * LOG2E),
                         jnp.tile(g_k[layer].astype(F32), reps)]).reshape(2, 1, SEG)
    lane = jnp.arange(LANES)
    g64 = jnp.where((lane[:, None] // QK_DIM) == (lane[None, :] // QK_DIM), 1.0 / QK_DIM, 0.0).astype(BF16)
    lam = (jnp.exp(jnp.sum(lam_q1[layer].astype(F32) * lam_k1[layer].astype(F32)))
           - jnp.exp(jnp.sum(lam_q2[layer].astype(F32) * lam_k2[layer].astype(F32)))
           + LAM_INIT).reshape(1)

    p, lf = _inproj(x2d, sc1, sh1, g_mix[layer][None, :], w_in[layer].astype(BF16), qk_gain, lbs, g64, seq)

    nt = seq // T_ATT
    pos_sub = positions.astype(I32).reshape(batch * seq // T_SUB, T_SUB)
    smin = jnp.min(pos_sub, axis=1)
    smax = jnp.max(pos_sub, axis=1)
    posq = positions.astype(I32).reshape(batch, nt, 1, T_ATT)
    posk = positions.astype(I32).reshape(batch, seq, 1)
    rb_t = rel_bias.astype(F32).T * LOG2E
    rb_tab = jnp.zeros((N_HEADS, LANES), F32).at[:, :REL_BUCKETS].set(rb_t)
    oa = _attention(p, smin, smax, posq, posk, rb_tab, rb_t, lam,
                    g_sub[layer][:, None].astype(F32), batch, seq)

    ci = jnp.arange(C_HGRN)
    tril = (ci[None, :] <= ci[:, None]).astype(BF16)
    triu = (ci[None, :] >= ci[:, None]).astype(BF16)
    oh = _hgrn(p, lf, tril, triu, g_hgrn[layer][None, :].astype(F32), batch, seq)

    half = d // 2
    w_out_bf = w_out[layer].astype(BF16)
    wr_t = w_router[layer].astype(F32).T
    wr_hi = wr_t.astype(BF16)
    wr_lo = (wr_t - wr_hi.astype(F32)).astype(BF16)
    ti = jnp.arange(TM_OUT)
    upper = (ti[:, None] < ti[None, :]).astype(BF16)
    ones = jnp.ones((TM_OUT, LANES), BF16)
    x1, hp, eidx, slot, gate, cnt = _outproj(
        oa, oh, w_out_bf[:half], w_out_bf[half:], x2d, gt1, g_ffn[layer][None, :], sc2, sh2,
        wr_hi, wr_lo, router_bias[layer].astype(F32)[:, None], upper, ones, seq)

    tb = TB_EXP
    counts = cnt[:, 0].astype(I32)
    padded = (counts + tb - 1) // tb * tb
    pends = jnp.cumsum(padded)
    pstart = (pends - padded).astype(I32)
    n_blocks = (n * TOP_K) // tb + N_EXPERTS
    n_valid = (pends[-1] // tb).astype(I32).reshape(1)
    blk_start = jnp.arange(n_blocks, dtype=I32) * tb
    block_e = jnp.minimum(jnp.sum(pends[None, :] <= blk_start[:, None], axis=1), N_EXPERTS - 1).astype(I32)
    pfill = (pstart + counts).astype(I32)
    pend = pends.astype(I32)
    dest = _dest(pstart, eidx, slot)
    has_rows = counts > 0
    group_e = jnp.nonzero(has_rows, size=N_EXPERTS, fill_value=0)[0].astype(I32)
    n_groups = jnp.sum(has_rows).astype(I32).reshape(1)

    xs = _scatter(pfill, pend, n_valid, dest, hp, n_blocks * tb)
    y = _experts(block_e, n_valid, group_e, n_groups, xs,
                 w_exp_gate[layer], w_exp_up[layer], w_exp_down[layer], n_blocks)
    out = _combine(dest, x1, hp, gate.T, gt2,
                   w_sh_gate[layer].astype(BF16), w_sh_up[layer].astype(BF16),
                   w_sh_down[layer].astype(BF16), y, seq)
    return out.reshape(batch, seq, d)
```

```python
import functools
import math

import jax
import jax.numpy as jnp
from jax import lax
from jax.experimental import pallas as pl
from jax.experimental.pallas import tpu as pltpu

F32 = jnp.float32
BF16 = jnp.bfloat16
I32 = jnp.int32
U32 = jnp.uint32

D_MODEL = 2048
N_HEADS = 8
QK_DIM = 64
HEAD_DIM = 128
SEG = 1024
N_SEG = 8
REL_BUCKETS = 32
REL_MAX_DIST = 128
N_EXPERTS = 256
TOP_K = 8
N_GROUPS = 8
TOPK_GROUPS = 4
GROUP_SIZE = N_EXPERTS // N_GROUPS
EXPERT_DIM = 512
ROUTED_SCALE = 2.5
EPS = 1e-6
LAM_INIT = 0.8 - 0.6 * math.exp(-0.3 * 0)
LOG2E = math.log2(math.e)

LANES = 128
VMEM_LIMIT = 56 * 1024 * 1024

TM_IN = 512
IN_CHUNK = 256
T_ATT = 512
T_KEY = 1024
T_SUB = 128
V_PAD = 16
C_HGRN = 64
HGRN_GROUP = 8
TM_OUT = 256
T_ROW = 256
TB_EXP = 256
W_SLOTS = 2
H_SLOTS = 3
NEG_BIG = -1e30
EXP_CLAMP = 80.0
FAST_BOUND = 60.0
BOUND_SLACK = 1.02
BOUND_PAD = 0.01

NT_DIMS = (((1,), (1,)), ((), ()))


def _silu(x):
    return x * jax.nn.sigmoid(x)


def _pack_pair(lo_f32, hi_f32):
    lo = lax.bitcast_convert_type(lo_f32.astype(BF16).astype(F32), U32)
    hi = lax.bitcast_convert_type(hi_f32.astype(BF16).astype(F32), U32)
    return (hi & jnp.uint32(0xFFFF0000)) | (lo >> 16)


def _unpack_pair(word):
    lo = lax.bitcast_convert_type(word << 16, F32)
    hi = lax.bitcast_convert_type(word & jnp.uint32(0xFFFF0000), F32)
    return lo, hi


def _ada_kernel(c_ref, w_ref, b_ref, o_ref):
    a = _silu(c_ref[...]).astype(BF16)
    o_ref[...] = jnp.dot(a, w_ref[...].astype(BF16), preferred_element_type=F32) + b_ref[...]


def _ada(c_pad, w, b):
    d, n = w.shape
    tn = 1024
    return pl.pallas_call(
        _ada_kernel,
        grid=(n // tn,),
        in_specs=[pl.BlockSpec((8, d), lambda j: (0, 0)),
                  pl.BlockSpec((d, tn), lambda j: (0, j)),
                  pl.BlockSpec((1, tn), lambda j: (0, j))],
        out_specs=pl.BlockSpec((8, tn), lambda j: (0, j)),
        out_shape=jax.ShapeDtypeStruct((8, n), F32),
        compiler_params=pltpu.CompilerParams(dimension_semantics=("arbitrary",),
                                             vmem_limit_bytes=VMEM_LIMIT),
        name="ada",
    )(c_pad, w, b)


def _inproj_kernel(x_ref, sc_ref, sh_ref, gmix_ref, w_ref, qkg_ref, lb_ref, g64_ref,
                   p_ref, lf_ref, h_scr):
    j = pl.program_id(1)

    @pl.when(j == 0)
    def _():
        x = x_ref[...]
        ms = jnp.mean(x * x, axis=-1, keepdims=True)
        y = x * lax.rsqrt(ms + EPS) * gmix_ref[...]
        h_scr[...] = (y * (1.0 + sc_ref[0]) + sh_ref[0]).astype(BF16)

    def chunks(epilogue):
        for c in range(SEG // IN_CHUNK):
            sl = slice(c * IN_CHUNK, (c + 1) * IN_CHUNK)
            epilogue(sl, jnp.dot(h_scr[...], w_ref[:, sl], preferred_element_type=F32))

    @pl.when(j < 2)
    def _():
        acc = jnp.dot(h_scr[...], w_ref[...], preferred_element_type=F32)
        gain = qkg_ref[0]
        for c in range(SEG // LANES):
            sl = slice(c * LANES, (c + 1) * LANES)
            xs = acc[:, sl]
            ms = jnp.dot((xs * xs).astype(BF16), g64_ref[...], preferred_element_type=F32)
            p_ref[:, sl] = (xs * lax.rsqrt(ms + EPS) * gain[:, sl]).astype(BF16)

    @pl.when((j == 2) | (j == 4))
    def _():
        def plain(sl, acc):
            p_ref[:, sl] = acc.astype(BF16)

        chunks(plain)

    @pl.when((j == 3) | (j == 7))
    def _():
        def silu(sl, acc):
            p_ref[:, sl] = _silu(acc).astype(BF16)

        chunks(silu)

    @pl.when((j == 5) | (j == 6))
    def _():
        def log_gate(sl, z):
            lb = lb_ref[0, :, sl]
            f = lb + (1.0 - lb) * jax.nn.sigmoid(z)
            lf_ref[:, sl] = jnp.log(f)
            p_ref[:, sl] = z.astype(BF16)

        chunks(log_gate)


def _inproj(x2d, sc1, sh1, g_mix, w_in_bf, qk_gain, lbs, g64, seq):
    n, d = x2d.shape
    tm = TM_IN
    tiles_per_batch = seq // tm
    return pl.pallas_call(
        _inproj_kernel,
        grid=(n // tm, N_SEG),
        in_specs=[
            pl.BlockSpec((tm, d), lambda i, j: (i, 0)),
            pl.BlockSpec((1, 1, d), lambda i, j: (i // tiles_per_batch, 0, 0)),
            pl.BlockSpec((1, 1, d), lambda i, j: (i // tiles_per_batch, 0, 0)),
            pl.BlockSpec((1, d), lambda i, j: (0, 0)),
            pl.BlockSpec((d, SEG), lambda i, j: (0, j)),
            pl.BlockSpec((1, 1, SEG), lambda i, j: (jnp.minimum(j, 1), 0, 0)),
            pl.BlockSpec((1, 1, SEG), lambda i, j: (jnp.clip(j - 5, 0, 1), 0, 0)),
            pl.BlockSpec((LANES, LANES), lambda i, j: (0, 0)),
        ],
        out_specs=[
            pl.BlockSpec((tm, SEG), lambda i, j: (i, j)),
            pl.BlockSpec((tm, SEG), lambda i, j: (i, jnp.clip(j - 5, 0, 1))),
        ],
        out_shape=[jax.ShapeDtypeStruct((n, N_SEG * SEG), BF16),
                   jax.ShapeDtypeStruct((n, 2 * SEG), F32)],
        scratch_shapes=[pltpu.VMEM((tm, d), BF16)],
        compiler_params=pltpu.CompilerParams(dimension_semantics=("arbitrary", "arbitrary"),
                                             vmem_limit_bytes=VMEM_LIMIT),
        name="inproj",
    )(x2d, sc1, sh1, g_mix, w_in_bf, qk_gain, lbs, g64)


def _t5_bias_tile(pos_q, pos_k, table):
    half = REL_BUCKETS // 2
    max_exact = half // 2
    rel = pos_k - pos_q
    n = jnp.abs(rel)
    nf = jnp.maximum(n, 1).astype(F32)
    large = max_exact + (jnp.log(nf / max_exact) / math.log(REL_MAX_DIST / max_exact)
                         * (half - max_exact)).astype(I32)
    large = jnp.minimum(large, half - 1)
    bucket = jnp.where(rel > 0, half, 0) + jnp.where(n < max_exact, n, large)
    rows = bucket.shape[0]
    tbl = jnp.broadcast_to(table, (rows, LANES))
    cols = [jnp.take_along_axis(tbl, bucket[:, c * LANES:(c + 1) * LANES], axis=1)
            for c in range(bucket.shape[1] // LANES)]
    return jnp.concatenate(cols, axis=1)


def _attn_kernel(smin_ref, smax_ref, q_ref, k_ref, v_ref, posq_ref, posk_ref, rbt_ref, rb_ref, lam_ref,
                 gsub_ref, o_ref, vt_scr, s_a, s_b, cm_a, cm_b, p_a, p_b, al_a, al_b, m_scr, a_scr,
                 kmax_scr):
    b = pl.program_id(0)
    h = pl.program_id(1)
    i = pl.program_id(2)
    tq = T_ATT
    tk = T_KEY
    n_sub = tk // T_SUB
    nq_sub = tq // T_SUB
    ntk = k_ref.shape[0] // tk
    n_pairs = ntk // 2
    subs_per_batch = k_ref.shape[0] // T_SUB

    @pl.when(i == 0)
    def _():
        ones_row = jnp.where(lax.broadcasted_iota(I32, (V_PAD, tk), 0) == 0, 1.0, 0.0).astype(BF16)
        kmax_scr[...] = jnp.zeros(kmax_scr.shape, F32)

        def tr(c, carry):
            r0 = pl.multiple_of(c * tk, tk)
            vt_scr[c, :HEAD_DIM, :] = v_ref[pl.ds(r0, tk), :].astype(F32).T.astype(BF16)
            vt_scr[c, HEAD_DIM:, :] = ones_row
            kf = k_ref[pl.ds(r0, tk), :].astype(F32)
            for mp in range(2):
                km = kf[:, mp * QK_DIM:(mp + 1) * QK_DIM]
                nk = jnp.dot(km * km, jnp.ones((QK_DIM, LANES), F32), preferred_element_type=F32)
                kmax_scr[mp] = jnp.maximum(kmax_scr[mp], jnp.max(nk, axis=0, keepdims=True))
            return carry

        lax.fori_loop(0, ntk, tr, 0)

    q = q_ref[...]
    qs = (q[:, :QK_DIM], q[:, QK_DIM:])
    m_scr[...] = jnp.full(m_scr.shape, NEG_BIG, F32)
    a_scr[...] = jnp.zeros(a_scr.shape, F32)

    sub0 = b * subs_per_batch + i * nq_sub
    q_lo = smin_ref[sub0]
    q_hi = smax_ref[sub0]
    for u in range(1, nq_sub):
        q_lo = jnp.minimum(q_lo, smin_ref[sub0 + u])
        q_hi = jnp.maximum(q_hi, smax_ref[sub0 + u])
    c_pos = rb_ref[h, REL_BUCKETS - 1]
    c_neg = rb_ref[h, REL_BUCKETS // 2 - 1]
    pos_q = posq_ref[0, 0]

    def classify(j, u):
        ksub = b * subs_per_batch + j * n_sub + u
        lo = smin_ref[ksub] - q_hi
        hi = smax_ref[ksub] - q_lo
        far = (lo >= REL_MAX_DIST) | (hi <= -REL_MAX_DIST)
        shift = jnp.where(lo >= REL_MAX_DIST, c_pos, jnp.where(hi <= -REL_MAX_DIST, c_neg, 0.0))
        return far, shift

    def scores(j, s_ref, cm_ref):
        kk = k_ref[pl.ds(pl.multiple_of(j * tk, tk), tk), :]
        ks = (kk[:, :QK_DIM], kk[:, QK_DIM:])
        for mp in range(2):
            s = lax.dot_general(ks[mp], qs[mp], NT_DIMS, preferred_element_type=F32)
            s_ref[mp] = s
            for u in range(n_sub):
                _, shift = classify(j, u)
                cm_ref[mp, u] = jnp.max(s[u * T_SUB:(u + 1) * T_SUB], axis=0, keepdims=True) + shift

    def fixup(j, s_ref, cm_ref):
        for u in range(n_sub):
            far, _ = classify(j, u)

            @pl.when(jnp.logical_not(far))
            def _():
                rows = pl.ds(u * T_SUB, T_SUB)
                pos_k = posk_ref[0, pl.ds(pl.multiple_of(j * tk + u * T_SUB, T_SUB), T_SUB), :]
                bias = _t5_bias_tile(pos_q, pos_k, rbt_ref[pl.ds(h, 1), :])
                for mp in range(2):
                    sb = s_ref[mp, rows, :] + bias
                    s_ref[mp, rows, :] = sb
                    cm_ref[mp, u] = jnp.max(sb, axis=0, keepdims=True)

    def soft(j, s_ref, cm_ref, p_ref, al_ref):
        for mp in range(2):
            m_old = m_scr[mp]
            m_new = m_old
            for u in range(n_sub):
                m_new = jnp.maximum(m_new, cm_ref[mp, u])
            for u in range(n_sub):
                rows = pl.ds(u * T_SUB, T_SUB)
                _, shift = classify(j, u)
                p_ref[mp, rows, :] = jnp.exp2(s_ref[mp, rows, :] - (m_new - shift)).astype(BF16)
            m_scr[mp] = m_new
            al_ref[mp] = jnp.exp2(m_old - m_new)

    def pv(j, p_ref, al_ref):
        vt = vt_scr[j]
        for mp in range(2):
            a_scr[mp] = al_ref[mp] * a_scr[mp] + jnp.dot(vt, p_ref[mp], preferred_element_type=F32)

    sbuf = ((s_a, cm_a), (s_b, cm_b))
    pbuf = ((p_a, al_a), (p_b, al_b))
    last = 2 * (n_pairs - 1)

    def online_path():
        def step(j, par):
            scores(j + 2, *sbuf[par])
            soft(j + 1, *sbuf[1 - par], *pbuf[1 - par])
            pv(j, *pbuf[par])
            fixup(j + 2, *sbuf[par])

        scores(0, *sbuf[0])
        fixup(0, *sbuf[0])
        scores(1, *sbuf[1])
        soft(0, *sbuf[0], *pbuf[0])
        fixup(1, *sbuf[1])

        def body(jj, carry):
            step(2 * jj, 0)
            step(2 * jj + 1, 1)
            return carry

        lax.fori_loop(0, n_pairs - 1, body, 0)
        soft(last + 1, *sbuf[1], *pbuf[1])
        pv(last, *pbuf[0])
        pv(last + 1, *pbuf[1])

    def col_bound(mp):
        qf = qs[mp].astype(F32)
        nq = lax.dot_general(jnp.ones((8, QK_DIM), F32), qf * qf, NT_DIMS,
                             preferred_element_type=F32)[0:1, :]
        return jnp.sqrt(nq * kmax_scr[mp, :, 0:1]) * BOUND_SLACK

    b_max = rb_ref[h, 0]
    for e in range(1, REL_BUCKETS):
        b_max = jnp.maximum(b_max, rb_ref[h, e])
    bounds = [col_bound(mp) + (b_max + BOUND_PAD) for mp in range(2)]
    bound_max = jnp.max(jnp.maximum(bounds[0], bounds[1]))

    def fast_scores(j, p_ref):
        kk = k_ref[pl.ds(pl.multiple_of(j * tk, tk), tk), :]
        ks = (kk[:, :QK_DIM], kk[:, QK_DIM:])
        for mp in range(2):
            s = lax.dot_general(ks[mp], qs[mp], NT_DIMS, preferred_element_type=F32)
            for u in range(n_sub):
                _, shift = classify(j, u)
                p_ref[mp, pl.ds(u * T_SUB, T_SUB), :] = jnp.exp2(
                    s[u * T_SUB:(u + 1) * T_SUB] - (bounds[mp] - shift)).astype(BF16)

    def fast_fixup(j, p_ref):
        fars = [classify(j, u)[0] for u in range(n_sub)]
        all_far = fars[0]
        for f in fars[1:]:
            all_far = all_far & f

        @pl.when(jnp.logical_not(all_far))
        def _():
            for u in range(n_sub):
                @pl.when(jnp.logical_not(fars[u]))
                def _():
                    rows = pl.ds(u * T_SUB, T_SUB)
                    pos_k = posk_ref[0, pl.ds(pl.multiple_of(j * tk + u * T_SUB, T_SUB), T_SUB), :]
                    scale = jnp.exp2(_t5_bias_tile(pos_q, pos_k, rbt_ref[pl.ds(h, 1), :]))
                    for mp in range(2):
                        p_ref[mp, rows, :] = (p_ref[mp, rows, :].astype(F32) * scale).astype(BF16)

    def fast_pv(j, p_ref):
        vt = vt_scr[j]
        for mp in range(2):
            a_scr[mp] = a_scr[mp] + jnp.dot(vt, p_ref[mp], preferred_element_type=F32)

    def fast_path():
        pb = (p_a, p_b)

        def step(j, par):
            fast_scores(j + 1, pb[1 - par])
            fast_pv(j, pb[par])
            fast_fixup(j + 1, pb[1 - par])

        fast_scores(0, pb[0])
        fast_fixup(0, pb[0])

        def body(jj, carry):
            step(2 * jj, 0)
            step(2 * jj + 1, 1)
            return carry

        lax.fori_loop(0, n_pairs - 1, body, 0)
        step(last, 0)
        fast_pv(last + 1, pb[1])

    use_fast = bound_max <= FAST_BOUND

    @pl.when(use_fast)
    def _():
        fast_path()

    @pl.when(jnp.logical_not(use_fast))
    def _():
        online_path()

    num = [a_scr[mp, :HEAD_DIM, :] / a_scr[mp, HEAD_DIM:HEAD_DIM + 1, :] for mp in range(2)]
    o = num[0] - lam_ref[0] * num[1]
    ms = jnp.mean(o * o, axis=0, keepdims=True)
    o = o * lax.rsqrt(ms + EPS) * (gsub_ref[...] * (1.0 - LAM_INIT))
    o_ref[...] = o.T.astype(BF16)


def _attention(p, smin, smax, posq, posk, rb_tab, rb_t, lam, g_sub, batch, seq):
    t = T_ATT
    tk = T_KEY
    nt = seq // t
    ntk = seq // tk
    assert seq % (2 * tk) == 0
    n = batch * seq
    n_sub = tk // T_SUB
    va = HEAD_DIM + V_PAD
    grid_spec = pltpu.PrefetchScalarGridSpec(
        num_scalar_prefetch=2,
        grid=(batch, N_HEADS, nt),
        in_specs=[
            pl.BlockSpec((t, HEAD_DIM), lambda b, h, i, *_: (b * nt + i, h)),
            pl.BlockSpec((seq, HEAD_DIM), lambda b, h, i, *_: (b, N_HEADS + h)),
            pl.BlockSpec((seq, HEAD_DIM), lambda b, h, i, *_: (b, 2 * N_HEADS + h)),
            pl.BlockSpec((1, 1, 1, t), lambda b, h, i, *_: (b, i, 0, 0)),
            pl.BlockSpec((1, seq, 1), lambda b, h, i, *_: (b, 0, 0)),
            pl.BlockSpec((N_HEADS, LANES), lambda b, h, i, *_: (0, 0)),
            pl.BlockSpec(memory_space=pltpu.SMEM),
            pl.BlockSpec(memory_space=pltpu.SMEM),
            pl.BlockSpec((HEAD_DIM, 1), lambda b, h, i, *_: (0, 0)),
        ],
        out_specs=pl.BlockSpec((t, HEAD_DIM), lambda b, h, i, *_: (b * nt + i, h)),
        scratch_shapes=[pltpu.VMEM((ntk, va, tk), BF16),
                        pltpu.VMEM((2, tk, t), F32),
                        pltpu.VMEM((2, tk, t), F32),
                        pltpu.VMEM((2, n_sub, 1, t), F32),
                        pltpu.VMEM((2, n_sub, 1, t), F32),
                        pltpu.VMEM((2, tk, t), BF16),
                        pltpu.VMEM((2, tk, t), BF16),
                        pltpu.VMEM((2, 1, t), F32),
                        pltpu.VMEM((2, 1, t), F32),
                        pltpu.VMEM((2, 1, t), F32),
                        pltpu.VMEM((2, va, t), F32),
                        pltpu.VMEM((2, 1, LANES), F32)],
    )
    return pl.pallas_call(
        _attn_kernel,
        grid_spec=grid_spec,
        out_shape=jax.ShapeDtypeStruct((n, N_HEADS * HEAD_DIM), BF16),
        compiler_params=pltpu.CompilerParams(
            dimension_semantics=("arbitrary", "arbitrary", "arbitrary"),
            vmem_limit_bytes=VMEM_LIMIT),
        name="attn",
    )(smin, smax, p, p, p, posq, posk, rb_tab, rb_t, lam, g_sub)


def _hgrn_group(r0, forward, q_ref, v_ref, lf_ref, tri_ref, st):
    c = C_HGRN
    order = range(HGRN_GROUP) if forward else range(HGRN_GROUP - 1, -1, -1)
    rows = [pl.ds(r0 + k * c, c) for k in order]
    tri = tri_ref[...]
    row = lax.broadcasted_iota(I32, (c, c), 0)
    col = lax.broadcasted_iota(I32, (c, c), 1)
    keep = (col <= row) if forward else (col >= row)

    gs = [lf_ref[r, :] for r in rows]
    bsums = []
    for g in gs:
        g_hi = g.astype(BF16)
        g_lo = (g - g_hi.astype(F32)).astype(BF16)
        bsums.append(jnp.dot(tri, g_hi, preferred_element_type=F32)
                     + jnp.dot(tri, g_lo, preferred_element_type=F32))

    q_in, q_t, k_t, k_st, v_t, vs, decay = [], [], [], [], [], [], []
    for r, g, bsum in zip(rows, gs, bsums):
        if forward:
            ref = bsum[c // 2 - 1:c // 2, :]
            b_end = bsum[c - 1:c, :]
        else:
            ref = bsum[c // 2:c // 2 + 1, :]
            b_end = bsum[0:1, :]
        q = q_ref[r, :].astype(F32)
        v = v_ref[r, :]
        kf = 1.0 - jnp.exp(g)
        q_in.append((q * jnp.exp(bsum)).astype(BF16))
        q_t.append((q * jnp.exp(jnp.minimum(bsum - ref, EXP_CLAMP))).astype(BF16))
        k_t.append((kf * jnp.exp(jnp.minimum(ref - bsum, EXP_CLAMP))).astype(BF16))
        k_st.append((kf * jnp.exp(b_end - bsum)).astype(BF16))
        v_t.append(v.astype(F32).T.astype(BF16))
        vs.append(v)
        decay.append(jnp.exp(b_end))

    scores = [lax.dot_general(a, b, NT_DIMS, preferred_element_type=F32) for a, b in zip(q_t, k_t)]
    st_add = [jnp.dot(a, b, preferred_element_type=F32) for a, b in zip(v_t, k_st)]
    intra = [jnp.dot(jnp.where(keep, s, 0.0).astype(BF16), v, preferred_element_type=F32)
             for s, v in zip(scores, vs)]

    outs = []
    for k in range(HGRN_GROUP):
        o = lax.dot_general(q_in[k], st.astype(BF16), NT_DIMS, preferred_element_type=F32) + intra[k]
        st = st * decay[k] + st_add[k]
        outs.append((rows[k], o))
    return outs, st


def _hgrn_kernel(q_ref, v_ref, g_ref, lff_ref, lfb_ref, tril_ref, triu_ref, gh_ref, o_ref, of_scr):
    rows_per_group = C_HGRN * HGRN_GROUP
    n_groups = q_ref.shape[0] // rows_per_group
    st0 = jnp.zeros((HEAD_DIM, HEAD_DIM), F32)

    def fwd(gi, st):
        r0 = pl.multiple_of(gi * rows_per_group, rows_per_group)
        outs, st = _hgrn_group(r0, True, q_ref, v_ref, lff_ref, tril_ref, st)
        for r, o in outs:
            of_scr[r, :] = o
        return st

    lax.fori_loop(0, n_groups, fwd, st0)

    def bwd(gi, st):
        r0 = pl.multiple_of((n_groups - 1 - gi) * rows_per_group, rows_per_group)
        outs, st = _hgrn_group(r0, False, q_ref, v_ref, lfb_ref, triu_ref, st)
        for r, o in outs:
            o = of_scr[r, :] + o
            ms = jnp.mean(o * o, axis=-1, keepdims=True)
            y = o * lax.rsqrt(ms + EPS) * gh_ref[...] * g_ref[r, :].astype(F32)
            o_ref[r, :] = y.astype(BF16)
        return st

    lax.fori_loop(0, n_groups, bwd, st0)


def _hgrn(p, lf, tril, triu, g_hgrn, batch, seq):
    n = batch * seq
    c = C_HGRN
    blk = lambda off: pl.BlockSpec((seq, HEAD_DIM), lambda b, h: (b, off + h))
    return pl.pallas_call(
        _hgrn_kernel,
        grid=(batch, N_HEADS),
        in_specs=[blk(3 * N_HEADS), blk(4 * N_HEADS), blk(7 * N_HEADS), blk(0), blk(N_HEADS),
                  pl.BlockSpec((c, c), lambda b, h: (0, 0)),
                  pl.BlockSpec((c, c), lambda b, h: (0, 0)),
                  pl.BlockSpec((1, HEAD_DIM), lambda b, h: (0, 0))],
        out_specs=pl.BlockSpec((seq, HEAD_DIM), lambda b, h: (b, h)),
        out_shape=jax.ShapeDtypeStruct((n, N_HEADS * HEAD_DIM), BF16),
        scratch_shapes=[pltpu.VMEM((seq, HEAD_DIM), F32)],
        compiler_params=pltpu.CompilerParams(dimension_semantics=("arbitrary", "arbitrary"),
                                             vmem_limit_bytes=VMEM_LIMIT),
        name="hgrn",
    )(p, p, p, lf, lf, tril, triu, g_hgrn)


def _col_max(x):
    return jnp.max(x, axis=0, keepdims=True)


def _outproj_kernel(oa_ref, oh_ref, w_ref, x_ref, gt_ref, gffn_ref, sc_ref, sh_ref,
                    wr_hi_ref, wr_lo_ref, rbias_ref, upper_ref, ones_ref,
                    x1_ref, hp_ref, eidx_ref, slot_ref, gate_ref, cnt_ref, cnt_scr):
    i = pl.program_id(0)
    tm = x_ref.shape[0]

    @pl.when(i == 0)
    def _():
        cnt_scr[...] = jnp.zeros(cnt_scr.shape, F32)

    mixed = jnp.concatenate([oa_ref[...], oh_ref[...]], axis=1)
    acc = jnp.dot(mixed, w_ref[...], preferred_element_type=F32)
    x1 = x_ref[...] + gt_ref[0] * acc
    x1_ref[...] = x1
    ms = jnp.mean(x1 * x1, axis=-1, keepdims=True)
    h2 = x1 * lax.rsqrt(ms + EPS) * gffn_ref[...] * (1.0 + sc_ref[0]) + sh_ref[0]
    half = h2.shape[1] // 2
    hp_ref[...] = _pack_pair(h2[:, :half], h2[:, half:])

    h_hi = h2.astype(BF16)
    h_lo = (h2 - h_hi.astype(F32)).astype(BF16)
    wr_hi = wr_hi_ref[...]
    logits = (lax.dot_general(wr_hi, h_hi, NT_DIMS, preferred_element_type=F32)
              + lax.dot_general(wr_hi, h_lo, NT_DIMS, preferred_element_type=F32)
              + lax.dot_general(wr_lo_ref[...], h_hi, NT_DIMS, preferred_element_type=F32))
    scores = jax.nn.sigmoid(logits)
    biased = scores + rbias_ref[...]

    gs = []
    for g in range(N_GROUPS):
        blk = biased[g * GROUP_SIZE:(g + 1) * GROUP_SIZE, :]
        top1 = _col_max(blk)
        eq = blk == top1
        n_eq = jnp.sum(eq.astype(F32), axis=0, keepdims=True)
        second = _col_max(jnp.where(eq, -jnp.inf, blk))
        gs.append(top1 + jnp.where(n_eq > 1.0, top1, second))
    gsm = jnp.concatenate(gs, axis=0)
    giota = lax.broadcasted_iota(I32, gsm.shape, 0)
    gsel = jnp.zeros(gsm.shape, F32)
    for _ in range(TOPK_GROUPS):
        top = _col_max(gsm)
        idx = jnp.min(jnp.where(gsm == top, giota, N_GROUPS), axis=0, keepdims=True)
        pick = giota == idx
        gsel = jnp.where(pick, 1.0, gsel)
        gsm = jnp.where(pick, -jnp.inf, gsm)
    emask = jnp.concatenate(
        [jnp.broadcast_to(gsel[g:g + 1, :], (GROUP_SIZE, tm)) for g in range(N_GROUPS)], axis=0)
    masked = jnp.where(emask > 0.5, biased, -jnp.inf)

    eiota = lax.broadcasted_iota(I32, masked.shape, 0)
    idxs, gates = [], []
    for _ in range(TOP_K):
        top = _col_max(masked)
        idx = jnp.min(jnp.where(masked == top, eiota, N_EXPERTS), axis=0, keepdims=True)
        pick = eiota == idx
        gates.append(jnp.sum(jnp.where(pick, scores, 0.0), axis=0, keepdims=True))
        idxs.append(idx)
        masked = jnp.where(pick, -jnp.inf, masked)
    gate = jnp.concatenate(gates, axis=0)
    gate = gate / jnp.sum(gate, axis=0, keepdims=True) * ROUTED_SCALE
    eidx = jnp.concatenate(idxs, axis=0)
    eidx_ref[...] = eidx
    gate_ref[...] = gate

    sel = jnp.zeros(masked.shape, F32)
    for k in range(TOP_K):
        sel = jnp.where(eiota == idxs[k], 1.0, sel)
    sel_bf = sel.astype(BF16)
    rank = jnp.dot(sel_bf, upper_ref[...], preferred_element_type=F32)
    base = cnt_scr[...]
    posn = base[:, :1] + rank
    slots = [jnp.sum(jnp.where(eiota == idxs[k], posn, 0.0), axis=0, keepdims=True)
             for k in range(TOP_K)]
    slot_ref[...] = jnp.concatenate(slots, axis=0).astype(I32)
    new_cnt = base + jnp.dot(sel_bf, ones_ref[...], preferred_element_type=F32)
    cnt_scr[...] = new_cnt
    cnt_ref[...] = new_cnt


def _outproj(oa, oh, w_out, x2d, gt1, g_ffn, sc2, sh2, wr_hi, wr_lo, rbias, upper, ones, seq):
    n, d = x2d.shape
    tm = TM_OUT
    tiles_per_batch = seq // tm
    half = d // 2
    row = lambda w: pl.BlockSpec((tm, w), lambda i: (i, 0))
    const = lambda shape: pl.BlockSpec(shape, lambda i: tuple(0 for _ in shape))
    per_batch = pl.BlockSpec((1, 1, d), lambda i: (i // tiles_per_batch, 0, 0))
    tok = pl.BlockSpec((TOP_K, tm), lambda i: (0, i))
    return pl.pallas_call(
        _outproj_kernel,
        grid=(n // tm,),
        in_specs=[row(half), row(half), const((d, d)), row(d), per_batch,
                  const((1, d)), per_batch, per_batch,
                  const((N_EXPERTS, d)), const((N_EXPERTS, d)), const((N_EXPERTS, 1)),
                  const((tm, tm)), const((tm, LANES))],
        out_specs=[row(d), row(half), tok, tok, tok, const((N_EXPERTS, LANES))],
        out_shape=[jax.ShapeDtypeStruct((n, d), F32),
                   jax.ShapeDtypeStruct((n, half), U32),
                   jax.ShapeDtypeStruct((TOP_K, n), I32),
                   jax.ShapeDtypeStruct((TOP_K, n), I32),
                   jax.ShapeDtypeStruct((TOP_K, n), F32),
                   jax.ShapeDtypeStruct((N_EXPERTS, LANES), F32)],
        scratch_shapes=[pltpu.VMEM((N_EXPERTS, LANES), F32)],
        compiler_params=pltpu.CompilerParams(dimension_semantics=("arbitrary",),
                                             vmem_limit_bytes=VMEM_LIMIT),
        name="outproj",
    )(oa, oh, w_out, x2d, gt1, g_ffn, sc2, sh2, wr_hi, wr_lo, rbias, upper, ones)


def _dest_kernel(pstart_ref, e_ref, slot_ref, o_ref):
    e = e_ref[...]

    def body(x, acc):
        return acc + jnp.where(e == x, pstart_ref[x], 0)

    o_ref[...] = lax.fori_loop(0, N_EXPERTS, body, slot_ref[...], unroll=8)


def _dest(pstart, eidx, slot):
    k, n = eidx.shape
    tn = min(n, 2048)
    grid_spec = pltpu.PrefetchScalarGridSpec(
        num_scalar_prefetch=1,
        grid=(n // tn,),
        in_specs=[pl.BlockSpec((k, tn), lambda i, *_: (0, i)),
                  pl.BlockSpec((k, tn), lambda i, *_: (0, i))],
        out_specs=pl.BlockSpec((k, tn), lambda i, *_: (0, i)),
    )
    return pl.pallas_call(
        _dest_kernel,
        grid_spec=grid_spec,
        out_shape=jax.ShapeDtypeStruct((k, n), I32),
        compiler_params=pltpu.CompilerParams(dimension_semantics=("arbitrary",)),
        name="dest",
    )(pstart, eidx, slot)


def _scatter_kernel(pfill_ref, pend_ref, nv_ref, dest_ref, h_hbm, xs_ref, zero_scr, h_buf, sems, lsem,
                    zsem):
    i = pl.program_id(0)
    n_steps = pl.num_programs(0)
    ts = h_buf.shape[1]
    tb = zero_scr.shape[0]
    n_tail = xs_ref.shape[0] // tb - nv_ref[0]

    def pad_fill(e, wait):
        def go(src, dst):
            cp = pltpu.make_async_copy(src, dst, zsem)
            cp.wait() if wait else cp.start()

        start = pfill_ref[e]
        end = pend_ref[e]
        head = jnp.minimum((-start) & 7, end - start)
        for r in range(7):
            @pl.when(r < head)
            def _():
                go(zero_scr.at[pl.ds(0, 1)], xs_ref.at[pl.ds(start + r, 1)])
        off = start + head
        rem = end - off
        size = tb // 2
        while size >= 8:
            cond = (rem & size) != 0

            @pl.when(cond)
            def _():
                go(zero_scr.at[pl.ds(0, size)], xs_ref.at[pl.ds(pl.multiple_of(off, 8), size)])
            off = off + jnp.where(cond, size, 0)
            size //= 2

    @pl.when(i == 0)
    def _():
        zero_scr[...] = jnp.zeros(zero_scr.shape, U32)

        def fill(e, carry):
            pad_fill(e, False)
            return carry

        lax.fori_loop(0, N_EXPERTS, fill, 0)

        def fill_tail(j, carry):
            start = pl.multiple_of((nv_ref[0] + j) * tb, tb)
            pltpu.make_async_copy(zero_scr, xs_ref.at[pl.ds(start, tb)], zsem).start()
            return carry

        lax.fori_loop(0, n_tail, fill_tail, 0)

        def drain(e, carry):
            pad_fill(e, True)
            return carry

        lax.fori_loop(0, N_EXPERTS, drain, 0)

        def drain_tail(j, carry):
            pltpu.make_async_copy(zero_scr, xs_ref.at[pl.ds(0, tb)], zsem).wait()
            return carry

        lax.fori_loop(0, n_tail, drain_tail, 0)

    def load(j):
        return pltpu.make_async_copy(h_hbm.at[pl.ds(pl.multiple_of(j * ts, ts), ts)],
                                     h_buf.at[lax.rem(j, H_SLOTS)], lsem.at[lax.rem(j, H_SLOTS)])

    def drain_scatters(j):
        sl = lax.rem(j, H_SLOTS)
        for k in range(TOP_K):
            pltpu.make_async_copy(h_buf.at[sl], xs_ref.at[pl.ds(0, ts)], sems.at[sl]).wait()

    @pl.when(i == 0)
    def _():
        load(0).start()

    @pl.when(i + 1 < n_steps)
    def _():
        load(i + 1).start()

    load(i).wait()
    cur = lax.rem(i, H_SLOTS)

    for t in range(ts):
        for k in range(TOP_K):
            pltpu.make_async_copy(h_buf.at[cur, pl.ds(t, 1)], xs_ref.at[pl.ds(dest_ref[k, t], 1)],
                                  sems.at[cur]).start(priority=k % 2)

    @pl.when(i >= 1)
    def _():
        drain_scatters(i - 1)

    @pl.when(i == n_steps - 1)
    def _():
        drain_scatters(i)


def _scatter(pfill, pend, n_valid, dest, hp, n_rows):
    n, w = hp.shape
    ts = T_ROW
    grid_spec = pltpu.PrefetchScalarGridSpec(
        num_scalar_prefetch=3,
        grid=(n // ts,),
        in_specs=[pl.BlockSpec((TOP_K, ts), lambda i, *_: (0, i), memory_space=pltpu.SMEM),
                  pl.BlockSpec(memory_space=pl.ANY)],
        out_specs=pl.BlockSpec(memory_space=pl.ANY),
        scratch_shapes=[pltpu.VMEM((TB_EXP, w), U32), pltpu.VMEM((H_SLOTS, ts, w), U32),
                        pltpu.SemaphoreType.DMA((H_SLOTS,)), pltpu.SemaphoreType.DMA((H_SLOTS,)),
                        pltpu.SemaphoreType.DMA],
    )
    return pl.pallas_call(
        _scatter_kernel,
        grid_spec=grid_spec,
        out_shape=jax.ShapeDtypeStruct((n_rows, w), U32),
        compiler_params=pltpu.CompilerParams(dimension_semantics=("arbitrary",),
                                             vmem_limit_bytes=VMEM_LIMIT),
        name="scatter",
    )(pfill, pend, n_valid, dest, hp)


def _experts_kernel(be_ref, nv_ref, ge_ref, ng_ref, xs_ref, wg_hbm, wu_hbm, wd_hbm, y_ref,
                    wg_f, wu_f, wd_f, wg_bf, wu_bf, wd_bf, sems, gctr):
    i = pl.program_id(0)
    prev = jnp.maximum(i - 1, 0)
    valid = i < nv_ref[0]
    fresh = valid & ((i == 0) | (be_ref[i] != be_ref[prev]))

    def weight_copies(g, slot):
        e = ge_ref[g]
        return (pltpu.make_async_copy(wg_hbm.at[e], wg_f.at[slot], sems.at[slot, 0]),
                pltpu.make_async_copy(wu_hbm.at[e], wu_f.at[slot], sems.at[slot, 1]),
                pltpu.make_async_copy(wd_hbm.at[e], wd_f.at[slot], sems.at[slot, 2]))

    @pl.when(i == 0)
    def _():
        gctr[0] = 0
        for g in range(W_SLOTS):
            @pl.when(g < ng_ref[0])
            def _():
                for cp in weight_copies(g, g):
                    cp.start()

    @pl.when(fresh)
    def _():
        g = gctr[0]
        slot = lax.rem(g, W_SLOTS)
        for cp in weight_copies(g, slot):
            cp.wait()
        wg_bf[...] = wg_f[slot].astype(BF16)
        wu_bf[...] = wu_f[slot].astype(BF16)
        wd_bf[...] = wd_f[slot].astype(BF16)

        @pl.when(g + W_SLOTS < ng_ref[0])
        def _():
            for cp in weight_copies(g + W_SLOTS, slot):
                cp.start()

        gctr[0] = g + 1

    @pl.when(valid)
    def _():
        lo, hi = _unpack_pair(xs_ref[...])
        x = jnp.concatenate([lo.astype(BF16), hi.astype(BF16)], axis=1)
        hg = jnp.dot(x, wg_bf[...], preferred_element_type=F32)
        hu = jnp.dot(x, wu_bf[...], preferred_element_type=F32)
        a = (_silu(hg) * hu).astype(BF16)
        y = jnp.dot(a, wd_bf[...], preferred_element_type=F32)
        half = y.shape[1] // 2
        y_ref[...] = _pack_pair(y[:, :half], y[:, half:])

    @pl.when(jnp.logical_not(valid))
    def _():
        y_ref[...] = jnp.zeros(y_ref.shape, U32)


def _experts(block_e, n_valid, group_e, n_groups, xs, w_gate, w_up, w_down, n_blocks):
    tb = TB_EXP
    w = xs.shape[1]
    _, d, f = w_gate.shape
    grid_spec = pltpu.PrefetchScalarGridSpec(
        num_scalar_prefetch=4,
        grid=(n_blocks,),
        in_specs=[pl.BlockSpec((tb, w), lambda i, be, nv, ge, ng: (jnp.minimum(i, nv[0] - 1), 0)),
                  pl.BlockSpec(memory_space=pl.ANY),
                  pl.BlockSpec(memory_space=pl.ANY),
                  pl.BlockSpec(memory_space=pl.ANY)],
        out_specs=pl.BlockSpec((tb, w), lambda i, be, nv, ge, ng: (i, 0)),
        scratch_shapes=[pltpu.VMEM((W_SLOTS, d, f), F32), pltpu.VMEM((W_SLOTS, d, f), F32),
                        pltpu.VMEM((W_SLOTS, f, d), F32),
                        pltpu.VMEM((d, f), BF16), pltpu.VMEM((d, f), BF16), pltpu.VMEM((f, d), BF16),
                        pltpu.SemaphoreType.DMA((W_SLOTS, 3)), pltpu.SMEM((1,), I32)],
    )
    return pl.pallas_call(
        _experts_kernel,
        grid_spec=grid_spec,
        out_shape=jax.ShapeDtypeStruct((n_blocks * tb, w), U32),
        compiler_params=pltpu.CompilerParams(dimension_semantics=("arbitrary",),
                                             vmem_limit_bytes=VMEM_LIMIT),
        name="experts",
    )(block_e, n_valid, group_e, n_groups, xs, w_gate, w_up, w_down)


def _combine_kernel(dcur_ref, dnxt_ref, x1_ref, hp_ref, gate_ref, gt_ref,
                    wsg_ref, wsu_ref, wsd_ref, y_ref, o_ref, buf, sems):
    i = pl.program_id(0)
    tc = x1_ref.shape[0]
    slot = lax.rem(i, 2)

    def issue(d_ref, sl):
        for t in range(tc):
            for k in range(TOP_K):
                pltpu.make_async_copy(y_ref.at[pl.ds(d_ref[k, t], 1)], buf.at[sl, k, pl.ds(t, 1)],
                                      sems.at[sl]).start(priority=k % 2)

    def compute():
        lo, hi = _unpack_pair(hp_ref[...])
        x = jnp.concatenate([lo.astype(BF16), hi.astype(BF16)], axis=1)
        hg = jnp.dot(x, wsg_ref[...], preferred_element_type=F32)
        hu = jnp.dot(x, wsu_ref[...], preferred_element_type=F32)
        a = (_silu(hg) * hu).astype(BF16)
        shared = jnp.dot(a, wsd_ref[...], preferred_element_type=F32)
        half = shared.shape[1] // 2
        gate = gate_ref[...]
        r_lo = shared[:, :half]
        r_hi = shared[:, half:]
        for k in range(TOP_K):
            lo, hi = _unpack_pair(buf[slot, k])
            gk = gate[:, k:k + 1]
            r_lo = r_lo + gk * lo
            r_hi = r_hi + gk * hi
        gt = gt_ref[0]
        o_ref[:, :half] = x1_ref[:, :half] + gt[:, :half] * r_lo
        o_ref[:, half:] = x1_ref[:, half:] + gt[:, half:] * r_hi

    @pl.when(i == 0)
    def _():
        issue(dcur_ref, 0)

    for k in range(TOP_K):
        pltpu.make_async_copy(y_ref.at[pl.ds(0, tc)], buf.at[slot, k], sems.at[slot]).wait()

    @pl.when(i + 1 < pl.num_programs(0))
    def _():
        issue(dnxt_ref, 1 - slot)
        compute()

    @pl.when(i + 1 >= pl.num_programs(0))
    def _():
        compute()


def _combine(dest, x1, hp, gate_t, gt2, wsg, wsu, wsd, y, seq):
    n, d = x1.shape
    tc = T_ROW
    w = hp.shape[1]
    f = wsg.shape[1]
    tiles_per_batch = seq // tc
    last = n // tc - 1
    return pl.pallas_call(
        _combine_kernel,
        grid=(n // tc,),
        in_specs=[pl.BlockSpec((TOP_K, tc), lambda i: (0, i), memory_space=pltpu.SMEM),
                  pl.BlockSpec((TOP_K, tc), lambda i: (0, jnp.minimum(i + 1, last)),
                               memory_space=pltpu.SMEM),
                  pl.BlockSpec((tc, d), lambda i: (i, 0)),
                  pl.BlockSpec((tc, w), lambda i: (i, 0)),
                  pl.BlockSpec((tc, TOP_K), lambda i: (i, 0)),
                  pl.BlockSpec((1, 1, d), lambda i: (i // tiles_per_batch, 0, 0)),
                  pl.BlockSpec((d, f), lambda i: (0, 0)),
                  pl.BlockSpec((d, f), lambda i: (0, 0)),
                  pl.BlockSpec((f, d), lambda i: (0, 0)),
                  pl.BlockSpec(memory_space=pl.ANY)],
        out_specs=pl.BlockSpec((tc, d), lambda i: (i, 0)),
        scratch_shapes=[pltpu.VMEM((2, TOP_K, tc, w), U32), pltpu.SemaphoreType.DMA((2,))],
        out_shape=jax.ShapeDtypeStruct((n, d), F32),
        compiler_params=pltpu.CompilerParams(dimension_semantics=("arbitrary",),
                                             vmem_limit_bytes=VMEM_LIMIT),
        name="combine",
    )(dest, dest, x1, hp, gate_t, gt2, wsg, wsu, wsd, y)


def kernel(x, c, positions, rel_bias, hgrn_lb_logits, w_ada, b_ada, g_mix, w_in, g_q, g_k, lam_q1, lam_k1, lam_q2, lam_k2, g_sub, g_hgrn, w_out, g_ffn, w_router, router_bias, w_exp_gate, w_exp_up, w_exp_down, w_sh_gate, w_sh_up, w_sh_down):
    batch, seq, d = x.shape
    n = batch * seq
    layer = 0
    x2d = x.reshape(n, d)

    c_pad = jnp.zeros((8, d), F32).at[:batch].set(c.astype(F32))
    mod = _ada(c_pad, w_ada[layer], b_ada[layer][None, :])[:batch]
    sh1, sc1, gt1, sh2, sc2, gt2 = [m.reshape(batch, 1, d) for m in jnp.split(mod, 6, axis=-1)]

    lbs = jnp.cumsum(jax.nn.softmax(hgrn_lb_logits.astype(F32), axis=1), axis=1)[:, layer]
    lbs = lbs.reshape(2, 1, SEG)
    reps = SEG // QK_DIM
    qk_gain = jnp.stack([jnp.tile(g_q[layer].astype(F32), reps) * (QK_DIM ** -0.5 * LOG2E),
                         jnp.tile(g_k[layer].astype(F32), reps)]).reshape(2, 1, SEG)
    lane = jnp.arange(LANES)
    g64 = jnp.where((lane[:, None] // QK_DIM) == (lane[None, :] // QK_DIM), 1.0 / QK_DIM, 0.0).astype(BF16)
    lam = (jnp.exp(jnp.sum(lam_q1[layer].astype(F32) * lam_k1[layer].astype(F32)))
           - jnp.exp(jnp.sum(lam_q2[layer].astype(F32) * lam_k2[layer].astype(F32)))
           + LAM_INIT).reshape(1)

    p, lf = _inproj(x2d, sc1, sh1, g_mix[layer][None, :], w_in[layer].astype(BF16), qk_gain, lbs, g64, seq)

    nt = seq // T_ATT
    pos_sub = positions.astype(I32).reshape(batch * seq // T_SUB, T_SUB)
    smin = jnp.min(pos_sub, axis=1)
    smax = jnp.max(pos_sub, axis=1)
    posq = positions.astype(I32).reshape(batch, nt, 1, T_ATT)
    posk = positions.astype(I32).reshape(batch, seq, 1)
    rb_t = rel_bias.astype(F32).T * LOG2E
    rb_tab = jnp.zeros((N_HEADS, LANES), F32).at[:, :REL_BUCKETS].set(rb_t)
    oa = _attention(p, smin, smax, posq, posk, rb_tab, rb_t, lam,
                    g_sub[layer][:, None].astype(F32), batch, seq)

    ci = jnp.arange(C_HGRN)
    tril = (ci[None, :] <= ci[:, None]).astype(BF16)
    triu = (ci[None, :] >= ci[:, None]).astype(BF16)
    oh = _hgrn(p, lf, tril, triu, g_hgrn[layer][None, :].astype(F32), batch, seq)

    half = d // 2
    w_out_bf = w_out[layer].astype(BF16)
    wr_t = w_router[layer].astype(F32).T
    wr_hi = wr_t.astype(BF16)
    wr_lo = (wr_t - wr_hi.astype(F32)).astype(BF16)
    ti = jnp.arange(TM_OUT)
    upper = (ti[:, None] < ti[None, :]).astype(BF16)
    ones = jnp.ones((TM_OUT, LANES), BF16)
    x1, hp, eidx, slot, gate, cnt = _outproj(
        oa, oh, w_out_bf, x2d, gt1, g_ffn[layer][None, :], sc2, sh2,
        wr_hi, wr_lo, router_bias[layer].astype(F32)[:, None], upper, ones, seq)

    tb = TB_EXP
    counts = cnt[:, 0].astype(I32)
    padded = (counts + tb - 1) // tb * tb
    pends = jnp.cumsum(padded)
    pstart = (pends - padded).astype(I32)
    n_blocks = (n * TOP_K) // tb + N_EXPERTS
    n_valid = (pends[-1] // tb).astype(I32).reshape(1)
    blk_start = jnp.arange(n_blocks, dtype=I32) * tb
    block_e = jnp.minimum(jnp.sum(pends[None, :] <= blk_start[:, None], axis=1), N_EXPERTS - 1).astype(I32)
    pfill = (pstart + counts).astype(I32)
    pend = pends.astype(I32)
    dest = _dest(pstart, eidx, slot)
    has_rows = counts > 0
    group_e = jnp.nonzero(has_rows, size=N_EXPERTS, fill_value=0)[0].astype(I32)
    n_groups = jnp.sum(has_rows).astype(I32).reshape(1)

    xs = _scatter(pfill, pend, n_valid, dest, hp, n_blocks * tb)
    y = _experts(block_e, n_valid, group_e, n_groups, xs,
                 w_exp_gate[layer], w_exp_up[layer], w_exp_down[layer], n_blocks)
    out = _combine(dest, x1, hp, gate.T, gt2,
                   w_sh_gate[layer].astype(BF16), w_sh_up[layer].astype(BF16),
                   w_sh_down[layer].astype(BF16), y, seq)
    return out.reshape(batch, seq, d)
```

```python
import functools
import math

import jax
import jax.numpy as jnp
from jax import lax
from jax.experimental import pallas as pl
from jax.experimental.pallas import tpu as pltpu

F32 = jnp.float32
BF16 = jnp.bfloat16
I32 = jnp.int32
U32 = jnp.uint32

D_MODEL = 2048
N_HEADS = 8
QK_DIM = 64
HEAD_DIM = 128
SEG = 1024
N_SEG = 8
REL_BUCKETS = 32
REL_MAX_DIST = 128
N_EXPERTS = 256
TOP_K = 8
N_GROUPS = 8
TOPK_GROUPS = 4
GROUP_SIZE = N_EXPERTS // N_GROUPS
EXPERT_DIM = 512
ROUTED_SCALE = 2.5
EPS = 1e-6
LAM_INIT = 0.8 - 0.6 * math.exp(-0.3 * 0)
LOG2E = math.log2(math.e)

LANES = 128
VMEM_LIMIT = 56 * 1024 * 1024

TM_IN = 512
IN_CHUNK = 256
T_ATT = 512
T_KEY = 1024
T_SUB = 128
V_PAD = 16
C_HGRN = 64
HGRN_GROUP = 16
TM_OUT = 256
T_ROW = 256
TB_EXP = 256
W_SLOTS = 2
H_SLOTS = 3
NEG_BIG = -1e30
EXP_CLAMP = 80.0
FAST_BOUND = 60.0
BOUND_SLACK = 1.02
BOUND_PAD = 0.01

NT_DIMS = (((1,), (1,)), ((), ()))


def _silu(x):
    return x * jax.nn.sigmoid(x)


def _pack_pair(lo_f32, hi_f32):
    lo = lax.bitcast_convert_type(lo_f32.astype(BF16).astype(F32), U32)
    hi = lax.bitcast_convert_type(hi_f32.astype(BF16).astype(F32), U32)
    return (hi & jnp.uint32(0xFFFF0000)) | (lo >> 16)


def _unpack_pair(word):
    lo = lax.bitcast_convert_type(word << 16, F32)
    hi = lax.bitcast_convert_type(word & jnp.uint32(0xFFFF0000), F32)
    return lo, hi


def _ada_kernel(c_ref, w_ref, b_ref, o_ref):
    a = _silu(c_ref[...]).astype(BF16)
    o_ref[...] = jnp.dot(a, w_ref[...].astype(BF16), preferred_element_type=F32) + b_ref[...]


def _ada(c_pad, w, b):
    d, n = w.shape
    tn = 1024
    return pl.pallas_call(
        _ada_kernel,
        grid=(n // tn,),
        in_specs=[pl.BlockSpec((8, d), lambda j: (0, 0)),
                  pl.BlockSpec((d, tn), lambda j: (0, j)),
                  pl.BlockSpec((1, tn), lambda j: (0, j))],
        out_specs=pl.BlockSpec((8, tn), lambda j: (0, j)),
        out_shape=jax.ShapeDtypeStruct((8, n), F32),
        compiler_params=pltpu.CompilerParams(dimension_semantics=("arbitrary",),
                                             vmem_limit_bytes=VMEM_LIMIT),
        name="ada",
    )(c_pad, w, b)


def _inproj_kernel(x_ref, sc_ref, sh_ref, gmix_ref, w_ref, qkg_ref, lb_ref, g64_ref,
                   p_ref, lf_ref, h_scr):
    j = pl.program_id(1)

    @pl.when(j == 0)
    def _():
        x = x_ref[...]
        ms = jnp.mean(x * x, axis=-1, keepdims=True)
        y = x * lax.rsqrt(ms + EPS) * gmix_ref[...]
        h_scr[...] = (y * (1.0 + sc_ref[0]) + sh_ref[0]).astype(BF16)

    def chunks(epilogue):
        for c in range(SEG // IN_CHUNK):
            sl = slice(c * IN_CHUNK, (c + 1) * IN_CHUNK)
            epilogue(sl, jnp.dot(h_scr[...], w_ref[:, sl], preferred_element_type=F32))

    @pl.when(j < 2)
    def _():
        acc = jnp.dot(h_scr[...], w_ref[...], preferred_element_type=F32)
        gain = qkg_ref[0]
        for c in range(SEG // LANES):
            sl = slice(c * LANES, (c + 1) * LANES)
            xs = acc[:, sl]
            ms = jnp.dot((xs * xs).astype(BF16), g64_ref[...], preferred_element_type=F32)
            p_ref[:, sl] = (xs * lax.rsqrt(ms + EPS) * gain[:, sl]).astype(BF16)

    @pl.when((j == 2) | (j == 4))
    def _():
        def plain(sl, acc):
            p_ref[:, sl] = acc.astype(BF16)

        chunks(plain)

    @pl.when((j == 3) | (j == 7))
    def _():
        def silu(sl, acc):
            p_ref[:, sl] = _silu(acc).astype(BF16)

        chunks(silu)

    @pl.when((j == 5) | (j == 6))
    def _():
        def log_gate(sl, z):
            lb = lb_ref[0, :, sl]
            f = lb + (1.0 - lb) * jax.nn.sigmoid(z)
            lf_ref[:, sl] = jnp.log(f)
            p_ref[:, sl] = z.astype(BF16)

        chunks(log_gate)


def _inproj(x2d, sc1, sh1, g_mix, w_in_bf, qk_gain, lbs, g64, seq):
    n, d = x2d.shape
    tm = TM_IN
    tiles_per_batch = seq // tm
    return pl.pallas_call(
        _inproj_kernel,
        grid=(n // tm, N_SEG),
        in_specs=[
            pl.BlockSpec((tm, d), lambda i, j: (i, 0)),
            pl.BlockSpec((1, 1, d), lambda i, j: (i // tiles_per_batch, 0, 0)),
            pl.BlockSpec((1, 1, d), lambda i, j: (i // tiles_per_batch, 0, 0)),
            pl.BlockSpec((1, d), lambda i, j: (0, 0)),
            pl.BlockSpec((d, SEG), lambda i, j: (0, j)),
            pl.BlockSpec((1, 1, SEG), lambda i, j: (jnp.minimum(j, 1), 0, 0)),
            pl.BlockSpec((1, 1, SEG), lambda i, j: (jnp.clip(j - 5, 0, 1), 0, 0)),
            pl.BlockSpec((LANES, LANES), lambda i, j: (0, 0)),
        ],
        out_specs=[
            pl.BlockSpec((tm, SEG), lambda i, j: (i, j)),
            pl.BlockSpec((tm, SEG), lambda i, j: (i, jnp.clip(j - 5, 0, 1))),
        ],
        out_shape=[jax.ShapeDtypeStruct((n, N_SEG * SEG), BF16),
                   jax.ShapeDtypeStruct((n, 2 * SEG), F32)],
        scratch_shapes=[pltpu.VMEM((tm, d), BF16)],
        compiler_params=pltpu.CompilerParams(dimension_semantics=("arbitrary", "arbitrary"),
                                             vmem_limit_bytes=VMEM_LIMIT),
        name="inproj",
    )(x2d, sc1, sh1, g_mix, w_in_bf, qk_gain, lbs, g64)


def _t5_bias_tile(pos_q, pos_k, table):
    half = REL_BUCKETS // 2
    max_exact = half // 2
    rel = pos_k - pos_q
    n = jnp.abs(rel)
    nf = jnp.maximum(n, 1).astype(F32)
    large = max_exact + (jnp.log(nf / max_exact) / math.log(REL_MAX_DIST / max_exact)
                         * (half - max_exact)).astype(I32)
    large = jnp.minimum(large, half - 1)
    bucket = jnp.where(rel > 0, half, 0) + jnp.where(n < max_exact, n, large)
    rows = bucket.shape[0]
    tbl = jnp.broadcast_to(table, (rows, LANES))
    cols = [jnp.take_along_axis(tbl, bucket[:, c * LANES:(c + 1) * LANES], axis=1)
            for c in range(bucket.shape[1] // LANES)]
    return jnp.concatenate(cols, axis=1)


def _attn_kernel(smin_ref, smax_ref, q_ref, k_ref, v_ref, posq_ref, posk_ref, rbt_ref, rbx_ref, rb_ref, lam_ref,
                 gsub_ref, o_ref, vt_scr, s_a, s_b, cm_a, cm_b, p_a, p_b, al_a, al_b, m_scr, a_scr,
                 kmax_scr):
    b = pl.program_id(0)
    h = pl.program_id(1)
    i = pl.program_id(2)
    tq = T_ATT
    tk = T_KEY
    n_sub = tk // T_SUB
    nq_sub = tq // T_SUB
    ntk = k_ref.shape[0] // tk
    n_pairs = ntk // 2
    subs_per_batch = k_ref.shape[0] // T_SUB

    @pl.when(i == 0)
    def _():
        ones_row = jnp.where(lax.broadcasted_iota(I32, (V_PAD, tk), 0) == 0, 1.0, 0.0).astype(BF16)
        kmax_scr[...] = jnp.zeros(kmax_scr.shape, F32)

        def tr(c, carry):
            r0 = pl.multiple_of(c * tk, tk)
            vt_scr[c, :HEAD_DIM, :] = v_ref[pl.ds(r0, tk), :].astype(F32).T.astype(BF16)
            vt_scr[c, HEAD_DIM:, :] = ones_row
            kf = k_ref[pl.ds(r0, tk), :].astype(F32)
            for mp in range(2):
                km = kf[:, mp * QK_DIM:(mp + 1) * QK_DIM]
                nk = jnp.dot(km * km, jnp.ones((QK_DIM, LANES), F32), preferred_element_type=F32)
                kmax_scr[mp] = jnp.maximum(kmax_scr[mp], jnp.max(nk, axis=0, keepdims=True))
            return carry

        lax.fori_loop(0, ntk, tr, 0)

    q = q_ref[...]
    qs = (q[:, :QK_DIM], q[:, QK_DIM:])
    m_scr[...] = jnp.full(m_scr.shape, NEG_BIG, F32)
    a_scr[...] = jnp.zeros(a_scr.shape, F32)

    sub0 = b * subs_per_batch + i * nq_sub
    q_lo = smin_ref[sub0]
    q_hi = smax_ref[sub0]
    for u in range(1, nq_sub):
        q_lo = jnp.minimum(q_lo, smin_ref[sub0 + u])
        q_hi = jnp.maximum(q_hi, smax_ref[sub0 + u])
    c_pos = rb_ref[h, REL_BUCKETS - 1]
    c_neg = rb_ref[h, REL_BUCKETS // 2 - 1]
    pos_q = posq_ref[0, 0]

    def classify(j, u):
        ksub = b * subs_per_batch + j * n_sub + u
        lo = smin_ref[ksub] - q_hi
        hi = smax_ref[ksub] - q_lo
        far = (lo >= REL_MAX_DIST) | (hi <= -REL_MAX_DIST)
        shift = jnp.where(lo >= REL_MAX_DIST, c_pos, jnp.where(hi <= -REL_MAX_DIST, c_neg, 0.0))
        return far, shift

    def scores(j, s_ref, cm_ref):
        kk = k_ref[pl.ds(pl.multiple_of(j * tk, tk), tk), :]
        ks = (kk[:, :QK_DIM], kk[:, QK_DIM:])
        for mp in range(2):
            s = lax.dot_general(ks[mp], qs[mp], NT_DIMS, preferred_element_type=F32)
            s_ref[mp] = s
            for u in range(n_sub):
                _, shift = classify(j, u)
                cm_ref[mp, u] = jnp.max(s[u * T_SUB:(u + 1) * T_SUB], axis=0, keepdims=True) + shift

    def fixup(j, s_ref, cm_ref):
        for u in range(n_sub):
            far, _ = classify(j, u)

            @pl.when(jnp.logical_not(far))
            def _():
                rows = pl.ds(u * T_SUB, T_SUB)
                pos_k = posk_ref[0, pl.ds(pl.multiple_of(j * tk + u * T_SUB, T_SUB), T_SUB), :]
                bias = _t5_bias_tile(pos_q, pos_k, rbt_ref[pl.ds(h, 1), :])
                for mp in range(2):
                    sb = s_ref[mp, rows, :] + bias
                    s_ref[mp, rows, :] = sb
                    cm_ref[mp, u] = jnp.max(sb, axis=0, keepdims=True)

    def soft(j, s_ref, cm_ref, p_ref, al_ref):
        for mp in range(2):
            m_old = m_scr[mp]
            m_new = m_old
            for u in range(n_sub):
                m_new = jnp.maximum(m_new, cm_ref[mp, u])
            for u in range(n_sub):
                rows = pl.ds(u * T_SUB, T_SUB)
                _, shift = classify(j, u)
                p_ref[mp, rows, :] = jnp.exp2(s_ref[mp, rows, :] - (m_new - shift)).astype(BF16)
            m_scr[mp] = m_new
            al_ref[mp] = jnp.exp2(m_old - m_new)

    def pv(j, p_ref, al_ref):
        vt = vt_scr[j]
        for mp in range(2):
            a_scr[mp] = al_ref[mp] * a_scr[mp] + jnp.dot(vt, p_ref[mp], preferred_element_type=F32)

    sbuf = ((s_a, cm_a), (s_b, cm_b))
    pbuf = ((p_a, al_a), (p_b, al_b))
    last = 2 * (n_pairs - 1)

    def online_path():
        def step(j, par):
            scores(j + 2, *sbuf[par])
            soft(j + 1, *sbuf[1 - par], *pbuf[1 - par])
            pv(j, *pbuf[par])
            fixup(j + 2, *sbuf[par])

        scores(0, *sbuf[0])
        fixup(0, *sbuf[0])
        scores(1, *sbuf[1])
        soft(0, *sbuf[0], *pbuf[0])
        fixup(1, *sbuf[1])

        def body(jj, carry):
            step(2 * jj, 0)
            step(2 * jj + 1, 1)
            return carry

        lax.fori_loop(0, n_pairs - 1, body, 0)
        soft(last + 1, *sbuf[1], *pbuf[1])
        pv(last, *pbuf[0])
        pv(last + 1, *pbuf[1])

    def col_bound(mp):
        qf = qs[mp].astype(F32)
        nq = lax.dot_general(jnp.ones((8, QK_DIM), F32), qf * qf, NT_DIMS,
                             preferred_element_type=F32)[0:1, :]
        return jnp.sqrt(nq * kmax_scr[mp, :, 0:1]) * BOUND_SLACK

    b_max = rb_ref[h, 0]
    for e in range(1, REL_BUCKETS):
        b_max = jnp.maximum(b_max, rb_ref[h, e])
    bounds = [col_bound(mp) + (b_max + BOUND_PAD) for mp in range(2)]
    bound_max = jnp.max(jnp.maximum(bounds[0], bounds[1]))

    def fast_scores(j, p_ref):
        kk = k_ref[pl.ds(pl.multiple_of(j * tk, tk), tk), :]
        ks = (kk[:, :QK_DIM], kk[:, QK_DIM:])
        for mp in range(2):
            s = lax.dot_general(ks[mp], qs[mp], NT_DIMS, preferred_element_type=F32)
            for u in range(n_sub):
                _, shift = classify(j, u)
                p_ref[mp, pl.ds(u * T_SUB, T_SUB), :] = jnp.exp2(
                    s[u * T_SUB:(u + 1) * T_SUB] - (bounds[mp] - shift)).astype(BF16)

    def fast_fixup(j, p_ref):
        fars = [classify(j, u)[0] for u in range(n_sub)]
        all_far = fars[0]
        for f in fars[1:]:
            all_far = all_far & f

        @pl.when(jnp.logical_not(all_far))
        def _():
            for u in range(n_sub):
                @pl.when(jnp.logical_not(fars[u]))
                def _():
                    rows = pl.ds(u * T_SUB, T_SUB)
                    pos_k = posk_ref[0, pl.ds(pl.multiple_of(j * tk + u * T_SUB, T_SUB), T_SUB), :]
                    scale = _t5_bias_tile(pos_q, pos_k, rbx_ref[pl.ds(h, 1), :]).astype(BF16)
                    for mp in range(2):
                        p_ref[mp, rows, :] = p_ref[mp, rows, :] * scale

    def fast_pv(j, p_ref):
        vt = vt_scr[j]
        for mp in range(2):
            a_scr[mp] = a_scr[mp] + jnp.dot(vt, p_ref[mp], preferred_element_type=F32)

    def fast_path():
        pb = (p_a, p_b)

        def step(j, par):
            fast_scores(j + 1, pb[1 - par])
            fast_pv(j, pb[par])
            fast_fixup(j + 1, pb[1 - par])

        fast_scores(0, pb[0])
        fast_fixup(0, pb[0])

        def body(jj, carry):
            step(2 * jj, 0)
            step(2 * jj + 1, 1)
            return carry

        lax.fori_loop(0, n_pairs - 1, body, 0)
        step(last, 0)
        fast_pv(last + 1, pb[1])

    use_fast = bound_max <= FAST_BOUND

    @pl.when(use_fast)
    def _():
        fast_path()

    @pl.when(jnp.logical_not(use_fast))
    def _():
        online_path()

    num = [a_scr[mp, :HEAD_DIM, :] / a_scr[mp, HEAD_DIM:HEAD_DIM + 1, :] for mp in range(2)]
    o = num[0] - lam_ref[0] * num[1]
    ms = jnp.mean(o * o, axis=0, keepdims=True)
    o = o * lax.rsqrt(ms + EPS) * (gsub_ref[...] * (1.0 - LAM_INIT))
    o_ref[...] = o.T.astype(BF16)


def _attention(p, smin, smax, posq, posk, rb_tab, rb_t, lam, g_sub, batch, seq):
    t = T_ATT
    tk = T_KEY
    nt = seq // t
    ntk = seq // tk
    assert seq % (2 * tk) == 0
    n = batch * seq
    n_sub = tk // T_SUB
    va = HEAD_DIM + V_PAD
    grid_spec = pltpu.PrefetchScalarGridSpec(
        num_scalar_prefetch=2,
        grid=(batch, N_HEADS, nt),
        in_specs=[
            pl.BlockSpec((t, HEAD_DIM), lambda b, h, i, *_: (b * nt + i, h)),
            pl.BlockSpec((seq, HEAD_DIM), lambda b, h, i, *_: (b, N_HEADS + h)),
            pl.BlockSpec((seq, HEAD_DIM), lambda b, h, i, *_: (b, 2 * N_HEADS + h)),
            pl.BlockSpec((1, 1, 1, t), lambda b, h, i, *_: (b, i, 0, 0)),
            pl.BlockSpec((1, seq, 1), lambda b, h, i, *_: (b, 0, 0)),
            pl.BlockSpec((N_HEADS, LANES), lambda b, h, i, *_: (0, 0)),
            pl.BlockSpec((N_HEADS, LANES), lambda b, h, i, *_: (0, 0)),
            pl.BlockSpec(memory_space=pltpu.SMEM),
            pl.BlockSpec(memory_space=pltpu.SMEM),
            pl.BlockSpec((HEAD_DIM, 1), lambda b, h, i, *_: (0, 0)),
        ],
        out_specs=pl.BlockSpec((t, HEAD_DIM), lambda b, h, i, *_: (b * nt + i, h)),
        scratch_shapes=[pltpu.VMEM((ntk, va, tk), BF16),
                        pltpu.VMEM((2, tk, t), F32),
                        pltpu.VMEM((2, tk, t), F32),
                        pltpu.VMEM((2, n_sub, 1, t), F32),
                        pltpu.VMEM((2, n_sub, 1, t), F32),
                        pltpu.VMEM((2, tk, t), BF16),
                        pltpu.VMEM((2, tk, t), BF16),
                        pltpu.VMEM((2, 1, t), F32),
                        pltpu.VMEM((2, 1, t), F32),
                        pltpu.VMEM((2, 1, t), F32),
                        pltpu.VMEM((2, va, t), F32),
                        pltpu.VMEM((2, 1, LANES), F32)],
    )
    return pl.pallas_call(
        _attn_kernel,
        grid_spec=grid_spec,
        out_shape=jax.ShapeDtypeStruct((n, N_HEADS * HEAD_DIM), BF16),
        compiler_params=pltpu.CompilerParams(
            dimension_semantics=("arbitrary", "arbitrary", "arbitrary"),
            vmem_limit_bytes=VMEM_LIMIT),
        name="attn",
    )(smin, smax, p, p, p, posq, posk, rb_tab, jnp.exp2(rb_tab), rb_t, lam, g_sub)


def _hgrn_group(r0, forward, q_ref, v_ref, lf_ref, tri_ref, st):
    c = C_HGRN
    order = range(HGRN_GROUP) if forward else range(HGRN_GROUP - 1, -1, -1)
    rows = [pl.ds(r0 + k * c, c) for k in order]
    tri = tri_ref[...]
    row = lax.broadcasted_iota(I32, (c, c), 0)
    col = lax.broadcasted_iota(I32, (c, c), 1)
    keep = (col <= row) if forward else (col >= row)

    gs = [lf_ref[r, :] for r in rows]
    bsums = []
    for g in gs:
        g_hi = g.astype(BF16)
        g_lo = (g - g_hi.astype(F32)).astype(BF16)
        bsums.append(jnp.dot(tri, g_hi, preferred_element_type=F32)
                     + jnp.dot(tri, g_lo, preferred_element_type=F32))

    q_in, q_t, k_t, k_st, v_t, vs, decay = [], [], [], [], [], [], []
    for r, g, bsum in zip(rows, gs, bsums):
        if forward:
            ref = bsum[c // 2 - 1:c // 2, :]
            b_end = bsum[c - 1:c, :]
        else:
            ref = bsum[c // 2:c // 2 + 1, :]
            b_end = bsum[0:1, :]
        q = q_ref[r, :].astype(F32)
        v = v_ref[r, :]
        kf = 1.0 - jnp.exp(g)
        q_in.append((q * jnp.exp(bsum)).astype(BF16))
        q_t.append((q * jnp.exp(jnp.minimum(bsum - ref, EXP_CLAMP))).astype(BF16))
        k_t.append((kf * jnp.exp(jnp.minimum(ref - bsum, EXP_CLAMP))).astype(BF16))
        k_st.append((kf * jnp.exp(b_end - bsum)).astype(BF16))
        v_t.append(v.astype(F32).T.astype(BF16))
        vs.append(v)
        decay.append(jnp.exp(b_end))

    scores = [lax.dot_general(a, b, NT_DIMS, preferred_element_type=F32) for a, b in zip(q_t, k_t)]
    st_add = [jnp.dot(a, b, preferred_element_type=F32) for a, b in zip(v_t, k_st)]
    intra = [jnp.dot(jnp.where(keep, s, 0.0).astype(BF16), v, preferred_element_type=F32)
             for s, v in zip(scores, vs)]

    outs = []
    for k in range(HGRN_GROUP):
        o = lax.dot_general(q_in[k], st.astype(BF16), NT_DIMS, preferred_element_type=F32) + intra[k]
        st = st * decay[k] + st_add[k]
        outs.append((rows[k], o))
    return outs, st


def _hgrn_kernel(q_ref, v_ref, g_ref, lff_ref, lfb_ref, tril_ref, triu_ref, gh_ref, o_ref, of_scr):
    rows_per_group = C_HGRN * HGRN_GROUP
    n_groups = q_ref.shape[0] // rows_per_group
    st0 = jnp.zeros((HEAD_DIM, HEAD_DIM), F32)

    def fwd(gi, st):
        r0 = pl.multiple_of(gi * rows_per_group, rows_per_group)
        outs, st = _hgrn_group(r0, True, q_ref, v_ref, lff_ref, tril_ref, st)
        for r, o in outs:
            of_scr[r, :] = o
        return st

    lax.fori_loop(0, n_groups, fwd, st0)

    def bwd(gi, st):
        r0 = pl.multiple_of((n_groups - 1 - gi) * rows_per_group, rows_per_group)
        outs, st = _hgrn_group(r0, False, q_ref, v_ref, lfb_ref, triu_ref, st)
        for r, o in outs:
            o = of_scr[r, :] + o
            ms = jnp.mean(o * o, axis=-1, keepdims=True)
            y = o * lax.rsqrt(ms + EPS) * gh_ref[...] * g_ref[r, :].astype(F32)
            o_ref[r, :] = y.astype(BF16)
        return st

    lax.fori_loop(0, n_groups, bwd, st0)


def _hgrn(p, lf, tril, triu, g_hgrn, batch, seq):
    n = batch * seq
    c = C_HGRN
    blk = lambda off: pl.BlockSpec((seq, HEAD_DIM), lambda b, h: (b, off + h))
    return pl.pallas_call(
        _hgrn_kernel,
        grid=(batch, N_HEADS),
        in_specs=[blk(3 * N_HEADS), blk(4 * N_HEADS), blk(7 * N_HEADS), blk(0), blk(N_HEADS),
                  pl.BlockSpec((c, c), lambda b, h: (0, 0)),
                  pl.BlockSpec((c, c), lambda b, h: (0, 0)),
                  pl.BlockSpec((1, HEAD_DIM), lambda b, h: (0, 0))],
        out_specs=pl.BlockSpec((seq, HEAD_DIM), lambda b, h: (b, h)),
        out_shape=jax.ShapeDtypeStruct((n, N_HEADS * HEAD_DIM), BF16),
        scratch_shapes=[pltpu.VMEM((seq, HEAD_DIM), F32)],
        compiler_params=pltpu.CompilerParams(dimension_semantics=("arbitrary", "arbitrary"),
                                             vmem_limit_bytes=VMEM_LIMIT),
        name="hgrn",
    )(p, p, p, lf, lf, tril, triu, g_hgrn)


def _col_max(x):
    return jnp.max(x, axis=0, keepdims=True)


def _outproj_kernel(oa_ref, oh_ref, w_ref, x_ref, gt_ref, gffn_ref, sc_ref, sh_ref,
                    wr_hi_ref, wr_lo_ref, rbias_ref, upper_ref, ones_ref,
                    x1_ref, hp_ref, eidx_ref, slot_ref, gate_ref, cnt_ref, cnt_scr):
    i = pl.program_id(0)
    tm = x_ref.shape[0]

    @pl.when(i == 0)
    def _():
        cnt_scr[...] = jnp.zeros(cnt_scr.shape, F32)

    mixed = jnp.concatenate([oa_ref[...], oh_ref[...]], axis=1)
    acc = jnp.dot(mixed, w_ref[...], preferred_element_type=F32)
    x1 = x_ref[...] + gt_ref[0] * acc
    x1_ref[...] = x1
    ms = jnp.mean(x1 * x1, axis=-1, keepdims=True)
    h2 = x1 * lax.rsqrt(ms + EPS) * gffn_ref[...] * (1.0 + sc_ref[0]) + sh_ref[0]
    half = h2.shape[1] // 2
    hp_ref[...] = _pack_pair(h2[:, :half], h2[:, half:])

    h_hi = h2.astype(BF16)
    h_lo = (h2 - h_hi.astype(F32)).astype(BF16)
    wr_hi = wr_hi_ref[...]
    logits = (lax.dot_general(wr_hi, h_hi, NT_DIMS, preferred_element_type=F32)
              + lax.dot_general(wr_hi, h_lo, NT_DIMS, preferred_element_type=F32)
              + lax.dot_general(wr_lo_ref[...], h_hi, NT_DIMS, preferred_element_type=F32))
    scores = jax.nn.sigmoid(logits)
    biased = scores + rbias_ref[...]

    gs = []
    for g in range(N_GROUPS):
        blk = biased[g * GROUP_SIZE:(g + 1) * GROUP_SIZE, :]
        top1 = _col_max(blk)
        eq = blk == top1
        n_eq = jnp.sum(eq.astype(F32), axis=0, keepdims=True)
        second = _col_max(jnp.where(eq, -jnp.inf, blk))
        gs.append(top1 + jnp.where(n_eq > 1.0, top1, second))
    gsm = jnp.concatenate(gs, axis=0)
    giota = lax.broadcasted_iota(I32, gsm.shape, 0)
    gsel = jnp.zeros(gsm.shape, F32)
    for _ in range(TOPK_GROUPS):
        top = _col_max(gsm)
        idx = jnp.min(jnp.where(gsm == top, giota, N_GROUPS), axis=0, keepdims=True)
        pick = giota == idx
        gsel = jnp.where(pick, 1.0, gsel)
        gsm = jnp.where(pick, -jnp.inf, gsm)
    emask = jnp.concatenate(
        [jnp.broadcast_to(gsel[g:g + 1, :], (GROUP_SIZE, tm)) for g in range(N_GROUPS)], axis=0)
    masked = jnp.where(emask > 0.5, biased, -jnp.inf)

    eiota = lax.broadcasted_iota(I32, masked.shape, 0)
    idxs, gates = [], []
    for _ in range(TOP_K):
        top = _col_max(masked)
        idx = jnp.min(jnp.where(masked == top, eiota, N_EXPERTS), axis=0, keepdims=True)
        pick = eiota == idx
        gates.append(jnp.sum(jnp.where(pick, scores, 0.0), axis=0, keepdims=True))
        idxs.append(idx)
        masked = jnp.where(pick, -jnp.inf, masked)
    gate = jnp.concatenate(gates, axis=0)
    gate = gate / jnp.sum(gate, axis=0, keepdims=True) * ROUTED_SCALE
    eidx = jnp.concatenate(idxs, axis=0)
    eidx_ref[...] = eidx
    gate_ref[...] = gate

    sel = jnp.zeros(masked.shape, F32)
    for k in range(TOP_K):
        sel = jnp.where(eiota == idxs[k], 1.0, sel)
    sel_bf = sel.astype(BF16)
    rank = jnp.dot(sel_bf, upper_ref[...], preferred_element_type=F32)
    base = cnt_scr[...]
    posn = base[:, :1] + rank
    slots = [jnp.sum(jnp.where(eiota == idxs[k], posn, 0.0), axis=0, keepdims=True)
             for k in range(TOP_K)]
    slot_ref[...] = jnp.concatenate(slots, axis=0).astype(I32)
    new_cnt = base + jnp.dot(sel_bf, ones_ref[...], preferred_element_type=F32)
    cnt_scr[...] = new_cnt
    cnt_ref[...] = new_cnt


def _outproj(oa, oh, w_out, x2d, gt1, g_ffn, sc2, sh2, wr_hi, wr_lo, rbias, upper, ones, seq):
    n, d = x2d.shape
    tm = TM_OUT
    tiles_per_batch = seq // tm
    half = d // 2
    row = lambda w: pl.BlockSpec((tm, w), lambda i: (i, 0))
    const = lambda shape: pl.BlockSpec(shape, lambda i: tuple(0 for _ in shape))
    per_batch = pl.BlockSpec((1, 1, d), lambda i: (i // tiles_per_batch, 0, 0))
    tok = pl.BlockSpec((TOP_K, tm), lambda i: (0, i))
    return pl.pallas_call(
        _outproj_kernel,
        grid=(n // tm,),
        in_specs=[row(half), row(half), const((d, d)), row(d), per_batch,
                  const((1, d)), per_batch, per_batch,
                  const((N_EXPERTS, d)), const((N_EXPERTS, d)), const((N_EXPERTS, 1)),
                  const((tm, tm)), const((tm, LANES))],
        out_specs=[row(d), row(half), tok, tok, tok, const((N_EXPERTS, LANES))],
        out_shape=[jax.ShapeDtypeStruct((n, d), F32),
                   jax.ShapeDtypeStruct((n, half), U32),
                   jax.ShapeDtypeStruct((TOP_K, n), I32),
                   jax.ShapeDtypeStruct((TOP_K, n), I32),
                   jax.ShapeDtypeStruct((TOP_K, n), F32),
                   jax.ShapeDtypeStruct((N_EXPERTS, LANES), F32)],
        scratch_shapes=[pltpu.VMEM((N_EXPERTS, LANES), F32)],
        compiler_params=pltpu.CompilerParams(dimension_semantics=("arbitrary",),
                                             vmem_limit_bytes=VMEM_LIMIT),
        name="outproj",
    )(oa, oh, w_out, x2d, gt1, g_ffn, sc2, sh2, wr_hi, wr_lo, rbias, upper, ones)


def _dest_kernel(pstart_ref, e_ref, slot_ref, o_ref):
    e = e_ref[...]

    def body(x, acc):
        return acc + jnp.where(e == x, pstart_ref[x], 0)

    o_ref[...] = lax.fori_loop(0, N_EXPERTS, body, slot_ref[...], unroll=8)


def _dest(pstart, eidx, slot):
    k, n = eidx.shape
    tn = min(n, 2048)
    grid_spec = pltpu.PrefetchScalarGridSpec(
        num_scalar_prefetch=1,
        grid=(n // tn,),
        in_specs=[pl.BlockSpec((k, tn), lambda i, *_: (0, i)),
                  pl.BlockSpec((k, tn), lambda i, *_: (0, i))],
        out_specs=pl.BlockSpec((k, tn), lambda i, *_: (0, i)),
    )
    return pl.pallas_call(
        _dest_kernel,
        grid_spec=grid_spec,
        out_shape=jax.ShapeDtypeStruct((k, n), I32),
        compiler_params=pltpu.CompilerParams(dimension_semantics=("arbitrary",)),
        name="dest",
    )(pstart, eidx, slot)


def _scatter_kernel(pfill_ref, pend_ref, nv_ref, dest_ref, h_hbm, xs_ref, zero_scr, h_buf, sems, lsem,
                    zsem):
    i = pl.program_id(0)
    n_steps = pl.num_programs(0)
    ts = h_buf.shape[1]
    tb = zero_scr.shape[0]
    n_tail = xs_ref.shape[0] // tb - nv_ref[0]

    def pad_fill(e, wait):
        def go(src, dst):
            cp = pltpu.make_async_copy(src, dst, zsem)
            cp.wait() if wait else cp.start()

        start = pfill_ref[e]
        end = pend_ref[e]
        head = jnp.minimum((-start) & 7, end - start)
        for r in range(7):
            @pl.when(r < head)
            def _():
                go(zero_scr.at[pl.ds(0, 1)], xs_ref.at[pl.ds(start + r, 1)])
        off = start + head
        rem = end - off
        size = tb // 2
        while size >= 8:
            cond = (rem & size) != 0

            @pl.when(cond)
            def _():
                go(zero_scr.at[pl.ds(0, size)], xs_ref.at[pl.ds(pl.multiple_of(off, 8), size)])
            off = off + jnp.where(cond, size, 0)
            size //= 2

    @pl.when(i == 0)
    def _():
        zero_scr[...] = jnp.zeros(zero_scr.shape, U32)

        def fill(e, carry):
            pad_fill(e, False)
            return carry

        lax.fori_loop(0, N_EXPERTS, fill, 0)

        def fill_tail(j, carry):
            start = pl.multiple_of((nv_ref[0] + j) * tb, tb)
            pltpu.make_async_copy(zero_scr, xs_ref.at[pl.ds(start, tb)], zsem).start()
            return carry

        lax.fori_loop(0, n_tail, fill_tail, 0)

        def drain(e, carry):
            pad_fill(e, True)
            return carry

        lax.fori_loop(0, N_EXPERTS, drain, 0)

        def drain_tail(j, carry):
            pltpu.make_async_copy(zero_scr, xs_ref.at[pl.ds(0, tb)], zsem).wait()
            return carry

        lax.fori_loop(0, n_tail, drain_tail, 0)

    def load(j):
        return pltpu.make_async_copy(h_hbm.at[pl.ds(pl.multiple_of(j * ts, ts), ts)],
                                     h_buf.at[lax.rem(j, H_SLOTS)], lsem.at[lax.rem(j, H_SLOTS)])

    def drain_scatters(j):
        sl = lax.rem(j, H_SLOTS)
        for k in range(TOP_K):
            pltpu.make_async_copy(h_buf.at[sl], xs_ref.at[pl.ds(0, ts)], sems.at[sl]).wait()

    @pl.when(i == 0)
    def _():
        load(0).start()

    @pl.when(i + 1 < n_steps)
    def _():
        load(i + 1).start()

    load(i).wait()
    cur = lax.rem(i, H_SLOTS)

    for t in range(ts):
        for k in range(TOP_K):
            pltpu.make_async_copy(h_buf.at[cur, pl.ds(t, 1)], xs_ref.at[pl.ds(dest_ref[k, t], 1)],
                                  sems.at[cur]).start(priority=k % 2)

    @pl.when(i >= 1)
    def _():
        drain_scatters(i - 1)

    @pl.when(i == n_steps - 1)
    def _():
        drain_scatters(i)


def _scatter(pfill, pend, n_valid, dest, hp, n_rows):
    n, w = hp.shape
    ts = T_ROW
    grid_spec = pltpu.PrefetchScalarGridSpec(
        num_scalar_prefetch=3,
        grid=(n // ts,),
        in_specs=[pl.BlockSpec((TOP_K, ts), lambda i, *_: (0, i), memory_space=pltpu.SMEM),
                  pl.BlockSpec(memory_space=pl.ANY)],
        out_specs=pl.BlockSpec(memory_space=pl.ANY),
        scratch_shapes=[pltpu.VMEM((TB_EXP, w), U32), pltpu.VMEM((H_SLOTS, ts, w), U32),
                        pltpu.SemaphoreType.DMA((H_SLOTS,)), pltpu.SemaphoreType.DMA((H_SLOTS,)),
                        pltpu.SemaphoreType.DMA],
    )
    return pl.pallas_call(
        _scatter_kernel,
        grid_spec=grid_spec,
        out_shape=jax.ShapeDtypeStruct((n_rows, w), U32),
        compiler_params=pltpu.CompilerParams(dimension_semantics=("arbitrary",),
                                             vmem_limit_bytes=VMEM_LIMIT),
        name="scatter",
    )(pfill, pend, n_valid, dest, hp)


def _experts_kernel(be_ref, nv_ref, ge_ref, ng_ref, xs_ref, wg_hbm, wu_hbm, wd_hbm, y_ref,
                    wg_f, wu_f, wd_f, wg_bf, wu_bf, wd_bf, sems, gctr):
    i = pl.program_id(0)
    prev = jnp.maximum(i - 1, 0)
    valid = i < nv_ref[0]
    fresh = valid & ((i == 0) | (be_ref[i] != be_ref[prev]))

    def weight_copies(g, slot):
        e = ge_ref[g]
        return (pltpu.make_async_copy(wg_hbm.at[e], wg_f.at[slot], sems.at[slot, 0]),
                pltpu.make_async_copy(wu_hbm.at[e], wu_f.at[slot], sems.at[slot, 1]),
                pltpu.make_async_copy(wd_hbm.at[e], wd_f.at[slot], sems.at[slot, 2]))

    @pl.when(i == 0)
    def _():
        gctr[0] = 0
        for g in range(W_SLOTS):
            @pl.when(g < ng_ref[0])
            def _():
                for cp in weight_copies(g, g):
                    cp.start()

    @pl.when(fresh)
    def _():
        g = gctr[0]
        slot = lax.rem(g, W_SLOTS)
        for cp in weight_copies(g, slot):
            cp.wait()
        wg_bf[...] = wg_f[slot].astype(BF16)
        wu_bf[...] = wu_f[slot].astype(BF16)
        wd_bf[...] = wd_f[slot].astype(BF16)

        @pl.when(g + W_SLOTS < ng_ref[0])
        def _():
            for cp in weight_copies(g + W_SLOTS, slot):
                cp.start()

        gctr[0] = g + 1

    @pl.when(valid)
    def _():
        lo, hi = _unpack_pair(xs_ref[...])
        x = jnp.concatenate([lo.astype(BF16), hi.astype(BF16)], axis=1)
        hg = jnp.dot(x, wg_bf[...], preferred_element_type=F32)
        hu = jnp.dot(x, wu_bf[...], preferred_element_type=F32)
        a = (_silu(hg) * hu).astype(BF16)
        y = jnp.dot(a, wd_bf[...], preferred_element_type=F32)
        half = y.shape[1] // 2
        y_ref[...] = _pack_pair(y[:, :half], y[:, half:])

    @pl.when(jnp.logical_not(valid))
    def _():
        y_ref[...] = jnp.zeros(y_ref.shape, U32)


def _experts(block_e, n_valid, group_e, n_groups, xs, w_gate, w_up, w_down, n_blocks):
    tb = TB_EXP
    w = xs.shape[1]
    _, d, f = w_gate.shape
    grid_spec = pltpu.PrefetchScalarGridSpec(
        num_scalar_prefetch=4,
        grid=(n_blocks,),
        in_specs=[pl.BlockSpec((tb, w), lambda i, be, nv, ge, ng: (jnp.minimum(i, nv[0] - 1), 0)),
                  pl.BlockSpec(memory_space=pl.ANY),
                  pl.BlockSpec(memory_space=pl.ANY),
                  pl.BlockSpec(memory_space=pl.ANY)],
        out_specs=pl.BlockSpec((tb, w), lambda i, be, nv, ge, ng: (i, 0)),
        scratch_shapes=[pltpu.VMEM((W_SLOTS, d, f), F32), pltpu.VMEM((W_SLOTS, d, f), F32),
                        pltpu.VMEM((W_SLOTS, f, d), F32),
                        pltpu.VMEM((d, f), BF16), pltpu.VMEM((d, f), BF16), pltpu.VMEM((f, d), BF16),
                        pltpu.SemaphoreType.DMA((W_SLOTS, 3)), pltpu.SMEM((1,), I32)],
    )
    return pl.pallas_call(
        _experts_kernel,
        grid_spec=grid_spec,
        out_shape=jax.ShapeDtypeStruct((n_blocks * tb, w), U32),
        compiler_params=pltpu.CompilerParams(dimension_semantics=("arbitrary",),
                                             vmem_limit_bytes=VMEM_LIMIT),
        name="experts",
    )(block_e, n_valid, group_e, n_groups, xs, w_gate, w_up, w_down)


def _combine_kernel(dcur_ref, dnxt_ref, x1_ref, hp_ref, gate_ref, gt_ref,
                    wsg_ref, wsu_ref, wsd_ref, y_ref, o_ref, buf, sems):
    i = pl.program_id(0)
    tc = x1_ref.shape[0]
    slot = lax.rem(i, 2)

    def issue(d_ref, sl):
        for t in range(tc):
            for k in range(TOP_K):
                pltpu.make_async_copy(y_ref.at[pl.ds(d_ref[k, t], 1)], buf.at[sl, k, pl.ds(t, 1)],
                                      sems.at[sl]).start(priority=k % 2)

    def compute():
        lo, hi = _unpack_pair(hp_ref[...])
        x = jnp.concatenate([lo.astype(BF16), hi.astype(BF16)], axis=1)
        hg = jnp.dot(x, wsg_ref[...], preferred_element_type=F32)
        hu = jnp.dot(x, wsu_ref[...], preferred_element_type=F32)
        a = (_silu(hg) * hu).astype(BF16)
        shared = jnp.dot(a, wsd_ref[...], preferred_element_type=F32)
        half = shared.shape[1] // 2
        gate = gate_ref[...]
        r_lo = shared[:, :half]
        r_hi = shared[:, half:]
        for k in range(TOP_K):
            lo, hi = _unpack_pair(buf[slot, k])
            gk = gate[:, k:k + 1]
            r_lo = r_lo + gk * lo
            r_hi = r_hi + gk * hi
        gt = gt_ref[0]
        o_ref[:, :half] = x1_ref[:, :half] + gt[:, :half] * r_lo
        o_ref[:, half:] = x1_ref[:, half:] + gt[:, half:] * r_hi

    @pl.when(i == 0)
    def _():
        issue(dcur_ref, 0)

    for k in range(TOP_K):
        pltpu.make_async_copy(y_ref.at[pl.ds(0, tc)], buf.at[slot, k], sems.at[slot]).wait()

    @pl.when(i + 1 < pl.num_programs(0))
    def _():
        issue(dnxt_ref, 1 - slot)
        compute()

    @pl.when(i + 1 >= pl.num_programs(0))
    def _():
        compute()


def _combine(dest, x1, hp, gate_t, gt2, wsg, wsu, wsd, y, seq):
    n, d = x1.shape
    tc = T_ROW
    w = hp.shape[1]
    f = wsg.shape[1]
    tiles_per_batch = seq // tc
    last = n // tc - 1
    return pl.pallas_call(
        _combine_kernel,
        grid=(n // tc,),
        in_specs=[pl.BlockSpec((TOP_K, tc), lambda i: (0, i), memory_space=pltpu.SMEM),
                  pl.BlockSpec((TOP_K, tc), lambda i: (0, jnp.minimum(i + 1, last)),
                               memory_space=pltpu.SMEM),
                  pl.BlockSpec((tc, d), lambda i: (i, 0)),
                  pl.BlockSpec((tc, w), lambda i: (i, 0)),
                  pl.BlockSpec((tc, TOP_K), lambda i: (i, 0)),
                  pl.BlockSpec((1, 1, d), lambda i: (i // tiles_per_batch, 0, 0)),
                  pl.BlockSpec((d, f), lambda i: (0, 0)),
                  pl.BlockSpec((d, f), lambda i: (0, 0)),
                  pl.BlockSpec((f, d), lambda i: (0, 0)),
                  pl.BlockSpec(memory_space=pl.ANY)],
        out_specs=pl.BlockSpec((tc, d), lambda i: (i, 0)),
        scratch_shapes=[pltpu.VMEM((2, TOP_K, tc, w), U32), pltpu.SemaphoreType.DMA((2,))],
        out_shape=jax.ShapeDtypeStruct((n, d), F32),
        compiler_params=pltpu.CompilerParams(dimension_semantics=("arbitrary",),
                                             vmem_limit_bytes=VMEM_LIMIT),
        name="combine",
    )(dest, dest, x1, hp, gate_t, gt2, wsg, wsu, wsd, y)


def kernel(x, c, positions, rel_bias, hgrn_lb_logits, w_ada, b_ada, g_mix, w_in, g_q, g_k, lam_q1, lam_k1, lam_q2, lam_k2, g_sub, g_hgrn, w_out, g_ffn, w_router, router_bias, w_exp_gate, w_exp_up, w_exp_down, w_sh_gate, w_sh_up, w_sh_down):
    batch, seq, d = x.shape
    n = batch * seq
    layer = 0
    x2d = x.reshape(n, d)

    c_pad = jnp.zeros((8, d), F32).at[:batch].set(c.astype(F32))
    mod = _ada(c_pad, w_ada[layer], b_ada[layer][None, :])[:batch]
    sh1, sc1, gt1, sh2, sc2, gt2 = [m.reshape(batch, 1, d) for m in jnp.split(mod, 6, axis=-1)]

    lbs = jnp.cumsum(jax.nn.softmax(hgrn_lb_logits.astype(F32), axis=1), axis=1)[:, layer]
    lbs = lbs.reshape(2, 1, SEG)
    reps = SEG // QK_DIM
    qk_gain = jnp.stack([jnp.tile(g_q[layer].astype(F32), reps) * (QK_DIM ** -0.5 * LOG2E),
                         jnp.tile(g_k[layer].astype(F32), reps)]).reshape(2, 1, SEG)
    lane = jnp.arange(LANES)
    g64 = jnp.where((lane[:, None] // QK_DIM) == (lane[None, :] // QK_DIM), 1.0 / QK_DIM, 0.0).astype(BF16)
    lam = (jnp.exp(jnp.sum(lam_q1[layer].astype(F32) * lam_k1[layer].astype(F32)))
           - jnp.exp(jnp.sum(lam_q2[layer].astype(F32) * lam_k2[layer].astype(F32)))
           + LAM_INIT).reshape(1)

    p, lf = _inproj(x2d, sc1, sh1, g_mix[layer][None, :], w_in[layer].astype(BF16), qk_gain, lbs, g64, seq)

    nt = seq // T_ATT
    pos_sub = positions.astype(I32).reshape(batch * seq // T_SUB, T_SUB)
    smin = jnp.min(pos_sub, axis=1)
    smax = jnp.max(pos_sub, axis=1)
    posq = positions.astype(I32).reshape(batch, nt, 1, T_ATT)
    posk = positions.astype(I32).reshape(batch, seq, 1)
    rb_t = rel_bias.astype(F32).T * LOG2E
    rb_tab = jnp.zeros((N_HEADS, LANES), F32).at[:, :REL_BUCKETS].set(rb_t)
    oa = _attention(p, smin, smax, posq, posk, rb_tab, rb_t, lam,
                    g_sub[layer][:, None].astype(F32), batch, seq)

    ci = jnp.arange(C_HGRN)
    tril = (ci[None, :] <= ci[:, None]).astype(BF16)
    triu = (ci[None, :] >= ci[:, None]).astype(BF16)
    oh = _hgrn(p, lf, tril, triu, g_hgrn[layer][None, :].astype(F32), batch, seq)

    half = d // 2
    w_out_bf = w_out[layer].astype(BF16)
    wr_t = w_router[layer].astype(F32).T
    wr_hi = wr_t.astype(BF16)
    wr_lo = (wr_t - wr_hi.astype(F32)).astype(BF16)
    ti = jnp.arange(TM_OUT)
    upper = (ti[:, None] < ti[None, :]).astype(BF16)
    ones = jnp.ones((TM_OUT, LANES), BF16)
    x1, hp, eidx, slot, gate, cnt = _outproj(
        oa, oh, w_out_bf, x2d, gt1, g_ffn[layer][None, :], sc2, sh2,
        wr_hi, wr_lo, router_bias[layer].astype(F32)[:, None], upper, ones, seq)

    tb = TB_EXP
    counts = cnt[:, 0].astype(I32)
    padded = (counts + tb - 1) // tb * tb
    pends = jnp.cumsum(padded)
    pstart = (pends - padded).astype(I32)
    n_blocks = (n * TOP_K) // tb + N_EXPERTS
    n_valid = (pends[-1] // tb).astype(I32).reshape(1)
    blk_start = jnp.arange(n_blocks, dtype=I32) * tb
    block_e = jnp.minimum(jnp.sum(pends[None, :] <= blk_start[:, None], axis=1), N_EXPERTS - 1).astype(I32)
    pfill = (pstart + counts).astype(I32)
    pend = pends.astype(I32)
    dest = _dest(pstart, eidx, slot)
    has_rows = counts > 0
    group_e = jnp.nonzero(has_rows, size=N_EXPERTS, fill_value=0)[0].astype(I32)
    n_groups = jnp.sum(has_rows).astype(I32).reshape(1)

    xs = _scatter(pfill, pend, n_valid, dest, hp, n_blocks * tb)
    y = _experts(block_e, n_valid, group_e, n_groups, xs,
                 w_exp_gate[layer], w_exp_up[layer], w_exp_down[layer], n_blocks)
    out = _combine(dest, x1, hp, gate.T, gt2,
                   w_sh_gate[layer].astype(BF16), w_sh_up[layer].astype(BF16),
                   w_sh_down[layer].astype(BF16), y, seq)
    return out.reshape(batch, seq, d)
```

```python
import functools
import math

import jax
import jax.numpy as jnp
from jax import lax
from jax.experimental import pallas as pl
from jax.experimental.pallas import tpu as pltpu

F32 = jnp.float32
BF16 = jnp.bfloat16
I32 = jnp.int32
U32 = jnp.uint32

D_MODEL = 2048
N_HEADS = 8
QK_DIM = 64
HEAD_DIM = 128
SEG = 1024
N_SEG = 8
REL_BUCKETS = 32
REL_MAX_DIST = 128
N_EXPERTS = 256
TOP_K = 8
N_GROUPS = 8
TOPK_GROUPS = 4
GROUP_SIZE = N_EXPERTS // N_GROUPS
EXPERT_DIM = 512
ROUTED_SCALE = 2.5
EPS = 1e-6
LAM_INIT = 0.8 - 0.6 * math.exp(-0.3 * 0)
LOG2E = math.log2(math.e)

LANES = 128
VMEM_LIMIT = 56 * 1024 * 1024

TM_IN = 512
IN_CHUNK = 256
T_ATT = 512
T_KEY = 1024
T_SUB = 128
V_PAD = 16
C_HGRN = 64
HGRN_GROUP = 16
TM_OUT = 256
T_ROW = 256
TB_EXP = 256
W_SLOTS = 3
W_AHEAD = 2
H_SLOTS = 3
NEG_BIG = -1e30
EXP_CLAMP = 80.0
FAST_BOUND = 60.0
BOUND_SLACK = 1.02
BOUND_PAD = 0.01

NT_DIMS = (((1,), (1,)), ((), ()))


def _silu(x):
    return x * jax.nn.sigmoid(x)


def _pack_pair(lo_f32, hi_f32):
    lo = lax.bitcast_convert_type(lo_f32.astype(BF16).astype(F32), U32)
    hi = lax.bitcast_convert_type(hi_f32.astype(BF16).astype(F32), U32)
    return (hi & jnp.uint32(0xFFFF0000)) | (lo >> 16)


def _unpack_pair(word):
    lo = lax.bitcast_convert_type(word << 16, F32)
    hi = lax.bitcast_convert_type(word & jnp.uint32(0xFFFF0000), F32)
    return lo, hi


def _ada_kernel(c_ref, w_ref, b_ref, o_ref):
    a = _silu(c_ref[...]).astype(BF16)
    o_ref[...] = jnp.dot(a, w_ref[...].astype(BF16), preferred_element_type=F32) + b_ref[...]


def _ada(c_pad, w, b):
    d, n = w.shape
    tn = 1024
    return pl.pallas_call(
        _ada_kernel,
        grid=(n // tn,),
        in_specs=[pl.BlockSpec((8, d), lambda j: (0, 0)),
                  pl.BlockSpec((d, tn), lambda j: (0, j)),
                  pl.BlockSpec((1, tn), lambda j: (0, j))],
        out_specs=pl.BlockSpec((8, tn), lambda j: (0, j)),
        out_shape=jax.ShapeDtypeStruct((8, n), F32),
        compiler_params=pltpu.CompilerParams(dimension_semantics=("arbitrary",),
                                             vmem_limit_bytes=VMEM_LIMIT),
        name="ada",
    )(c_pad, w, b)


def _inproj_kernel(x_ref, sc_ref, sh_ref, gmix_ref, w_ref, qkg_ref, lb_ref, g64_ref,
                   p_ref, lf_ref, h_scr):
    j = pl.program_id(1)

    @pl.when(j == 0)
    def _():
        x = x_ref[...]
        ms = jnp.mean(x * x, axis=-1, keepdims=True)
        y = x * lax.rsqrt(ms + EPS) * gmix_ref[...]
        h_scr[...] = (y * (1.0 + sc_ref[0]) + sh_ref[0]).astype(BF16)

    def chunks(epilogue):
        for c in range(SEG // IN_CHUNK):
            sl = slice(c * IN_CHUNK, (c + 1) * IN_CHUNK)
            epilogue(sl, jnp.dot(h_scr[...], w_ref[:, sl], preferred_element_type=F32))

    @pl.when(j < 2)
    def _():
        acc = jnp.dot(h_scr[...], w_ref[...], preferred_element_type=F32)
        gain = qkg_ref[0]
        for c in range(SEG // LANES):
            sl = slice(c * LANES, (c + 1) * LANES)
            xs = acc[:, sl]
            ms = jnp.dot((xs * xs).astype(BF16), g64_ref[...], preferred_element_type=F32)
            p_ref[:, sl] = (xs * lax.rsqrt(ms + EPS) * gain[:, sl]).astype(BF16)

    @pl.when((j == 2) | (j == 4))
    def _():
        def plain(sl, acc):
            p_ref[:, sl] = acc.astype(BF16)

        chunks(plain)

    @pl.when((j == 3) | (j == 7))
    def _():
        def silu(sl, acc):
            p_ref[:, sl] = _silu(acc).astype(BF16)

        chunks(silu)

    @pl.when((j == 5) | (j == 6))
    def _():
        def log_gate(sl, z):
            lb = lb_ref[0, :, sl]
            f = lb + (1.0 - lb) * jax.nn.sigmoid(z)
            lf_ref[:, sl] = jnp.log(f)
            p_ref[:, sl] = z.astype(BF16)

        chunks(log_gate)


def _inproj(x2d, sc1, sh1, g_mix, w_in_bf, qk_gain, lbs, g64, seq):
    n, d = x2d.shape
    tm = TM_IN
    tiles_per_batch = seq // tm
    return pl.pallas_call(
        _inproj_kernel,
        grid=(n // tm, N_SEG),
        in_specs=[
            pl.BlockSpec((tm, d), lambda i, j: (i, 0)),
            pl.BlockSpec((1, 1, d), lambda i, j: (i // tiles_per_batch, 0, 0)),
            pl.BlockSpec((1, 1, d), lambda i, j: (i // tiles_per_batch, 0, 0)),
            pl.BlockSpec((1, d), lambda i, j: (0, 0)),
            pl.BlockSpec((d, SEG), lambda i, j: (0, j)),
            pl.BlockSpec((1, 1, SEG), lambda i, j: (jnp.minimum(j, 1), 0, 0)),
            pl.BlockSpec((1, 1, SEG), lambda i, j: (jnp.clip(j - 5, 0, 1), 0, 0)),
            pl.BlockSpec((LANES, LANES), lambda i, j: (0, 0)),
        ],
        out_specs=[
            pl.BlockSpec((tm, SEG), lambda i, j: (i, j)),
            pl.BlockSpec((tm, SEG), lambda i, j: (i, jnp.clip(j - 5, 0, 1))),
        ],
        out_shape=[jax.ShapeDtypeStruct((n, N_SEG * SEG), BF16),
                   jax.ShapeDtypeStruct((n, 2 * SEG), F32)],
        scratch_shapes=[pltpu.VMEM((tm, d), BF16)],
        compiler_params=pltpu.CompilerParams(dimension_semantics=("arbitrary", "arbitrary"),
                                             vmem_limit_bytes=VMEM_LIMIT),
        name="inproj",
    )(x2d, sc1, sh1, g_mix, w_in_bf, qk_gain, lbs, g64)


def _t5_bias_tile(pos_q, pos_k, table):
    half = REL_BUCKETS // 2
    max_exact = half // 2
    rel = pos_k - pos_q
    n = jnp.abs(rel)
    nf = jnp.maximum(n, 1).astype(F32)
    large = max_exact + (jnp.log(nf / max_exact) / math.log(REL_MAX_DIST / max_exact)
                         * (half - max_exact)).astype(I32)
    large = jnp.minimum(large, half - 1)
    bucket = jnp.where(rel > 0, half, 0) + jnp.where(n < max_exact, n, large)
    rows = bucket.shape[0]
    tbl = jnp.broadcast_to(table, (rows, LANES))
    cols = [jnp.take_along_axis(tbl, bucket[:, c * LANES:(c + 1) * LANES], axis=1)
            for c in range(bucket.shape[1] // LANES)]
    return jnp.concatenate(cols, axis=1)


def _attn_kernel(smin_ref, smax_ref, q_ref, k_ref, v_ref, posq_ref, posk_ref, rbt_ref, rbx_ref, rb_ref, lam_ref,
                 gsub_ref, o_ref, vt_scr, s_a, s_b, cm_a, cm_b, p_a, p_b, al_a, al_b, m_scr, a_scr,
                 kmax_scr):
    b = pl.program_id(0)
    h = pl.program_id(1)
    i = pl.program_id(2)
    tq = T_ATT
    tk = T_KEY
    n_sub = tk // T_SUB
    nq_sub = tq // T_SUB
    ntk = k_ref.shape[0] // tk
    n_pairs = ntk // 2
    subs_per_batch = k_ref.shape[0] // T_SUB

    @pl.when(i == 0)
    def _():
        ones_row = jnp.where(lax.broadcasted_iota(I32, (V_PAD, tk), 0) == 0, 1.0, 0.0).astype(BF16)
        kmax_scr[...] = jnp.zeros(kmax_scr.shape, F32)

        def tr(c, carry):
            r0 = pl.multiple_of(c * tk, tk)
            vt_scr[c, :HEAD_DIM, :] = v_ref[pl.ds(r0, tk), :].astype(F32).T.astype(BF16)
            vt_scr[c, HEAD_DIM:, :] = ones_row
            kf = k_ref[pl.ds(r0, tk), :].astype(F32)
            for mp in range(2):
                km = kf[:, mp * QK_DIM:(mp + 1) * QK_DIM]
                nk = jnp.dot(km * km, jnp.ones((QK_DIM, LANES), F32), preferred_element_type=F32)
                kmax_scr[mp] = jnp.maximum(kmax_scr[mp], jnp.max(nk, axis=0, keepdims=True))
            return carry

        lax.fori_loop(0, ntk, tr, 0)

    q = q_ref[...]
    qs = (q[:, :QK_DIM], q[:, QK_DIM:])
    m_scr[...] = jnp.full(m_scr.shape, NEG_BIG, F32)
    a_scr[...] = jnp.zeros(a_scr.shape, F32)

    sub0 = b * subs_per_batch + i * nq_sub
    q_lo = smin_ref[sub0]
    q_hi = smax_ref[sub0]
    for u in range(1, nq_sub):
        q_lo = jnp.minimum(q_lo, smin_ref[sub0 + u])
        q_hi = jnp.maximum(q_hi, smax_ref[sub0 + u])
    c_pos = rb_ref[h, REL_BUCKETS - 1]
    c_neg = rb_ref[h, REL_BUCKETS // 2 - 1]
    pos_q = posq_ref[0, 0]

    def classify(j, u):
        ksub = b * subs_per_batch + j * n_sub + u
        lo = smin_ref[ksub] - q_hi
        hi = smax_ref[ksub] - q_lo
        far = (lo >= REL_MAX_DIST) | (hi <= -REL_MAX_DIST)
        shift = jnp.where(lo >= REL_MAX_DIST, c_pos, jnp.where(hi <= -REL_MAX_DIST, c_neg, 0.0))
        return far, shift

    def scores(j, s_ref, cm_ref):
        kk = k_ref[pl.ds(pl.multiple_of(j * tk, tk), tk), :]
        ks = (kk[:, :QK_DIM], kk[:, QK_DIM:])
        for mp in range(2):
            s = lax.dot_general(ks[mp], qs[mp], NT_DIMS, preferred_element_type=F32)
            s_ref[mp] = s
            for u in range(n_sub):
                _, shift = classify(j, u)
                cm_ref[mp, u] = jnp.max(s[u * T_SUB:(u + 1) * T_SUB], axis=0, keepdims=True) + shift

    def fixup(j, s_ref, cm_ref):
        for u in range(n_sub):
            far, _ = classify(j, u)

            @pl.when(jnp.logical_not(far))
            def _():
                rows = pl.ds(u * T_SUB, T_SUB)
                pos_k = posk_ref[0, pl.ds(pl.multiple_of(j * tk + u * T_SUB, T_SUB), T_SUB), :]
                bias = _t5_bias_tile(pos_q, pos_k, rbt_ref[pl.ds(h, 1), :])
                for mp in range(2):
                    sb = s_ref[mp, rows, :] + bias
                    s_ref[mp, rows, :] = sb
                    cm_ref[mp, u] = jnp.max(sb, axis=0, keepdims=True)

    def soft(j, s_ref, cm_ref, p_ref, al_ref):
        for mp in range(2):
            m_old = m_scr[mp]
            m_new = m_old
            for u in range(n_sub):
                m_new = jnp.maximum(m_new, cm_ref[mp, u])
            for u in range(n_sub):
                rows = pl.ds(u * T_SUB, T_SUB)
                _, shift = classify(j, u)
                p_ref[mp, rows, :] = jnp.exp2(s_ref[mp, rows, :] - (m_new - shift)).astype(BF16)
            m_scr[mp] = m_new
            al_ref[mp] = jnp.exp2(m_old - m_new)

    def pv(j, p_ref, al_ref):
        vt = vt_scr[j]
        for mp in range(2):
            a_scr[mp] = al_ref[mp] * a_scr[mp] + jnp.dot(vt, p_ref[mp], preferred_element_type=F32)

    sbuf = ((s_a, cm_a), (s_b, cm_b))
    pbuf = ((p_a, al_a), (p_b, al_b))
    last = 2 * (n_pairs - 1)

    def online_path():
        def step(j, par):
            scores(j + 2, *sbuf[par])
            soft(j + 1, *sbuf[1 - par], *pbuf[1 - par])
            pv(j, *pbuf[par])
            fixup(j + 2, *sbuf[par])

        scores(0, *sbuf[0])
        fixup(0, *sbuf[0])
        scores(1, *sbuf[1])
        soft(0, *sbuf[0], *pbuf[0])
        fixup(1, *sbuf[1])

        def body(jj, carry):
            step(2 * jj, 0)
            step(2 * jj + 1, 1)
            return carry

        lax.fori_loop(0, n_pairs - 1, body, 0)
        soft(last + 1, *sbuf[1], *pbuf[1])
        pv(last, *pbuf[0])
        pv(last + 1, *pbuf[1])

    def col_bound(mp):
        qf = qs[mp].astype(F32)
        nq = lax.dot_general(jnp.ones((8, QK_DIM), F32), qf * qf, NT_DIMS,
                             preferred_element_type=F32)[0:1, :]
        return jnp.sqrt(nq * kmax_scr[mp, :, 0:1]) * BOUND_SLACK

    b_max = rb_ref[h, 0]
    for e in range(1, REL_BUCKETS):
        b_max = jnp.maximum(b_max, rb_ref[h, e])
    bounds = [col_bound(mp) + (b_max + BOUND_PAD) for mp in range(2)]
    bound_max = jnp.max(jnp.maximum(bounds[0], bounds[1]))

    def fast_scores(j, p_ref):
        kk = k_ref[pl.ds(pl.multiple_of(j * tk, tk), tk), :]
        ks = (kk[:, :QK_DIM], kk[:, QK_DIM:])
        for mp in range(2):
            s = lax.dot_general(ks[mp], qs[mp], NT_DIMS, preferred_element_type=F32)
            for u in range(n_sub):
                _, shift = classify(j, u)
                p_ref[mp, pl.ds(u * T_SUB, T_SUB), :] = jnp.exp2(
                    s[u * T_SUB:(u + 1) * T_SUB] - (bounds[mp] - shift)).astype(BF16)

    def fast_fixup(j, p_ref):
        fars = [classify(j, u)[0] for u in range(n_sub)]
        all_far = fars[0]
        for f in fars[1:]:
            all_far = all_far & f

        @pl.when(jnp.logical_not(all_far))
        def _():
            for u in range(n_sub):
                @pl.when(jnp.logical_not(fars[u]))
                def _():
                    rows = pl.ds(u * T_SUB, T_SUB)
                    pos_k = posk_ref[0, pl.ds(pl.multiple_of(j * tk + u * T_SUB, T_SUB), T_SUB), :]
                    scale = _t5_bias_tile(pos_q, pos_k, rbx_ref[pl.ds(h, 1), :]).astype(BF16)
                    for mp in range(2):
                        p_ref[mp, rows, :] = p_ref[mp, rows, :] * scale

    def fast_pv(j, p_ref):
        vt = vt_scr[j]
        for mp in range(2):
            a_scr[mp] = a_scr[mp] + jnp.dot(vt, p_ref[mp], preferred_element_type=F32)

    def fast_path():
        pb = (p_a, p_b)

        def step(j, par):
            fast_scores(j + 1, pb[1 - par])
            fast_pv(j, pb[par])
            fast_fixup(j + 1, pb[1 - par])

        fast_scores(0, pb[0])
        fast_fixup(0, pb[0])

        def body(jj, carry):
            step(2 * jj, 0)
            step(2 * jj + 1, 1)
            return carry

        lax.fori_loop(0, n_pairs - 1, body, 0)
        step(last, 0)
        fast_pv(last + 1, pb[1])

    use_fast = bound_max <= FAST_BOUND

    @pl.when(use_fast)
    def _():
        fast_path()

    @pl.when(jnp.logical_not(use_fast))
    def _():
        online_path()

    num = [a_scr[mp, :HEAD_DIM, :] / a_scr[mp, HEAD_DIM:HEAD_DIM + 1, :] for mp in range(2)]
    o = num[0] - lam_ref[0] * num[1]
    ms = jnp.mean(o * o, axis=0, keepdims=True)
    o = o * lax.rsqrt(ms + EPS) * (gsub_ref[...] * (1.0 - LAM_INIT))
    o_ref[...] = o.T.astype(BF16)


def _attention(p, smin, smax, posq, posk, rb_tab, rb_t, lam, g_sub, batch, seq):
    t = T_ATT
    tk = T_KEY
    nt = seq // t
    ntk = seq // tk
    assert seq % (2 * tk) == 0
    n = batch * seq
    n_sub = tk // T_SUB
    va = HEAD_DIM + V_PAD
    grid_spec = pltpu.PrefetchScalarGridSpec(
        num_scalar_prefetch=2,
        grid=(batch, N_HEADS, nt),
        in_specs=[
            pl.BlockSpec((t, HEAD_DIM), lambda b, h, i, *_: (b * nt + i, h)),
            pl.BlockSpec((seq, HEAD_DIM), lambda b, h, i, *_: (b, N_HEADS + h)),
            pl.BlockSpec((seq, HEAD_DIM), lambda b, h, i, *_: (b, 2 * N_HEADS + h)),
            pl.BlockSpec((1, 1, 1, t), lambda b, h, i, *_: (b, i, 0, 0)),
            pl.BlockSpec((1, seq, 1), lambda b, h, i, *_: (b, 0, 0)),
            pl.BlockSpec((N_HEADS, LANES), lambda b, h, i, *_: (0, 0)),
            pl.BlockSpec((N_HEADS, LANES), lambda b, h, i, *_: (0, 0)),
            pl.BlockSpec(memory_space=pltpu.SMEM),
            pl.BlockSpec(memory_space=pltpu.SMEM),
            pl.BlockSpec((HEAD_DIM, 1), lambda b, h, i, *_: (0, 0)),
        ],
        out_specs=pl.BlockSpec((t, HEAD_DIM), lambda b, h, i, *_: (b * nt + i, h)),
        scratch_shapes=[pltpu.VMEM((ntk, va, tk), BF16),
                        pltpu.VMEM((2, tk, t), F32),
                        pltpu.VMEM((2, tk, t), F32),
                        pltpu.VMEM((2, n_sub, 1, t), F32),
                        pltpu.VMEM((2, n_sub, 1, t), F32),
                        pltpu.VMEM((2, tk, t), BF16),
                        pltpu.VMEM((2, tk, t), BF16),
                        pltpu.VMEM((2, 1, t), F32),
                        pltpu.VMEM((2, 1, t), F32),
                        pltpu.VMEM((2, 1, t), F32),
                        pltpu.VMEM((2, va, t), F32),
                        pltpu.VMEM((2, 1, LANES), F32)],
    )
    return pl.pallas_call(
        _attn_kernel,
        grid_spec=grid_spec,
        out_shape=jax.ShapeDtypeStruct((n, N_HEADS * HEAD_DIM), BF16),
        compiler_params=pltpu.CompilerParams(
            dimension_semantics=("arbitrary", "arbitrary", "arbitrary"),
            vmem_limit_bytes=VMEM_LIMIT),
        name="attn",
    )(smin, smax, p, p, p, posq, posk, rb_tab, jnp.exp2(rb_tab), rb_t, lam, g_sub)


def _hgrn_group(r0, forward, q_ref, v_ref, lf_ref, tri_ref, st):
    c = C_HGRN
    order = range(HGRN_GROUP) if forward else range(HGRN_GROUP - 1, -1, -1)
    rows = [pl.ds(r0 + k * c, c) for k in order]
    tri = tri_ref[...]
    row = lax.broadcasted_iota(I32, (c, c), 0)
    col = lax.broadcasted_iota(I32, (c, c), 1)
    keep = (col <= row) if forward else (col >= row)

    gs = [lf_ref[r, :] for r in rows]
    bsums = []
    for g in gs:
        g_hi = g.astype(BF16)
        g_lo = (g - g_hi.astype(F32)).astype(BF16)
        bsums.append(jnp.dot(tri, g_hi, preferred_element_type=F32)
                     + jnp.dot(tri, g_lo, preferred_element_type=F32))

    q_in, q_t, k_t, k_st, v_t, vs, decay = [], [], [], [], [], [], []
    for r, g, bsum in zip(rows, gs, bsums):
        if forward:
            ref = bsum[c // 2 - 1:c // 2, :]
            b_end = bsum[c - 1:c, :]
        else:
            ref = bsum[c // 2:c // 2 + 1, :]
            b_end = bsum[0:1, :]
        q = q_ref[r, :].astype(F32)
        v = v_ref[r, :]
        kf = 1.0 - jnp.exp(g)
        q_in.append((q * jnp.exp(bsum)).astype(BF16))
        q_t.append((q * jnp.exp(jnp.minimum(bsum - ref, EXP_CLAMP))).astype(BF16))
        k_t.append((kf * jnp.exp(jnp.minimum(ref - bsum, EXP_CLAMP))).astype(BF16))
        k_st.append((kf * jnp.exp(b_end - bsum)).astype(BF16))
        v_t.append(v.astype(F32).T.astype(BF16))
        vs.append(v)
        decay.append(jnp.exp(b_end))

    scores = [lax.dot_general(a, b, NT_DIMS, preferred_element_type=F32) for a, b in zip(q_t, k_t)]
    st_add = [jnp.dot(a, b, preferred_element_type=F32) for a, b in zip(v_t, k_st)]
    intra = [jnp.dot(jnp.where(keep, s, 0.0).astype(BF16), v, preferred_element_type=F32)
             for s, v in zip(scores, vs)]

    outs = []
    for k in range(HGRN_GROUP):
        o = lax.dot_general(q_in[k], st.astype(BF16), NT_DIMS, preferred_element_type=F32) + intra[k]
        st = st * decay[k] + st_add[k]
        outs.append((rows[k], o))
    return outs, st


def _hgrn_kernel(q_ref, v_ref, g_ref, lff_ref, lfb_ref, tril_ref, triu_ref, gh_ref, o_ref, of_scr):
    rows_per_group = C_HGRN * HGRN_GROUP
    n_groups = q_ref.shape[0] // rows_per_group
    st0 = jnp.zeros((HEAD_DIM, HEAD_DIM), F32)

    def fwd(gi, st):
        r0 = pl.multiple_of(gi * rows_per_group, rows_per_group)
        outs, st = _hgrn_group(r0, True, q_ref, v_ref, lff_ref, tril_ref, st)
        for r, o in outs:
            of_scr[r, :] = o
        return st

    lax.fori_loop(0, n_groups, fwd, st0)

    def bwd(gi, st):
        r0 = pl.multiple_of((n_groups - 1 - gi) * rows_per_group, rows_per_group)
        outs, st = _hgrn_group(r0, False, q_ref, v_ref, lfb_ref, triu_ref, st)
        for r, o in outs:
            o = of_scr[r, :] + o
            ms = jnp.mean(o * o, axis=-1, keepdims=True)
            y = o * lax.rsqrt(ms + EPS) * gh_ref[...] * g_ref[r, :].astype(F32)
            o_ref[r, :] = y.astype(BF16)
        return st

    lax.fori_loop(0, n_groups, bwd, st0)


def _hgrn(p, lf, tril, triu, g_hgrn, batch, seq):
    n = batch * seq
    c = C_HGRN
    blk = lambda off: pl.BlockSpec((seq, HEAD_DIM), lambda b, h: (b, off + h))
    return pl.pallas_call(
        _hgrn_kernel,
        grid=(batch, N_HEADS),
        in_specs=[blk(3 * N_HEADS), blk(4 * N_HEADS), blk(7 * N_HEADS), blk(0), blk(N_HEADS),
                  pl.BlockSpec((c, c), lambda b, h: (0, 0)),
                  pl.BlockSpec((c, c), lambda b, h: (0, 0)),
                  pl.BlockSpec((1, HEAD_DIM), lambda b, h: (0, 0))],
        out_specs=pl.BlockSpec((seq, HEAD_DIM), lambda b, h: (b, h)),
        out_shape=jax.ShapeDtypeStruct((n, N_HEADS * HEAD_DIM), BF16),
        scratch_shapes=[pltpu.VMEM((seq, HEAD_DIM), F32)],
        compiler_params=pltpu.CompilerParams(dimension_semantics=("arbitrary", "arbitrary"),
                                             vmem_limit_bytes=VMEM_LIMIT),
        name="hgrn",
    )(p, p, p, lf, lf, tril, triu, g_hgrn)


def _col_max(x):
    return jnp.max(x, axis=0, keepdims=True)


def _outproj_kernel(oa_ref, oh_ref, w_ref, x_ref, gt_ref, gffn_ref, sc_ref, sh_ref,
                    wr_hi_ref, wr_lo_ref, rbias_ref, upper_ref, ones_ref,
                    x1_ref, hp_ref, eidx_ref, slot_ref, gate_ref, cnt_ref, cnt_scr):
    i = pl.program_id(0)
    tm = x_ref.shape[0]

    @pl.when(i == 0)
    def _():
        cnt_scr[...] = jnp.zeros(cnt_scr.shape, F32)

    mixed = jnp.concatenate([oa_ref[...], oh_ref[...]], axis=1)
    acc = jnp.dot(mixed, w_ref[...], preferred_element_type=F32)
    x1 = x_ref[...] + gt_ref[0] * acc
    x1_ref[...] = x1
    ms = jnp.mean(x1 * x1, axis=-1, keepdims=True)
    h2 = x1 * lax.rsqrt(ms + EPS) * gffn_ref[...] * (1.0 + sc_ref[0]) + sh_ref[0]
    half = h2.shape[1] // 2
    hp_ref[...] = _pack_pair(h2[:, :half], h2[:, half:])

    h_hi = h2.astype(BF16)
    h_lo = (h2 - h_hi.astype(F32)).astype(BF16)
    wr_hi = wr_hi_ref[...]
    logits = (lax.dot_general(wr_hi, h_hi, NT_DIMS, preferred_element_type=F32)
              + lax.dot_general(wr_hi, h_lo, NT_DIMS, preferred_element_type=F32)
              + lax.dot_general(wr_lo_ref[...], h_hi, NT_DIMS, preferred_element_type=F32))
    scores = jax.nn.sigmoid(logits)
    biased = scores + rbias_ref[...]

    gs = []
    for g in range(N_GROUPS):
        blk = biased[g * GROUP_SIZE:(g + 1) * GROUP_SIZE, :]
        top1 = _col_max(blk)
        eq = blk == top1
        n_eq = jnp.sum(eq.astype(F32), axis=0, keepdims=True)
        second = _col_max(jnp.where(eq, -jnp.inf, blk))
        gs.append(top1 + jnp.where(n_eq > 1.0, top1, second))
    gsm = jnp.concatenate(gs, axis=0)
    giota = lax.broadcasted_iota(I32, gsm.shape, 0)
    gsel = jnp.zeros(gsm.shape, F32)
    for _ in range(TOPK_GROUPS):
        top = _col_max(gsm)
        idx = jnp.min(jnp.where(gsm == top, giota, N_GROUPS), axis=0, keepdims=True)
        pick = giota == idx
        gsel = jnp.where(pick, 1.0, gsel)
        gsm = jnp.where(pick, -jnp.inf, gsm)
    emask = jnp.concatenate(
        [jnp.broadcast_to(gsel[g:g + 1, :], (GROUP_SIZE, tm)) for g in range(N_GROUPS)], axis=0)
    masked = jnp.where(emask > 0.5, biased, -jnp.inf)

    eiota = lax.broadcasted_iota(I32, masked.shape, 0)
    idxs, gates = [], []
    for _ in range(TOP_K):
        top = _col_max(masked)
        idx = jnp.min(jnp.where(masked == top, eiota, N_EXPERTS), axis=0, keepdims=True)
        pick = eiota == idx
        gates.append(jnp.sum(jnp.where(pick, scores, 0.0), axis=0, keepdims=True))
        idxs.append(idx)
        masked = jnp.where(pick, -jnp.inf, masked)
    gate = jnp.concatenate(gates, axis=0)
    gate = gate / jnp.sum(gate, axis=0, keepdims=True) * ROUTED_SCALE
    eidx = jnp.concatenate(idxs, axis=0)
    eidx_ref[...] = eidx
    gate_ref[...] = gate

    sel = jnp.zeros(masked.shape, F32)
    for k in range(TOP_K):
        sel = jnp.where(eiota == idxs[k], 1.0, sel)
    sel_bf = sel.astype(BF16)
    rank = jnp.dot(sel_bf, upper_ref[...], preferred_element_type=F32)
    base = cnt_scr[...]
    posn = base[:, :1] + rank
    slots = [jnp.sum(jnp.where(eiota == idxs[k], posn, 0.0), axis=0, keepdims=True)
             for k in range(TOP_K)]
    slot_ref[...] = jnp.concatenate(slots, axis=0).astype(I32)
    new_cnt = base + jnp.dot(sel_bf, ones_ref[...], preferred_element_type=F32)
    cnt_scr[...] = new_cnt
    cnt_ref[...] = new_cnt


def _outproj(oa, oh, w_out, x2d, gt1, g_ffn, sc2, sh2, wr_hi, wr_lo, rbias, upper, ones, seq):
    n, d = x2d.shape
    tm = TM_OUT
    tiles_per_batch = seq // tm
    half = d // 2
    row = lambda w: pl.BlockSpec((tm, w), lambda i: (i, 0))
    const = lambda shape: pl.BlockSpec(shape, lambda i: tuple(0 for _ in shape))
    per_batch = pl.BlockSpec((1, 1, d), lambda i: (i // tiles_per_batch, 0, 0))
    tok = pl.BlockSpec((TOP_K, tm), lambda i: (0, i))
    return pl.pallas_call(
        _outproj_kernel,
        grid=(n // tm,),
        in_specs=[row(half), row(half), const((d, d)), row(d), per_batch,
                  const((1, d)), per_batch, per_batch,
                  const((N_EXPERTS, d)), const((N_EXPERTS, d)), const((N_EXPERTS, 1)),
                  const((tm, tm)), const((tm, LANES))],
        out_specs=[row(d), row(half), tok, tok, tok, const((N_EXPERTS, LANES))],
        out_shape=[jax.ShapeDtypeStruct((n, d), F32),
                   jax.ShapeDtypeStruct((n, half), U32),
                   jax.ShapeDtypeStruct((TOP_K, n), I32),
                   jax.ShapeDtypeStruct((TOP_K, n), I32),
                   jax.ShapeDtypeStruct((TOP_K, n), F32),
                   jax.ShapeDtypeStruct((N_EXPERTS, LANES), F32)],
        scratch_shapes=[pltpu.VMEM((N_EXPERTS, LANES), F32)],
        compiler_params=pltpu.CompilerParams(dimension_semantics=("arbitrary",),
                                             vmem_limit_bytes=VMEM_LIMIT),
        name="outproj",
    )(oa, oh, w_out, x2d, gt1, g_ffn, sc2, sh2, wr_hi, wr_lo, rbias, upper, ones)


def _dest_kernel(pstart_ref, e_ref, slot_ref, o_ref):
    e = e_ref[...]

    def body(x, acc):
        return acc + jnp.where(e == x, pstart_ref[x], 0)

    o_ref[...] = lax.fori_loop(0, N_EXPERTS, body, slot_ref[...], unroll=8)


def _dest(pstart, eidx, slot):
    k, n = eidx.shape
    tn = min(n, 2048)
    grid_spec = pltpu.PrefetchScalarGridSpec(
        num_scalar_prefetch=1,
        grid=(n // tn,),
        in_specs=[pl.BlockSpec((k, tn), lambda i, *_: (0, i)),
                  pl.BlockSpec((k, tn), lambda i, *_: (0, i))],
        out_specs=pl.BlockSpec((k, tn), lambda i, *_: (0, i)),
    )
    return pl.pallas_call(
        _dest_kernel,
        grid_spec=grid_spec,
        out_shape=jax.ShapeDtypeStruct((k, n), I32),
        compiler_params=pltpu.CompilerParams(dimension_semantics=("arbitrary",)),
        name="dest",
    )(pstart, eidx, slot)


def _scatter_kernel(pfill_ref, pend_ref, nv_ref, dest_ref, h_hbm, xs_ref, zero_scr, h_buf, sems, lsem,
                    zsem):
    i = pl.program_id(0)
    n_steps = pl.num_programs(0)
    ts = h_buf.shape[1]
    tb = zero_scr.shape[0]
    n_tail = xs_ref.shape[0] // tb - nv_ref[0]

    def pad_fill(e, wait):
        def go(src, dst):
            cp = pltpu.make_async_copy(src, dst, zsem)
            cp.wait() if wait else cp.start()

        start = pfill_ref[e]
        end = pend_ref[e]
        head = jnp.minimum((-start) & 7, end - start)
        for r in range(7):
            @pl.when(r < head)
            def _():
                go(zero_scr.at[pl.ds(0, 1)], xs_ref.at[pl.ds(start + r, 1)])
        off = start + head
        rem = end - off
        size = tb // 2
        while size >= 8:
            cond = (rem & size) != 0

            @pl.when(cond)
            def _():
                go(zero_scr.at[pl.ds(0, size)], xs_ref.at[pl.ds(pl.multiple_of(off, 8), size)])
            off = off + jnp.where(cond, size, 0)
            size //= 2

    @pl.when(i == 0)
    def _():
        zero_scr[...] = jnp.zeros(zero_scr.shape, U32)

        def fill(e, carry):
            pad_fill(e, False)
            return carry

        lax.fori_loop(0, N_EXPERTS, fill, 0)

        def fill_tail(j, carry):
            start = pl.multiple_of((nv_ref[0] + j) * tb, tb)
            pltpu.make_async_copy(zero_scr, xs_ref.at[pl.ds(start, tb)], zsem).start()
            return carry

        lax.fori_loop(0, n_tail, fill_tail, 0)

        def drain(e, carry):
            pad_fill(e, True)
            return carry

        lax.fori_loop(0, N_EXPERTS, drain, 0)

        def drain_tail(j, carry):
            pltpu.make_async_copy(zero_scr, xs_ref.at[pl.ds(0, tb)], zsem).wait()
            return carry

        lax.fori_loop(0, n_tail, drain_tail, 0)

    def load(j):
        return pltpu.make_async_copy(h_hbm.at[pl.ds(pl.multiple_of(j * ts, ts), ts)],
                                     h_buf.at[lax.rem(j, H_SLOTS)], lsem.at[lax.rem(j, H_SLOTS)])

    def drain_scatters(j):
        sl = lax.rem(j, H_SLOTS)
        for k in range(TOP_K):
            pltpu.make_async_copy(h_buf.at[sl], xs_ref.at[pl.ds(0, ts)], sems.at[sl]).wait()

    @pl.when(i == 0)
    def _():
        load(0).start()

    @pl.when(i + 1 < n_steps)
    def _():
        load(i + 1).start()

    load(i).wait()
    cur = lax.rem(i, H_SLOTS)

    for t in range(ts):
        for k in range(TOP_K):
            pltpu.make_async_copy(h_buf.at[cur, pl.ds(t, 1)], xs_ref.at[pl.ds(dest_ref[k, t], 1)],
                                  sems.at[cur]).start(priority=k % 2)

    @pl.when(i >= 1)
    def _():
        drain_scatters(i - 1)

    @pl.when(i == n_steps - 1)
    def _():
        drain_scatters(i)


def _scatter(pfill, pend, n_valid, dest, hp, n_rows):
    n, w = hp.shape
    ts = T_ROW
    grid_spec = pltpu.PrefetchScalarGridSpec(
        num_scalar_prefetch=3,
        grid=(n // ts,),
        in_specs=[pl.BlockSpec((TOP_K, ts), lambda i, *_: (0, i), memory_space=pltpu.SMEM),
                  pl.BlockSpec(memory_space=pl.ANY)],
        out_specs=pl.BlockSpec(memory_space=pl.ANY),
        scratch_shapes=[pltpu.VMEM((TB_EXP, w), U32), pltpu.VMEM((H_SLOTS, ts, w), U32),
                        pltpu.SemaphoreType.DMA((H_SLOTS,)), pltpu.SemaphoreType.DMA((H_SLOTS,)),
                        pltpu.SemaphoreType.DMA],
    )
    return pl.pallas_call(
        _scatter_kernel,
        grid_spec=grid_spec,
        out_shape=jax.ShapeDtypeStruct((n_rows, w), U32),
        compiler_params=pltpu.CompilerParams(dimension_semantics=("arbitrary",),
                                             vmem_limit_bytes=VMEM_LIMIT),
        name="scatter",
    )(pfill, pend, n_valid, dest, hp)


def _experts_kernel(be_ref, nv_ref, ge_ref, ng_ref, xs_ref, wg_hbm, wu_hbm, wd_hbm, y_ref,
                    wg_f, wu_f, wd_f, sems, gctr):
    i = pl.program_id(0)
    prev = jnp.maximum(i - 1, 0)
    valid = i < nv_ref[0]
    fresh = valid & ((i == 0) | (be_ref[i] != be_ref[prev]))

    def weight_copies(g, slot):
        e = ge_ref[g]
        return (pltpu.make_async_copy(wg_hbm.at[e], wg_f.at[slot], sems.at[slot, 0]),
                pltpu.make_async_copy(wu_hbm.at[e], wu_f.at[slot], sems.at[slot, 1]),
                pltpu.make_async_copy(wd_hbm.at[e], wd_f.at[slot], sems.at[slot, 2]))

    @pl.when(i == 0)
    def _():
        gctr[0] = 0
        for g in range(W_AHEAD):
            @pl.when(g < ng_ref[0])
            def _():
                for cp in weight_copies(g, g):
                    cp.start()

    @pl.when(fresh)
    def _():
        g = gctr[0]
        slot = lax.rem(g, W_SLOTS)
        for cp in weight_copies(g, slot):
            cp.wait()

        @pl.when(g + W_AHEAD < ng_ref[0])
        def _():
            for cp in weight_copies(g + W_AHEAD, lax.rem(g + W_AHEAD, W_SLOTS)):
                cp.start()

        gctr[0] = g + 1
        gctr[1] = slot

    @pl.when(valid)
    def _():
        slot = gctr[1]
        lo, hi = _unpack_pair(xs_ref[...])
        x = jnp.concatenate([lo, hi], axis=1)
        hg = jnp.dot(x, wg_f[slot], preferred_element_type=F32)
        hu = jnp.dot(x, wu_f[slot], preferred_element_type=F32)
        y = jnp.dot(_silu(hg) * hu, wd_f[slot], preferred_element_type=F32)
        half = y.shape[1] // 2
        y_ref[...] = _pack_pair(y[:, :half], y[:, half:])

    @pl.when(jnp.logical_not(valid))
    def _():
        y_ref[...] = jnp.zeros(y_ref.shape, U32)


def _experts(block_e, n_valid, group_e, n_groups, xs, w_gate, w_up, w_down, n_blocks):
    tb = TB_EXP
    w = xs.shape[1]
    _, d, f = w_gate.shape
    grid_spec = pltpu.PrefetchScalarGridSpec(
        num_scalar_prefetch=4,
        grid=(n_blocks,),
        in_specs=[pl.BlockSpec((tb, w), lambda i, be, nv, ge, ng: (jnp.minimum(i, nv[0] - 1), 0)),
                  pl.BlockSpec(memory_space=pl.ANY),
                  pl.BlockSpec(memory_space=pl.ANY),
                  pl.BlockSpec(memory_space=pl.ANY)],
        out_specs=pl.BlockSpec((tb, w), lambda i, be, nv, ge, ng: (i, 0)),
        scratch_shapes=[pltpu.VMEM((W_SLOTS, d, f), F32), pltpu.VMEM((W_SLOTS, d, f), F32),
                        pltpu.VMEM((W_SLOTS, f, d), F32),
                        pltpu.SemaphoreType.DMA((W_SLOTS, 3)), pltpu.SMEM((2,), I32)],
    )
    return pl.pallas_call(
        _experts_kernel,
        grid_spec=grid_spec,
        out_shape=jax.ShapeDtypeStruct((n_blocks * tb, w), U32),
        compiler_params=pltpu.CompilerParams(dimension_semantics=("arbitrary",),
                                             vmem_limit_bytes=VMEM_LIMIT),
        name="experts",
    )(block_e, n_valid, group_e, n_groups, xs, w_gate, w_up, w_down)


def _combine_kernel(dcur_ref, dnxt_ref, x1_ref, hp_ref, gate_ref, gt_ref,
                    wsg_ref, wsu_ref, wsd_ref, y_ref, o_ref, buf, sems):
    i = pl.program_id(0)
    tc = x1_ref.shape[0]
    slot = lax.rem(i, 2)

    def issue(d_ref, sl):
        for t in range(tc):
            for k in range(TOP_K):
                pltpu.make_async_copy(y_ref.at[pl.ds(d_ref[k, t], 1)], buf.at[sl, k, pl.ds(t, 1)],
                                      sems.at[sl]).start(priority=k % 2)

    def compute():
        lo, hi = _unpack_pair(hp_ref[...])
        x = jnp.concatenate([lo.astype(BF16), hi.astype(BF16)], axis=1)
        hg = jnp.dot(x, wsg_ref[...], preferred_element_type=F32)
        hu = jnp.dot(x, wsu_ref[...], preferred_element_type=F32)
        a = (_silu(hg) * hu).astype(BF16)
        shared = jnp.dot(a, wsd_ref[...], preferred_element_type=F32)
        half = shared.shape[1] // 2
        gate = gate_ref[...]
        r_lo = shared[:, :half]
        r_hi = shared[:, half:]
        for k in range(TOP_K):
            lo, hi = _unpack_pair(buf[slot, k])
            gk = gate[:, k:k + 1]
            r_lo = r_lo + gk * lo
            r_hi = r_hi + gk * hi
        gt = gt_ref[0]
        o_ref[:, :half] = x1_ref[:, :half] + gt[:, :half] * r_lo
        o_ref[:, half:] = x1_ref[:, half:] + gt[:, half:] * r_hi

    @pl.when(i == 0)
    def _():
        issue(dcur_ref, 0)

    for k in range(TOP_K):
        pltpu.make_async_copy(y_ref.at[pl.ds(0, tc)], buf.at[slot, k], sems.at[slot]).wait()

    @pl.when(i + 1 < pl.num_programs(0))
    def _():
        issue(dnxt_ref, 1 - slot)
        compute()

    @pl.when(i + 1 >= pl.num_programs(0))
    def _():
        compute()


def _combine(dest, x1, hp, gate_t, gt2, wsg, wsu, wsd, y, seq):
    n, d = x1.shape
    tc = T_ROW
    w = hp.shape[1]
    f = wsg.shape[1]
    tiles_per_batch = seq // tc
    last = n // tc - 1
    return pl.pallas_call(
        _combine_kernel,
        grid=(n // tc,),
        in_specs=[pl.BlockSpec((TOP_K, tc), lambda i: (0, i), memory_space=pltpu.SMEM),
                  pl.BlockSpec((TOP_K, tc), lambda i: (0, jnp.minimum(i + 1, last)),
                               memory_space=pltpu.SMEM),
                  pl.BlockSpec((tc, d), lambda i: (i, 0)),
                  pl.BlockSpec((tc, w), lambda i: (i, 0)),
                  pl.BlockSpec((tc, TOP_K), lambda i: (i, 0)),
                  pl.BlockSpec((1, 1, d), lambda i: (i // tiles_per_batch, 0, 0)),
                  pl.BlockSpec((d, f), lambda i: (0, 0)),
                  pl.BlockSpec((d, f), lambda i: (0, 0)),
                  pl.BlockSpec((f, d), lambda i: (0, 0)),
                  pl.BlockSpec(memory_space=pl.ANY)],
        out_specs=pl.BlockSpec((tc, d), lambda i: (i, 0)),
        scratch_shapes=[pltpu.VMEM((2, TOP_K, tc, w), U32), pltpu.SemaphoreType.DMA((2,))],
        out_shape=jax.ShapeDtypeStruct((n, d), F32),
        compiler_params=pltpu.CompilerParams(dimension_semantics=("arbitrary",),
                                             vmem_limit_bytes=VMEM_LIMIT),
        name="combine",
    )(dest, dest, x1, hp, gate_t, gt2, wsg, wsu, wsd, y)


def kernel(x, c, positions, rel_bias, hgrn_lb_logits, w_ada, b_ada, g_mix, w_in, g_q, g_k, lam_q1, lam_k1, lam_q2, lam_k2, g_sub, g_hgrn, w_out, g_ffn, w_router, router_bias, w_exp_gate, w_exp_up, w_exp_down, w_sh_gate, w_sh_up, w_sh_down):
    batch, seq, d = x.shape
    n = batch * seq
    layer = 0
    x2d = x.reshape(n, d)

    c_pad = jnp.zeros((8, d), F32).at[:batch].set(c.astype(F32))
    mod = _ada(c_pad, w_ada[layer], b_ada[layer][None, :])[:batch]
    sh1, sc1, gt1, sh2, sc2, gt2 = [m.reshape(batch, 1, d) for m in jnp.split(mod, 6, axis=-1)]

    lbs = jnp.cumsum(jax.nn.softmax(hgrn_lb_logits.astype(F32), axis=1), axis=1)[:, layer]
    lbs = lbs.reshape(2, 1, SEG)
    reps = SEG // QK_DIM
    qk_gain = jnp.stack([jnp.tile(g_q[layer].astype(F32), reps) * (QK_DIM ** -0.5 * LOG2E),
                         jnp.tile(g_k[layer].astype(F32), reps)]).reshape(2, 1, SEG)
    lane = jnp.arange(LANES)
    g64 = jnp.where((lane[:, None] // QK_DIM) == (lane[None, :] // QK_DIM), 1.0 / QK_DIM, 0.0).astype(BF16)
    lam = (jnp.exp(jnp.sum(lam_q1[layer].astype(F32) * lam_k1[layer].astype(F32)))
           - jnp.exp(jnp.sum(lam_q2[layer].astype(F32) * lam_k2[layer].astype(F32)))
           + LAM_INIT).reshape(1)

    p, lf = _inproj(x2d, sc1, sh1, g_mix[layer][None, :], w_in[layer].astype(BF16), qk_gain, lbs, g64, seq)

    nt = seq // T_ATT
    pos_sub = positions.astype(I32).reshape(batch * seq // T_SUB, T_SUB)
    smin = jnp.min(pos_sub, axis=1)
    smax = jnp.max(pos_sub, axis=1)
    posq = positions.astype(I32).reshape(batch, nt, 1, T_ATT)
    posk = positions.astype(I32).reshape(batch, seq, 1)
    rb_t = rel_bias.astype(F32).T * LOG2E
    rb_tab = jnp.zeros((N_HEADS, LANES), F32).at[:, :REL_BUCKETS].set(rb_t)
    oa = _attention(p, smin, smax, posq, posk, rb_tab, rb_t, lam,
                    g_sub[layer][:, None].astype(F32), batch, seq)

    ci = jnp.arange(C_HGRN)
    tril = (ci[None, :] <= ci[:, None]).astype(BF16)
    triu = (ci[None, :] >= ci[:, None]).astype(BF16)
    oh = _hgrn(p, lf, tril, triu, g_hgrn[layer][None, :].astype(F32), batch, seq)

    half = d // 2
    w_out_bf = w_out[layer].astype(BF16)
    wr_t = w_router[layer].astype(F32).T
    wr_hi = wr_t.astype(BF16)
    wr_lo = (wr_t - wr_hi.astype(F32)).astype(BF16)
    ti = jnp.arange(TM_OUT)
    upper = (ti[:, None] < ti[None, :]).astype(BF16)
    ones = jnp.ones((TM_OUT, LANES), BF16)
    x1, hp, eidx, slot, gate, cnt = _outproj(
        oa, oh, w_out_bf, x2d, gt1, g_ffn[layer][None, :], sc2, sh2,
        wr_hi, wr_lo, router_bias[layer].astype(F32)[:, None], upper, ones, seq)

    tb = TB_EXP
    counts = cnt[:, 0].astype(I32)
    padded = (counts + tb - 1) // tb * tb
    pends = jnp.cumsum(padded)
    pstart = (pends - padded).astype(I32)
    n_blocks = (n * TOP_K) // tb + N_EXPERTS
    n_valid = (pends[-1] // tb).astype(I32).reshape(1)
    blk_start = jnp.arange(n_blocks, dtype=I32) * tb
    block_e = jnp.minimum(jnp.sum(pends[None, :] <= blk_start[:, None], axis=1), N_EXPERTS - 1).astype(I32)
    pfill = (pstart + counts).astype(I32)
    pend = pends.astype(I32)
    dest = _dest(pstart, eidx, slot)
    has_rows = counts > 0
    group_e = jnp.nonzero(has_rows, size=N_EXPERTS, fill_value=0)[0].astype(I32)
    n_groups = jnp.sum(has_rows).astype(I32).reshape(1)

    xs = _scatter(pfill, pend, n_valid, dest, hp, n_blocks * tb)
    y = _experts(block_e, n_valid, group_e, n_groups, xs,
                 w_exp_gate[layer], w_exp_up[layer], w_exp_down[layer], n_blocks)
    out = _combine(dest, x1, hp, gate.T, gt2,
                   w_sh_gate[layer].astype(BF16), w_sh_up[layer].astype(BF16),
                   w_sh_down[layer].astype(BF16), y, seq)
    return out.reshape(batch, seq, d)
```

```python
import functools
import math

import jax
import jax.numpy as jnp
from jax import lax
from jax.experimental import pallas as pl
from jax.experimental.pallas import tpu as pltpu

F32 = jnp.float32
BF16 = jnp.bfloat16
I32 = jnp.int32
U32 = jnp.uint32

D_MODEL = 2048
N_HEADS = 8
QK_DIM = 64
HEAD_DIM = 128
SEG = 1024
N_SEG = 8
REL_BUCKETS = 32
REL_MAX_DIST = 128
N_EXPERTS = 256
TOP_K = 8
N_GROUPS = 8
TOPK_GROUPS = 4
GROUP_SIZE = N_EXPERTS // N_GROUPS
EXPERT_DIM = 512
ROUTED_SCALE = 2.5
EPS = 1e-6
LAM_INIT = 0.8 - 0.6 * math.exp(-0.3 * 0)
LOG2E = math.log2(math.e)

LANES = 128
VMEM_LIMIT = 56 * 1024 * 1024

TM_IN = 512
IN_CHUNK = 256
T_ATT = 512
T_KEY = 1024
T_SUB = 128
V_PAD = 16
C_HGRN = 64
HGRN_GROUP = 32
TM_OUT = 256
T_ROW = 256
TB_EXP = 256
W_SLOTS = 3
W_AHEAD = 2
H_SLOTS = 3
NEG_BIG = -1e30
EXP_CLAMP = 80.0
FAST_BOUND = 60.0
BOUND_SLACK = 1.02
BOUND_PAD = 0.01

NT_DIMS = (((1,), (1,)), ((), ()))


def _silu(x):
    return x * jax.nn.sigmoid(x)


def _pack_pair(lo_f32, hi_f32):
    lo = lax.bitcast_convert_type(lo_f32.astype(BF16).astype(F32), U32)
    hi = lax.bitcast_convert_type(hi_f32.astype(BF16).astype(F32), U32)
    return (hi & jnp.uint32(0xFFFF0000)) | (lo >> 16)


def _unpack_pair(word):
    lo = lax.bitcast_convert_type(word << 16, F32)
    hi = lax.bitcast_convert_type(word & jnp.uint32(0xFFFF0000), F32)
    return lo, hi


def _ada_kernel(c_ref, w_ref, b_ref, o_ref):
    a = _silu(c_ref[...]).astype(BF16)
    o_ref[...] = jnp.dot(a, w_ref[...].astype(BF16), preferred_element_type=F32) + b_ref[...]


def _ada(c_pad, w, b):
    d, n = w.shape
    tn = 1024
    return pl.pallas_call(
        _ada_kernel,
        grid=(n // tn,),
        in_specs=[pl.BlockSpec((8, d), lambda j: (0, 0)),
                  pl.BlockSpec((d, tn), lambda j: (0, j)),
                  pl.BlockSpec((1, tn), lambda j: (0, j))],
        out_specs=pl.BlockSpec((8, tn), lambda j: (0, j)),
        out_shape=jax.ShapeDtypeStruct((8, n), F32),
        compiler_params=pltpu.CompilerParams(dimension_semantics=("arbitrary",),
                                             vmem_limit_bytes=VMEM_LIMIT),
        name="ada",
    )(c_pad, w, b)


def _inproj_kernel(x_ref, sc_ref, sh_ref, gmix_ref, w_ref, qkg_ref, lb_ref, g64_ref,
                   p_ref, lf_ref, h_scr):
    j = pl.program_id(1)

    @pl.when(j == 0)
    def _():
        x = x_ref[...]
        ms = jnp.mean(x * x, axis=-1, keepdims=True)
        y = x * lax.rsqrt(ms + EPS) * gmix_ref[...]
        h_scr[...] = (y * (1.0 + sc_ref[0]) + sh_ref[0]).astype(BF16)

    def chunks(epilogue):
        for c in range(SEG // IN_CHUNK):
            sl = slice(c * IN_CHUNK, (c + 1) * IN_CHUNK)
            epilogue(sl, jnp.dot(h_scr[...], w_ref[:, sl], preferred_element_type=F32))

    @pl.when(j < 2)
    def _():
        acc = jnp.dot(h_scr[...], w_ref[...], preferred_element_type=F32)
        gain = qkg_ref[0]
        for c in range(SEG // LANES):
            sl = slice(c * LANES, (c + 1) * LANES)
            xs = acc[:, sl]
            ms = jnp.dot((xs * xs).astype(BF16), g64_ref[...], preferred_element_type=F32)
            p_ref[:, sl] = (xs * lax.rsqrt(ms + EPS) * gain[:, sl]).astype(BF16)

    @pl.when((j == 2) | (j == 4))
    def _():
        def plain(sl, acc):
            p_ref[:, sl] = acc.astype(BF16)

        chunks(plain)

    @pl.when((j == 3) | (j == 7))
    def _():
        def silu(sl, acc):
            p_ref[:, sl] = _silu(acc).astype(BF16)

        chunks(silu)

    @pl.when((j == 5) | (j == 6))
    def _():
        def log_gate(sl, z):
            lb = lb_ref[0, :, sl]
            f = lb + (1.0 - lb) * jax.nn.sigmoid(z)
            lf_ref[:, sl] = jnp.log(f)
            p_ref[:, sl] = z.astype(BF16)

        chunks(log_gate)


def _inproj(x2d, sc1, sh1, g_mix, w_in_bf, qk_gain, lbs, g64, seq):
    n, d = x2d.shape
    tm = TM_IN
    tiles_per_batch = seq // tm
    return pl.pallas_call(
        _inproj_kernel,
        grid=(n // tm, N_SEG),
        in_specs=[
            pl.BlockSpec((tm, d), lambda i, j: (i, 0)),
            pl.BlockSpec((1, 1, d), lambda i, j: (i // tiles_per_batch, 0, 0)),
            pl.BlockSpec((1, 1, d), lambda i, j: (i // tiles_per_batch, 0, 0)),
            pl.BlockSpec((1, d), lambda i, j: (0, 0)),
            pl.BlockSpec((d, SEG), lambda i, j: (0, j)),
            pl.BlockSpec((1, 1, SEG), lambda i, j: (jnp.minimum(j, 1), 0, 0)),
            pl.BlockSpec((1, 1, SEG), lambda i, j: (jnp.clip(j - 5, 0, 1), 0, 0)),
            pl.BlockSpec((LANES, LANES), lambda i, j: (0, 0)),
        ],
        out_specs=[
            pl.BlockSpec((tm, SEG), lambda i, j: (i, j)),
            pl.BlockSpec((tm, SEG), lambda i, j: (i, jnp.clip(j - 5, 0, 1))),
        ],
        out_shape=[jax.ShapeDtypeStruct((n, N_SEG * SEG), BF16),
                   jax.ShapeDtypeStruct((n, 2 * SEG), F32)],
        scratch_shapes=[pltpu.VMEM((tm, d), BF16)],
        compiler_params=pltpu.CompilerParams(dimension_semantics=("arbitrary", "arbitrary"),
                                             vmem_limit_bytes=VMEM_LIMIT),
        name="inproj",
    )(x2d, sc1, sh1, g_mix, w_in_bf, qk_gain, lbs, g64)


def _t5_bias_tile(pos_q, pos_k, table):
    half = REL_BUCKETS // 2
    max_exact = half // 2
    rel = pos_k - pos_q
    n = jnp.abs(rel)
    nf = jnp.maximum(n, 1).astype(F32)
    large = max_exact + (jnp.log(nf / max_exact) / math.log(REL_MAX_DIST / max_exact)
                         * (half - max_exact)).astype(I32)
    large = jnp.minimum(large, half - 1)
    bucket = jnp.where(rel > 0, half, 0) + jnp.where(n < max_exact, n, large)
    rows = bucket.shape[0]
    tbl = jnp.broadcast_to(table, (rows, LANES))
    cols = [jnp.take_along_axis(tbl, bucket[:, c * LANES:(c + 1) * LANES], axis=1)
            for c in range(bucket.shape[1] // LANES)]
    return jnp.concatenate(cols, axis=1)


def _attn_kernel(smin_ref, smax_ref, q_ref, k_ref, v_ref, posq_ref, posk_ref, rbt_ref, rbx_ref, rb_ref, lam_ref,
                 gsub_ref, o_ref, vt_scr, s_a, s_b, cm_a, cm_b, p_a, p_b, al_a, al_b, m_scr, a_scr,
                 kmax_scr):
    b = pl.program_id(0)
    h = pl.program_id(1)
    i = pl.program_id(2)
    tq = T_ATT
    tk = T_KEY
    n_sub = tk // T_SUB
    nq_sub = tq // T_SUB
    ntk = k_ref.shape[0] // tk
    n_pairs = ntk // 2
    subs_per_batch = k_ref.shape[0] // T_SUB

    @pl.when(i == 0)
    def _():
        ones_row = jnp.where(lax.broadcasted_iota(I32, (V_PAD, tk), 0) == 0, 1.0, 0.0).astype(BF16)
        kmax_scr[...] = jnp.zeros(kmax_scr.shape, F32)

        def tr(c, carry):
            r0 = pl.multiple_of(c * tk, tk)
            vt_scr[c, :HEAD_DIM, :] = v_ref[pl.ds(r0, tk), :].astype(F32).T.astype(BF16)
            vt_scr[c, HEAD_DIM:, :] = ones_row
            kf = k_ref[pl.ds(r0, tk), :].astype(F32)
            for mp in range(2):
                km = kf[:, mp * QK_DIM:(mp + 1) * QK_DIM]
                nk = jnp.dot(km * km, jnp.ones((QK_DIM, LANES), F32), preferred_element_type=F32)
                kmax_scr[mp] = jnp.maximum(kmax_scr[mp], jnp.max(nk, axis=0, keepdims=True))
            return carry

        lax.fori_loop(0, ntk, tr, 0)

    q = q_ref[...]
    qs = (q[:, :QK_DIM], q[:, QK_DIM:])
    m_scr[...] = jnp.full(m_scr.shape, NEG_BIG, F32)
    a_scr[...] = jnp.zeros(a_scr.shape, F32)

    sub0 = b * subs_per_batch + i * nq_sub
    q_lo = smin_ref[sub0]
    q_hi = smax_ref[sub0]
    for u in range(1, nq_sub):
        q_lo = jnp.minimum(q_lo, smin_ref[sub0 + u])
        q_hi = jnp.maximum(q_hi, smax_ref[sub0 + u])
    c_pos = rb_ref[h, REL_BUCKETS - 1]
    c_neg = rb_ref[h, REL_BUCKETS // 2 - 1]
    pos_q = posq_ref[0, 0]

    def classify(j, u):
        ksub = b * subs_per_batch + j * n_sub + u
        lo = smin_ref[ksub] - q_hi
        hi = smax_ref[ksub] - q_lo
        far = (lo >= REL_MAX_DIST) | (hi <= -REL_MAX_DIST)
        shift = jnp.where(lo >= REL_MAX_DIST, c_pos, jnp.where(hi <= -REL_MAX_DIST, c_neg, 0.0))
        return far, shift

    def scores(j, s_ref, cm_ref):
        kk = k_ref[pl.ds(pl.multiple_of(j * tk, tk), tk), :]
        ks = (kk[:, :QK_DIM], kk[:, QK_DIM:])
        for mp in range(2):
            s = lax.dot_general(ks[mp], qs[mp], NT_DIMS, preferred_element_type=F32)
            s_ref[mp] = s
            for u in range(n_sub):
                _, shift = classify(j, u)
                cm_ref[mp, u] = jnp.max(s[u * T_SUB:(u + 1) * T_SUB], axis=0, keepdims=True) + shift

    def fixup(j, s_ref, cm_ref):
        for u in range(n_sub):
            far, _ = classify(j, u)

            @pl.when(jnp.logical_not(far))
            def _():
                rows = pl.ds(u * T_SUB, T_SUB)
                pos_k = posk_ref[0, pl.ds(pl.multiple_of(j * tk + u * T_SUB, T_SUB), T_SUB), :]
                bias = _t5_bias_tile(pos_q, pos_k, rbt_ref[pl.ds(h, 1), :])
                for mp in range(2):
                    sb = s_ref[mp, rows, :] + bias
                    s_ref[mp, rows, :] = sb
                    cm_ref[mp, u] = jnp.max(sb, axis=0, keepdims=True)

    def soft(j, s_ref, cm_ref, p_ref, al_ref):
        for mp in range(2):
            m_old = m_scr[mp]
            m_new = m_old
            for u in range(n_sub):
                m_new = jnp.maximum(m_new, cm_ref[mp, u])
            for u in range(n_sub):
                rows = pl.ds(u * T_SUB, T_SUB)
                _, shift = classify(j, u)
                p_ref[mp, rows, :] = jnp.exp2(s_ref[mp, rows, :] - (m_new - shift)).astype(BF16)
            m_scr[mp] = m_new
            al_ref[mp] = jnp.exp2(m_old - m_new)

    def pv(j, p_ref, al_ref):
        vt = vt_scr[j]
        for mp in range(2):
            a_scr[mp] = al_ref[mp] * a_scr[mp] + jnp.dot(vt, p_ref[mp], preferred_element_type=F32)

    sbuf = ((s_a, cm_a), (s_b, cm_b))
    pbuf = ((p_a, al_a), (p_b, al_b))
    last = 2 * (n_pairs - 1)

    def online_path():
        def step(j, par):
            scores(j + 2, *sbuf[par])
            soft(j + 1, *sbuf[1 - par], *pbuf[1 - par])
            pv(j, *pbuf[par])
            fixup(j + 2, *sbuf[par])

        scores(0, *sbuf[0])
        fixup(0, *sbuf[0])
        scores(1, *sbuf[1])
        soft(0, *sbuf[0], *pbuf[0])
        fixup(1, *sbuf[1])

        def body(jj, carry):
            step(2 * jj, 0)
            step(2 * jj + 1, 1)
            return carry

        lax.fori_loop(0, n_pairs - 1, body, 0)
        soft(last + 1, *sbuf[1], *pbuf[1])
        pv(last, *pbuf[0])
        pv(last + 1, *pbuf[1])

    def col_bound(mp):
        qf = qs[mp].astype(F32)
        nq = lax.dot_general(jnp.ones((8, QK_DIM), F32), qf * qf, NT_DIMS,
                             preferred_element_type=F32)[0:1, :]
        return jnp.sqrt(nq * kmax_scr[mp, :, 0:1]) * BOUND_SLACK

    b_max = rb_ref[h, 0]
    for e in range(1, REL_BUCKETS):
        b_max = jnp.maximum(b_max, rb_ref[h, e])
    bounds = [col_bound(mp) + (b_max + BOUND_PAD) for mp in range(2)]
    bound_max = jnp.max(jnp.maximum(bounds[0], bounds[1]))

    def fast_scores(j, p_ref):
        kk = k_ref[pl.ds(pl.multiple_of(j * tk, tk), tk), :]
        ks = (kk[:, :QK_DIM], kk[:, QK_DIM:])
        for mp in range(2):
            s = lax.dot_general(ks[mp], qs[mp], NT_DIMS, preferred_element_type=F32)
            for u in range(n_sub):
                _, shift = classify(j, u)
                p_ref[mp, pl.ds(u * T_SUB, T_SUB), :] = jnp.exp2(
                    s[u * T_SUB:(u + 1) * T_SUB] - (bounds[mp] - shift)).astype(BF16)

    def fast_fixup(j, p_ref):
        fars = [classify(j, u)[0] for u in range(n_sub)]
        all_far = fars[0]
        for f in fars[1:]:
            all_far = all_far & f

        @pl.when(jnp.logical_not(all_far))
        def _():
            for u in range(n_sub):
                @pl.when(jnp.logical_not(fars[u]))
                def _():
                    rows = pl.ds(u * T_SUB, T_SUB)
                    pos_k = posk_ref[0, pl.ds(pl.multiple_of(j * tk + u * T_SUB, T_SUB), T_SUB), :]
                    scale = _t5_bias_tile(pos_q, pos_k, rbx_ref[pl.ds(h, 1), :]).astype(BF16)
                    for mp in range(2):
                        p_ref[mp, rows, :] = p_ref[mp, rows, :] * scale

    def fast_pv(j, p_ref):
        vt = vt_scr[j]
        for mp in range(2):
            a_scr[mp] = a_scr[mp] + jnp.dot(vt, p_ref[mp], preferred_element_type=F32)

    def fast_path():
        pb = (p_a, p_b)

        def step(j, par):
            fast_scores(j + 1, pb[1 - par])
            fast_pv(j, pb[par])
            fast_fixup(j + 1, pb[1 - par])

        fast_scores(0, pb[0])
        fast_fixup(0, pb[0])

        def body(jj, carry):
            step(2 * jj, 0)
            step(2 * jj + 1, 1)
            return carry

        lax.fori_loop(0, n_pairs - 1, body, 0)
        step(last, 0)
        fast_pv(last + 1, pb[1])

    use_fast = bound_max <= FAST_BOUND

    @pl.when(use_fast)
    def _():
        fast_path()

    @pl.when(jnp.logical_not(use_fast))
    def _():
        online_path()

    num = [a_scr[mp, :HEAD_DIM, :] / a_scr[mp, HEAD_DIM:HEAD_DIM + 1, :] for mp in range(2)]
    o = num[0] - lam_ref[0] * num[1]
    ms = jnp.mean(o * o, axis=0, keepdims=True)
    o = o * lax.rsqrt(ms + EPS) * (gsub_ref[...] * (1.0 - LAM_INIT))
    o_ref[...] = o.T.astype(BF16)


def _attention(p, smin, smax, posq, posk, rb_tab, rb_t, lam, g_sub, batch, seq):
    t = T_ATT
    tk = T_KEY
    nt = seq // t
    ntk = seq // tk
    assert seq % (2 * tk) == 0
    n = batch * seq
    n_sub = tk // T_SUB
    va = HEAD_DIM + V_PAD
    grid_spec = pltpu.PrefetchScalarGridSpec(
        num_scalar_prefetch=2,
        grid=(batch, N_HEADS, nt),
        in_specs=[
            pl.BlockSpec((t, HEAD_DIM), lambda b, h, i, *_: (b * nt + i, h)),
            pl.BlockSpec((seq, HEAD_DIM), lambda b, h, i, *_: (b, N_HEADS + h)),
            pl.BlockSpec((seq, HEAD_DIM), lambda b, h, i, *_: (b, 2 * N_HEADS + h)),
            pl.BlockSpec((1, 1, 1, t), lambda b, h, i, *_: (b, i, 0, 0)),
            pl.BlockSpec((1, seq, 1), lambda b, h, i, *_: (b, 0, 0)),
            pl.BlockSpec((N_HEADS, LANES), lambda b, h, i, *_: (0, 0)),
            pl.BlockSpec((N_HEADS, LANES), lambda b, h, i, *_: (0, 0)),
            pl.BlockSpec(memory_space=pltpu.SMEM),
            pl.BlockSpec(memory_space=pltpu.SMEM),
            pl.BlockSpec((HEAD_DIM, 1), lambda b, h, i, *_: (0, 0)),
        ],
        out_specs=pl.BlockSpec((t, HEAD_DIM), lambda b, h, i, *_: (b * nt + i, h)),
        scratch_shapes=[pltpu.VMEM((ntk, va, tk), BF16),
                        pltpu.VMEM((2, tk, t), F32),
                        pltpu.VMEM((2, tk, t), F32),
                        pltpu.VMEM((2, n_sub, 1, t), F32),
                        pltpu.VMEM((2, n_sub, 1, t), F32),
                        pltpu.VMEM((2, tk, t), BF16),
                        pltpu.VMEM((2, tk, t), BF16),
                        pltpu.VMEM((2, 1, t), F32),
                        pltpu.VMEM((2, 1, t), F32),
                        pltpu.VMEM((2, 1, t), F32),
                        pltpu.VMEM((2, va, t), F32),
                        pltpu.VMEM((2, 1, LANES), F32)],
    )
    return pl.pallas_call(
        _attn_kernel,
        grid_spec=grid_spec,
        out_shape=jax.ShapeDtypeStruct((n, N_HEADS * HEAD_DIM), BF16),
        compiler_params=pltpu.CompilerParams(
            dimension_semantics=("arbitrary", "arbitrary", "arbitrary"),
            vmem_limit_bytes=VMEM_LIMIT),
        name="attn",
    )(smin, smax, p, p, p, posq, posk, rb_tab, jnp.exp2(rb_tab), rb_t, lam, g_sub)


def _hgrn_group(r0, forward, q_ref, v_ref, lf_ref, tri_ref, st):
    c = C_HGRN
    order = range(HGRN_GROUP) if forward else range(HGRN_GROUP - 1, -1, -1)
    rows = [pl.ds(r0 + k * c, c) for k in order]
    tri = tri_ref[...]
    row = lax.broadcasted_iota(I32, (c, c), 0)
    col = lax.broadcasted_iota(I32, (c, c), 1)
    keep = (col <= row) if forward else (col >= row)

    gs = [lf_ref[r, :] for r in rows]
    bsums = []
    for g in gs:
        g_hi = g.astype(BF16)
        g_lo = (g - g_hi.astype(F32)).astype(BF16)
        bsums.append(jnp.dot(tri, g_hi, preferred_element_type=F32)
                     + jnp.dot(tri, g_lo, preferred_element_type=F32))

    q_in, q_t, k_t, k_st, v_t, vs, decay = [], [], [], [], [], [], []
    for r, g, bsum in zip(rows, gs, bsums):
        if forward:
            ref = bsum[c // 2 - 1:c // 2, :]
            b_end = bsum[c - 1:c, :]
        else:
            ref = bsum[c // 2:c // 2 + 1, :]
            b_end = bsum[0:1, :]
        q = q_ref[r, :].astype(F32)
        v = v_ref[r, :]
        kf = 1.0 - jnp.exp(g)
        q_in.append((q * jnp.exp(bsum)).astype(BF16))
        q_t.append((q * jnp.exp(jnp.minimum(bsum - ref, EXP_CLAMP))).astype(BF16))
        k_t.append((kf * jnp.exp(jnp.minimum(ref - bsum, EXP_CLAMP))).astype(BF16))
        k_st.append((kf * jnp.exp(b_end - bsum)).astype(BF16))
        v_t.append(v.astype(F32).T.astype(BF16))
        vs.append(v)
        decay.append(jnp.exp(b_end))

    scores = [lax.dot_general(a, b, NT_DIMS, preferred_element_type=F32) for a, b in zip(q_t, k_t)]
    st_add = [jnp.dot(a, b, preferred_element_type=F32) for a, b in zip(v_t, k_st)]
    intra = [jnp.dot(jnp.where(keep, s, 0.0).astype(BF16), v, preferred_element_type=F32)
             for s, v in zip(scores, vs)]

    outs = []
    for k in range(HGRN_GROUP):
        o = lax.dot_general(q_in[k], st.astype(BF16), NT_DIMS, preferred_element_type=F32) + intra[k]
        st = st * decay[k] + st_add[k]
        outs.append((rows[k], o))
    return outs, st


def _hgrn_kernel(q_ref, v_ref, g_ref, lff_ref, lfb_ref, tril_ref, triu_ref, gh_ref, o_ref, of_scr):
    rows_per_group = C_HGRN * HGRN_GROUP
    n_groups = q_ref.shape[0] // rows_per_group
    st0 = jnp.zeros((HEAD_DIM, HEAD_DIM), F32)

    def fwd(gi, st):
        r0 = pl.multiple_of(gi * rows_per_group, rows_per_group)
        outs, st = _hgrn_group(r0, True, q_ref, v_ref, lff_ref, tril_ref, st)
        for r, o in outs:
            of_scr[r, :] = o
        return st

    lax.fori_loop(0, n_groups, fwd, st0)

    def bwd(gi, st):
        r0 = pl.multiple_of((n_groups - 1 - gi) * rows_per_group, rows_per_group)
        outs, st = _hgrn_group(r0, False, q_ref, v_ref, lfb_ref, triu_ref, st)
        for r, o in outs:
            o = of_scr[r, :] + o
            ms = jnp.mean(o * o, axis=-1, keepdims=True)
            y = o * lax.rsqrt(ms + EPS) * gh_ref[...] * g_ref[r, :].astype(F32)
            o_ref[r, :] = y.astype(BF16)
        return st

    lax.fori_loop(0, n_groups, bwd, st0)


def _hgrn(p, lf, tril, triu, g_hgrn, batch, seq):
    n = batch * seq
    c = C_HGRN
    blk = lambda off: pl.BlockSpec((seq, HEAD_DIM), lambda b, h: (b, off + h))
    return pl.pallas_call(
        _hgrn_kernel,
        grid=(batch, N_HEADS),
        in_specs=[blk(3 * N_HEADS), blk(4 * N_HEADS), blk(7 * N_HEADS), blk(0), blk(N_HEADS),
                  pl.BlockSpec((c, c), lambda b, h: (0, 0)),
                  pl.BlockSpec((c, c), lambda b, h: (0, 0)),
                  pl.BlockSpec((1, HEAD_DIM), lambda b, h: (0, 0))],
        out_specs=pl.BlockSpec((seq, HEAD_DIM), lambda b, h: (b, h)),
        out_shape=jax.ShapeDtypeStruct((n, N_HEADS * HEAD_DIM), BF16),
        scratch_shapes=[pltpu.VMEM((seq, HEAD_DIM), F32)],
        compiler_params=pltpu.CompilerParams(dimension_semantics=("arbitrary", "arbitrary"),
                                             vmem_limit_bytes=VMEM_LIMIT),
        name="hgrn",
    )(p, p, p, lf, lf, tril, triu, g_hgrn)


def _col_max(x):
    return jnp.max(x, axis=0, keepdims=True)


def _outproj_kernel(oa_ref, oh_ref, w_ref, x_ref, gt_ref, gffn_ref, sc_ref, sh_ref,
                    wr_hi_ref, wr_lo_ref, rbias_ref, upper_ref, ones_ref,
                    x1_ref, hp_ref, eidx_ref, slot_ref, gate_ref, cnt_ref, cnt_scr):
    i = pl.program_id(0)
    tm = x_ref.shape[0]

    @pl.when(i == 0)
    def _():
        cnt_scr[...] = jnp.zeros(cnt_scr.shape, F32)

    mixed = jnp.concatenate([oa_ref[...], oh_ref[...]], axis=1)
    acc = jnp.dot(mixed, w_ref[...], preferred_element_type=F32)
    x1 = x_ref[...] + gt_ref[0] * acc
    x1_ref[...] = x1
    ms = jnp.mean(x1 * x1, axis=-1, keepdims=True)
    h2 = x1 * lax.rsqrt(ms + EPS) * gffn_ref[...] * (1.0 + sc_ref[0]) + sh_ref[0]
    half = h2.shape[1] // 2
    hp_ref[...] = _pack_pair(h2[:, :half], h2[:, half:])

    h_hi = h2.astype(BF16)
    h_lo = (h2 - h_hi.astype(F32)).astype(BF16)
    wr_hi = wr_hi_ref[...]
    logits = (lax.dot_general(wr_hi, h_hi, NT_DIMS, preferred_element_type=F32)
              + lax.dot_general(wr_hi, h_lo, NT_DIMS, preferred_element_type=F32)
              + lax.dot_general(wr_lo_ref[...], h_hi, NT_DIMS, preferred_element_type=F32))
    scores = jax.nn.sigmoid(logits)
    biased = scores + rbias_ref[...]

    gs = []
    for g in range(N_GROUPS):
        blk = biased[g * GROUP_SIZE:(g + 1) * GROUP_SIZE, :]
        top1 = _col_max(blk)
        eq = blk == top1
        n_eq = jnp.sum(eq.astype(F32), axis=0, keepdims=True)
        second = _col_max(jnp.where(eq, -jnp.inf, blk))
        gs.append(top1 + jnp.where(n_eq > 1.0, top1, second))
    gsm = jnp.concatenate(gs, axis=0)
    giota = lax.broadcasted_iota(I32, gsm.shape, 0)
    gsel = jnp.zeros(gsm.shape, F32)
    for _ in range(TOPK_GROUPS):
        top = _col_max(gsm)
        idx = jnp.min(jnp.where(gsm == top, giota, N_GROUPS), axis=0, keepdims=True)
        pick = giota == idx
        gsel = jnp.where(pick, 1.0, gsel)
        gsm = jnp.where(pick, -jnp.inf, gsm)
    emask = jnp.concatenate(
        [jnp.broadcast_to(gsel[g:g + 1, :], (GROUP_SIZE, tm)) for g in range(N_GROUPS)], axis=0)
    masked = jnp.where(emask > 0.5, biased, -jnp.inf)

    eiota = lax.broadcasted_iota(I32, masked.shape, 0)
    idxs, gates = [], []
    for _ in range(TOP_K):
        top = _col_max(masked)
        idx = jnp.min(jnp.where(masked == top, eiota, N_EXPERTS), axis=0, keepdims=True)
        pick = eiota == idx
        gates.append(jnp.sum(jnp.where(pick, scores, 0.0), axis=0, keepdims=True))
        idxs.append(idx)
        masked = jnp.where(pick, -jnp.inf, masked)
    gate = jnp.concatenate(gates, axis=0)
    gate = gate / jnp.sum(gate, axis=0, keepdims=True) * ROUTED_SCALE
    eidx = jnp.concatenate(idxs, axis=0)
    eidx_ref[...] = eidx
    gate_ref[...] = gate

    sel = jnp.zeros(masked.shape, F32)
    for k in range(TOP_K):
        sel = jnp.where(eiota == idxs[k], 1.0, sel)
    sel_bf = sel.astype(BF16)
    rank = jnp.dot(sel_bf, upper_ref[...], preferred_element_type=F32)
    base = cnt_scr[...]
    posn = base[:, :1] + rank
    slots = [jnp.sum(jnp.where(eiota == idxs[k], posn, 0.0), axis=0, keepdims=True)
             for k in range(TOP_K)]
    slot_ref[...] = jnp.concatenate(slots, axis=0).astype(I32)
    new_cnt = base + jnp.dot(sel_bf, ones_ref[...], preferred_element_type=F32)
    cnt_scr[...] = new_cnt
    cnt_ref[...] = new_cnt


def _outproj(oa, oh, w_out, x2d, gt1, g_ffn, sc2, sh2, wr_hi, wr_lo, rbias, upper, ones, seq):
    n, d = x2d.shape
    tm = TM_OUT
    tiles_per_batch = seq // tm
    half = d // 2
    row = lambda w: pl.BlockSpec((tm, w), lambda i: (i, 0))
    const = lambda shape: pl.BlockSpec(shape, lambda i: tuple(0 for _ in shape))
    per_batch = pl.BlockSpec((1, 1, d), lambda i: (i // tiles_per_batch, 0, 0))
    tok = pl.BlockSpec((TOP_K, tm), lambda i: (0, i))
    return pl.pallas_call(
        _outproj_kernel,
        grid=(n // tm,),
        in_specs=[row(half), row(half), const((d, d)), row(d), per_batch,
                  const((1, d)), per_batch, per_batch,
                  const((N_EXPERTS, d)), const((N_EXPERTS, d)), const((N_EXPERTS, 1)),
                  const((tm, tm)), const((tm, LANES))],
        out_specs=[row(d), row(half), tok, tok, tok, const((N_EXPERTS, LANES))],
        out_shape=[jax.ShapeDtypeStruct((n, d), F32),
                   jax.ShapeDtypeStruct((n, half), U32),
                   jax.ShapeDtypeStruct((TOP_K, n), I32),
                   jax.ShapeDtypeStruct((TOP_K, n), I32),
                   jax.ShapeDtypeStruct((TOP_K, n), F32),
                   jax.ShapeDtypeStruct((N_EXPERTS, LANES), F32)],
        scratch_shapes=[pltpu.VMEM((N_EXPERTS, LANES), F32)],
        compiler_params=pltpu.CompilerParams(dimension_semantics=("arbitrary",),
                                             vmem_limit_bytes=VMEM_LIMIT),
        name="outproj",
    )(oa, oh, w_out, x2d, gt1, g_ffn, sc2, sh2, wr_hi, wr_lo, rbias, upper, ones)


def _dest_kernel(pstart_ref, e_ref, slot_ref, o_ref):
    e = e_ref[...]

    def body(x, acc):
        return acc + jnp.where(e == x, pstart_ref[x], 0)

    o_ref[...] = lax.fori_loop(0, N_EXPERTS, body, slot_ref[...], unroll=8)


def _dest(pstart, eidx, slot):
    k, n = eidx.shape
    tn = min(n, 2048)
    grid_spec = pltpu.PrefetchScalarGridSpec(
        num_scalar_prefetch=1,
        grid=(n // tn,),
        in_specs=[pl.BlockSpec((k, tn), lambda i, *_: (0, i)),
                  pl.BlockSpec((k, tn), lambda i, *_: (0, i))],
        out_specs=pl.BlockSpec((k, tn), lambda i, *_: (0, i)),
    )
    return pl.pallas_call(
        _dest_kernel,
        grid_spec=grid_spec,
        out_shape=jax.ShapeDtypeStruct((k, n), I32),
        compiler_params=pltpu.CompilerParams(dimension_semantics=("arbitrary",)),
        name="dest",
    )(pstart, eidx, slot)


def _scatter_kernel(pfill_ref, pend_ref, nv_ref, dest_ref, h_hbm, xs_ref, zero_scr, h_buf, sems, lsem,
                    zsem):
    i = pl.program_id(0)
    n_steps = pl.num_programs(0)
    ts = h_buf.shape[1]
    tb = zero_scr.shape[0]
    n_tail = xs_ref.shape[0] // tb - nv_ref[0]

    def pad_fill(e, wait):
        def go(src, dst):
            cp = pltpu.make_async_copy(src, dst, zsem)
            cp.wait() if wait else cp.start()

        start = pfill_ref[e]
        end = pend_ref[e]
        head = jnp.minimum((-start) & 7, end - start)
        for r in range(7):
            @pl.when(r < head)
            def _():
                go(zero_scr.at[pl.ds(0, 1)], xs_ref.at[pl.ds(start + r, 1)])
        off = start + head
        rem = end - off
        size = tb // 2
        while size >= 8:
            cond = (rem & size) != 0

            @pl.when(cond)
            def _():
                go(zero_scr.at[pl.ds(0, size)], xs_ref.at[pl.ds(pl.multiple_of(off, 8), size)])
            off = off + jnp.where(cond, size, 0)
            size //= 2

    @pl.when(i == 0)
    def _():
        zero_scr[...] = jnp.zeros(zero_scr.shape, U32)

        def fill(e, carry):
            pad_fill(e, False)
            return carry

        lax.fori_loop(0, N_EXPERTS, fill, 0)

        def fill_tail(j, carry):
            start = pl.multiple_of((nv_ref[0] + j) * tb, tb)
            pltpu.make_async_copy(zero_scr, xs_ref.at[pl.ds(start, tb)], zsem).start()
            return carry

        lax.fori_loop(0, n_tail, fill_tail, 0)

        def drain(e, carry):
            pad_fill(e, True)
            return carry

        lax.fori_loop(0, N_EXPERTS, drain, 0)

        def drain_tail(j, carry):
            pltpu.make_async_copy(zero_scr, xs_ref.at[pl.ds(0, tb)], zsem).wait()
            return carry

        lax.fori_loop(0, n_tail, drain_tail, 0)

    def load(j):
        return pltpu.make_async_copy(h_hbm.at[pl.ds(pl.multiple_of(j * ts, ts), ts)],
                                     h_buf.at[lax.rem(j, H_SLOTS)], lsem.at[lax.rem(j, H_SLOTS)])

    def drain_scatters(j):
        sl = lax.rem(j, H_SLOTS)
        for k in range(TOP_K):
            pltpu.make_async_copy(h_buf.at[sl], xs_ref.at[pl.ds(0, ts)], sems.at[sl]).wait()

    @pl.when(i == 0)
    def _():
        load(0).start()

    @pl.when(i + 1 < n_steps)
    def _():
        load(i + 1).start()

    load(i).wait()
    cur = lax.rem(i, H_SLOTS)

    for t in range(ts):
        for k in range(TOP_K):
            pltpu.make_async_copy(h_buf.at[cur, pl.ds(t, 1)], xs_ref.at[pl.ds(dest_ref[k, t], 1)],
                                  sems.at[cur]).start(priority=k % 2)

    @pl.when(i >= 1)
    def _():
        drain_scatters(i - 1)

    @pl.when(i == n_steps - 1)
    def _():
        drain_scatters(i)


def _scatter(pfill, pend, n_valid, dest, hp, n_rows):
    n, w = hp.shape
    ts = T_ROW
    grid_spec = pltpu.PrefetchScalarGridSpec(
        num_scalar_prefetch=3,
        grid=(n // ts,),
        in_specs=[pl.BlockSpec((TOP_K, ts), lambda i, *_: (0, i), memory_space=pltpu.SMEM),
                  pl.BlockSpec(memory_space=pl.ANY)],
        out_specs=pl.BlockSpec(memory_space=pl.ANY),
        scratch_shapes=[pltpu.VMEM((TB_EXP, w), U32), pltpu.VMEM((H_SLOTS, ts, w), U32),
                        pltpu.SemaphoreType.DMA((H_SLOTS,)), pltpu.SemaphoreType.DMA((H_SLOTS,)),
                        pltpu.SemaphoreType.DMA],
    )
    return pl.pallas_call(
        _scatter_kernel,
        grid_spec=grid_spec,
        out_shape=jax.ShapeDtypeStruct((n_rows, w), U32),
        compiler_params=pltpu.CompilerParams(dimension_semantics=("arbitrary",),
                                             vmem_limit_bytes=VMEM_LIMIT),
        name="scatter",
    )(pfill, pend, n_valid, dest, hp)


def _experts_kernel(be_ref, nv_ref, ge_ref, ng_ref, xs_ref, wg_hbm, wu_hbm, wd_hbm, y_ref,
                    wg_f, wu_f, wd_f, sems, gctr):
    i = pl.program_id(0)
    prev = jnp.maximum(i - 1, 0)
    valid = i < nv_ref[0]
    fresh = valid & ((i == 0) | (be_ref[i] != be_ref[prev]))

    def weight_copies(g, slot):
        e = ge_ref[g]
        return (pltpu.make_async_copy(wg_hbm.at[e], wg_f.at[slot], sems.at[slot, 0]),
                pltpu.make_async_copy(wu_hbm.at[e], wu_f.at[slot], sems.at[slot, 1]),
                pltpu.make_async_copy(wd_hbm.at[e], wd_f.at[slot], sems.at[slot, 2]))

    @pl.when(i == 0)
    def _():
        gctr[0] = 0
        for g in range(W_AHEAD):
            @pl.when(g < ng_ref[0])
            def _():
                for cp in weight_copies(g, g):
                    cp.start()

    @pl.when(fresh)
    def _():
        g = gctr[0]
        slot = lax.rem(g, W_SLOTS)
        for cp in weight_copies(g, slot):
            cp.wait()

        @pl.when(g + W_AHEAD < ng_ref[0])
        def _():
            for cp in weight_copies(g + W_AHEAD, lax.rem(g + W_AHEAD, W_SLOTS)):
                cp.start()

        gctr[0] = g + 1
        gctr[1] = slot

    @pl.when(valid)
    def _():
        slot = gctr[1]
        lo, hi = _unpack_pair(xs_ref[...])
        x = jnp.concatenate([lo, hi], axis=1)
        hg = jnp.dot(x, wg_f[slot], preferred_element_type=F32)
        hu = jnp.dot(x, wu_f[slot], preferred_element_type=F32)
        y = jnp.dot(_silu(hg) * hu, wd_f[slot], preferred_element_type=F32)
        half = y.shape[1] // 2
        y_ref[...] = _pack_pair(y[:, :half], y[:, half:])

    @pl.when(jnp.logical_not(valid))
    def _():
        y_ref[...] = jnp.zeros(y_ref.shape, U32)


def _experts(block_e, n_valid, group_e, n_groups, xs, w_gate, w_up, w_down, n_blocks):
    tb = TB_EXP
    w = xs.shape[1]
    _, d, f = w_gate.shape
    grid_spec = pltpu.PrefetchScalarGridSpec(
        num_scalar_prefetch=4,
        grid=(n_blocks,),
        in_specs=[pl.BlockSpec((tb, w), lambda i, be, nv, ge, ng: (jnp.minimum(i, nv[0] - 1), 0)),
                  pl.BlockSpec(memory_space=pl.ANY),
                  pl.BlockSpec(memory_space=pl.ANY),
                  pl.BlockSpec(memory_space=pl.ANY)],
        out_specs=pl.BlockSpec((tb, w), lambda i, be, nv, ge, ng: (i, 0)),
        scratch_shapes=[pltpu.VMEM((W_SLOTS, d, f), F32), pltpu.VMEM((W_SLOTS, d, f), F32),
                        pltpu.VMEM((W_SLOTS, f, d), F32),
                        pltpu.SemaphoreType.DMA((W_SLOTS, 3)), pltpu.SMEM((2,), I32)],
    )
    return pl.pallas_call(
        _experts_kernel,
        grid_spec=grid_spec,
        out_shape=jax.ShapeDtypeStruct((n_blocks * tb, w), U32),
        compiler_params=pltpu.CompilerParams(dimension_semantics=("arbitrary",),
                                             vmem_limit_bytes=VMEM_LIMIT),
        name="experts",
    )(block_e, n_valid, group_e, n_groups, xs, w_gate, w_up, w_down)


def _combine_kernel(dcur_ref, dnxt_ref, x1_ref, hp_ref, gate_ref, gt_ref,
                    wsg_ref, wsu_ref, wsd_ref, y_ref, o_ref, buf, sems):
    i = pl.program_id(0)
    tc = x1_ref.shape[0]
    slot = lax.rem(i, 2)

    def issue(d_ref, sl):
        for t in range(tc):
            for k in range(TOP_K):
                pltpu.make_async_copy(y_ref.at[pl.ds(d_ref[k, t], 1)], buf.at[sl, k, pl.ds(t, 1)],
                                      sems.at[sl]).start(priority=k % 2)

    def compute():
        lo, hi = _unpack_pair(hp_ref[...])
        x = jnp.concatenate([lo.astype(BF16), hi.astype(BF16)], axis=1)
        hg = jnp.dot(x, wsg_ref[...], preferred_element_type=F32)
        hu = jnp.dot(x, wsu_ref[...], preferred_element_type=F32)
        a = (_silu(hg) * hu).astype(BF16)
        shared = jnp.dot(a, wsd_ref[...], preferred_element_type=F32)
        half = shared.shape[1] // 2
        gate = gate_ref[...]
        r_lo = shared[:, :half]
        r_hi = shared[:, half:]
        for k in range(TOP_K):
            lo, hi = _unpack_pair(buf[slot, k])
            gk = gate[:, k:k + 1]
            r_lo = r_lo + gk * lo
            r_hi = r_hi + gk * hi
        gt = gt_ref[0]
        o_ref[:, :half] = x1_ref[:, :half] + gt[:, :half] * r_lo
        o_ref[:, half:] = x1_ref[:, half:] + gt[:, half:] * r_hi

    @pl.when(i == 0)
    def _():
        issue(dcur_ref, 0)

    for k in range(TOP_K):
        pltpu.make_async_copy(y_ref.at[pl.ds(0, tc)], buf.at[slot, k], sems.at[slot]).wait()

    @pl.when(i + 1 < pl.num_programs(0))
    def _():
        issue(dnxt_ref, 1 - slot)
        compute()

    @pl.when(i + 1 >= pl.num_programs(0))
    def _():
        compute()


def _combine(dest, x1, hp, gate_t, gt2, wsg, wsu, wsd, y, seq):
    n, d = x1.shape
    tc = T_ROW
    w = hp.shape[1]
    f = wsg.shape[1]
    tiles_per_batch = seq // tc
    last = n // tc - 1
    return pl.pallas_call(
        _combine_kernel,
        grid=(n // tc,),
        in_specs=[pl.BlockSpec((TOP_K, tc), lambda i: (0, i), memory_space=pltpu.SMEM),
                  pl.BlockSpec((TOP_K, tc), lambda i: (0, jnp.minimum(i + 1, last)),
                               memory_space=pltpu.SMEM),
                  pl.BlockSpec((tc, d), lambda i: (i, 0)),
                  pl.BlockSpec((tc, w), lambda i: (i, 0)),
                  pl.BlockSpec((tc, TOP_K), lambda i: (i, 0)),
                  pl.BlockSpec((1, 1, d), lambda i: (i // tiles_per_batch, 0, 0)),
                  pl.BlockSpec((d, f), lambda i: (0, 0)),
                  pl.BlockSpec((d, f), lambda i: (0, 0)),
                  pl.BlockSpec((f, d), lambda i: (0, 0)),
                  pl.BlockSpec(memory_space=pl.ANY)],
        out_specs=pl.BlockSpec((tc, d), lambda i: (i, 0)),
        scratch_shapes=[pltpu.VMEM((2, TOP_K, tc, w), U32), pltpu.SemaphoreType.DMA((2,))],
        out_shape=jax.ShapeDtypeStruct((n, d), F32),
        compiler_params=pltpu.CompilerParams(dimension_semantics=("arbitrary",),
                                             vmem_limit_bytes=VMEM_LIMIT),
        name="combine",
    )(dest, dest, x1, hp, gate_t, gt2, wsg, wsu, wsd, y)


def kernel(x, c, positions, rel_bias, hgrn_lb_logits, w_ada, b_ada, g_mix, w_in, g_q, g_k, lam_q1, lam_k1, lam_q2, lam_k2, g_sub, g_hgrn, w_out, g_ffn, w_router, router_bias, w_exp_gate, w_exp_up, w_exp_down, w_sh_gate, w_sh_up, w_sh_down):
    batch, seq, d = x.shape
    n = batch * seq
    layer = 0
    x2d = x.reshape(n, d)

    c_pad = jnp.zeros((8, d), F32).at[:batch].set(c.astype(F32))
    mod = _ada(c_pad, w_ada[layer], b_ada[layer][None, :])[:batch]
    sh1, sc1, gt1, sh2, sc2, gt2 = [m.reshape(batch, 1, d) for m in jnp.split(mod, 6, axis=-1)]

    lbs = jnp.cumsum(jax.nn.softmax(hgrn_lb_logits.astype(F32), axis=1), axis=1)[:, layer]
    lbs = lbs.reshape(2, 1, SEG)
    reps = SEG // QK_DIM
    qk_gain = jnp.stack([jnp.tile(g_q[layer].astype(F32), reps) * (QK_DIM ** -0.5 * LOG2E),
                         jnp.tile(g_k[layer].astype(F32), reps)]).reshape(2, 1, SEG)
    lane = jnp.arange(LANES)
    g64 = jnp.where((lane[:, None] // QK_DIM) == (lane[None, :] // QK_DIM), 1.0 / QK_DIM, 0.0).astype(BF16)
    lam = (jnp.exp(jnp.sum(lam_q1[layer].astype(F32) * lam_k1[layer].astype(F32)))
           - jnp.exp(jnp.sum(lam_q2[layer].astype(F32) * lam_k2[layer].astype(F32)))
           + LAM_INIT).reshape(1)

    p, lf = _inproj(x2d, sc1, sh1, g_mix[layer][None, :], w_in[layer].astype(BF16), qk_gain, lbs, g64, seq)

    nt = seq // T_ATT
    pos_sub = positions.astype(I32).reshape(batch * seq // T_SUB, T_SUB)
    smin = jnp.min(pos_sub, axis=1)
    smax = jnp.max(pos_sub, axis=1)
    posq = positions.astype(I32).reshape(batch, nt, 1, T_ATT)
    posk = positions.astype(I32).reshape(batch, seq, 1)
    rb_t = rel_bias.astype(F32).T * LOG2E
    rb_tab = jnp.zeros((N_HEADS, LANES), F32).at[:, :REL_BUCKETS].set(rb_t)
    oa = _attention(p, smin, smax, posq, posk, rb_tab, rb_t, lam,
                    g_sub[layer][:, None].astype(F32), batch, seq)

    ci = jnp.arange(C_HGRN)
    tril = (ci[None, :] <= ci[:, None]).astype(BF16)
    triu = (ci[None, :] >= ci[:, None]).astype(BF16)
    oh = _hgrn(p, lf, tril, triu, g_hgrn[layer][None, :].astype(F32), batch, seq)

    half = d // 2
    w_out_bf = w_out[layer].astype(BF16)
    wr_t = w_router[layer].astype(F32).T
    wr_hi = wr_t.astype(BF16)
    wr_lo = (wr_t - wr_hi.astype(F32)).astype(BF16)
    ti = jnp.arange(TM_OUT)
    upper = (ti[:, None] < ti[None, :]).astype(BF16)
    ones = jnp.ones((TM_OUT, LANES), BF16)
    x1, hp, eidx, slot, gate, cnt = _outproj(
        oa, oh, w_out_bf, x2d, gt1, g_ffn[layer][None, :], sc2, sh2,
        wr_hi, wr_lo, router_bias[layer].astype(F32)[:, None], upper, ones, seq)

    tb = TB_EXP
    counts = cnt[:, 0].astype(I32)
    padded = (counts + tb - 1) // tb * tb
    pends = jnp.cumsum(padded)
    pstart = (pends - padded).astype(I32)
    n_blocks = (n * TOP_K) // tb + N_EXPERTS
    n_valid = (pends[-1] // tb).astype(I32).reshape(1)
    blk_start = jnp.arange(n_blocks, dtype=I32) * tb
    block_e = jnp.minimum(jnp.sum(pends[None, :] <= blk_start[:, None], axis=1), N_EXPERTS - 1).astype(I32)
    pfill = (pstart + counts).astype(I32)
    pend = pends.astype(I32)
    dest = _dest(pstart, eidx, slot)
    has_rows = counts > 0
    group_e = jnp.nonzero(has_rows, size=N_EXPERTS, fill_value=0)[0].astype(I32)
    n_groups = jnp.sum(has_rows).astype(I32).reshape(1)

    xs = _scatter(pfill, pend, n_valid, dest, hp, n_blocks * tb)
    y = _experts(block_e, n_valid, group_e, n_groups, xs,
                 w_exp_gate[layer], w_exp_up[layer], w_exp_down[layer], n_blocks)
    out = _combine(dest, x1, hp, gate.T, gt2,
                   w_sh_gate[layer].astype(BF16), w_sh_up[layer].astype(BF16),
                   w_sh_down[layer].astype(BF16), y, seq)
    return out.reshape(batch, seq, d)
```

```python
import functools
import math

import jax
import jax.numpy as jnp
from jax import lax
from jax.experimental import pallas as pl
from jax.experimental.pallas import tpu as pltpu

F32 = jnp.float32
BF16 = jnp.bfloat16
I32 = jnp.int32
U32 = jnp.uint32

D_MODEL = 2048
N_HEADS = 8
QK_DIM = 64
HEAD_DIM = 128
SEG = 1024
N_SEG = 8
REL_BUCKETS = 32
REL_MAX_DIST = 128
N_EXPERTS = 256
TOP_K = 8
N_GROUPS = 8
TOPK_GROUPS = 4
GROUP_SIZE = N_EXPERTS // N_GROUPS
EXPERT_DIM = 512
ROUTED_SCALE = 2.5
EPS = 1e-6
LAM_INIT = 0.8 - 0.6 * math.exp(-0.3 * 0)
LOG2E = math.log2(math.e)

LANES = 128
VMEM_LIMIT = 56 * 1024 * 1024

TM_IN = 512
IN_CHUNK = 256
T_ATT = 512
T_KEY = 1024
T_SUB = 128
V_PAD = 16
C_HGRN = 64
HGRN_GROUP = 32
TM_OUT = 256
T_ROW = 256
TB_EXP = 256
W_SLOTS = 3
W_AHEAD = 2
W_PRIORITY = 1
H_SLOTS = 3
NEG_BIG = -1e30
EXP_CLAMP = 80.0
FAST_BOUND = 60.0
BOUND_SLACK = 1.02
BOUND_PAD = 0.01

NT_DIMS = (((1,), (1,)), ((), ()))


def _silu(x):
    return x * jax.nn.sigmoid(x)


def _pack_pair(lo_f32, hi_f32):
    lo = lax.bitcast_convert_type(lo_f32.astype(BF16).astype(F32), U32)
    hi = lax.bitcast_convert_type(hi_f32.astype(BF16).astype(F32), U32)
    return (hi & jnp.uint32(0xFFFF0000)) | (lo >> 16)


def _unpack_pair(word):
    lo = lax.bitcast_convert_type(word << 16, F32)
    hi = lax.bitcast_convert_type(word & jnp.uint32(0xFFFF0000), F32)
    return lo, hi


def _ada_kernel(c_ref, w_ref, b_ref, o_ref):
    a = _silu(c_ref[...]).astype(BF16)
    o_ref[...] = jnp.dot(a, w_ref[...].astype(BF16), preferred_element_type=F32) + b_ref[...]


def _ada(c_pad, w, b):
    d, n = w.shape
    tn = 1024
    return pl.pallas_call(
        _ada_kernel,
        grid=(n // tn,),
        in_specs=[pl.BlockSpec((8, d), lambda j: (0, 0)),
                  pl.BlockSpec((d, tn), lambda j: (0, j)),
                  pl.BlockSpec((1, tn), lambda j: (0, j))],
        out_specs=pl.BlockSpec((8, tn), lambda j: (0, j)),
        out_shape=jax.ShapeDtypeStruct((8, n), F32),
        compiler_params=pltpu.CompilerParams(dimension_semantics=("arbitrary",),
                                             vmem_limit_bytes=VMEM_LIMIT),
        name="ada",
    )(c_pad, w, b)


def _inproj_kernel(x_ref, sc_ref, sh_ref, gmix_ref, w_ref, qkg_ref, lb_ref, g64_ref,
                   p_ref, lf_ref, h_scr):
    j = pl.program_id(1)

    @pl.when(j == 0)
    def _():
        x = x_ref[...]
        ms = jnp.mean(x * x, axis=-1, keepdims=True)
        y = x * lax.rsqrt(ms + EPS) * gmix_ref[...]
        h_scr[...] = (y * (1.0 + sc_ref[0]) + sh_ref[0]).astype(BF16)

    def chunks(epilogue):
        for c in range(SEG // IN_CHUNK):
            sl = slice(c * IN_CHUNK, (c + 1) * IN_CHUNK)
            epilogue(sl, jnp.dot(h_scr[...], w_ref[:, sl], preferred_element_type=F32))

    @pl.when(j < 2)
    def _():
        acc = jnp.dot(h_scr[...], w_ref[...], preferred_element_type=F32)
        gain = qkg_ref[0]
        for c in range(SEG // LANES):
            sl = slice(c * LANES, (c + 1) * LANES)
            xs = acc[:, sl]
            ms = jnp.dot((xs * xs).astype(BF16), g64_ref[...], preferred_element_type=F32)
            p_ref[:, sl] = (xs * lax.rsqrt(ms + EPS) * gain[:, sl]).astype(BF16)

    @pl.when((j == 2) | (j == 4))
    def _():
        def plain(sl, acc):
            p_ref[:, sl] = acc.astype(BF16)

        chunks(plain)

    @pl.when((j == 3) | (j == 7))
    def _():
        def silu(sl, acc):
            p_ref[:, sl] = _silu(acc).astype(BF16)

        chunks(silu)

    @pl.when((j == 5) | (j == 6))
    def _():
        def log_gate(sl, z):
            lb = lb_ref[0, :, sl]
            f = lb + (1.0 - lb) * jax.nn.sigmoid(z)
            lf_ref[:, sl] = jnp.log(f)
            p_ref[:, sl] = z.astype(BF16)

        chunks(log_gate)


def _inproj(x2d, sc1, sh1, g_mix, w_in_bf, qk_gain, lbs, g64, seq):
    n, d = x2d.shape
    tm = TM_IN
    tiles_per_batch = seq // tm
    return pl.pallas_call(
        _inproj_kernel,
        grid=(n // tm, N_SEG),
        in_specs=[
            pl.BlockSpec((tm, d), lambda i, j: (i, 0)),
            pl.BlockSpec((1, 1, d), lambda i, j: (i // tiles_per_batch, 0, 0)),
            pl.BlockSpec((1, 1, d), lambda i, j: (i // tiles_per_batch, 0, 0)),
            pl.BlockSpec((1, d), lambda i, j: (0, 0)),
            pl.BlockSpec((d, SEG), lambda i, j: (0, j)),
            pl.BlockSpec((1, 1, SEG), lambda i, j: (jnp.minimum(j, 1), 0, 0)),
            pl.BlockSpec((1, 1, SEG), lambda i, j: (jnp.clip(j - 5, 0, 1), 0, 0)),
            pl.BlockSpec((LANES, LANES), lambda i, j: (0, 0)),
        ],
        out_specs=[
            pl.BlockSpec((tm, SEG), lambda i, j: (i, j)),
            pl.BlockSpec((tm, SEG), lambda i, j: (i, jnp.clip(j - 5, 0, 1))),
        ],
        out_shape=[jax.ShapeDtypeStruct((n, N_SEG * SEG), BF16),
                   jax.ShapeDtypeStruct((n, 2 * SEG), F32)],
        scratch_shapes=[pltpu.VMEM((tm, d), BF16)],
        compiler_params=pltpu.CompilerParams(dimension_semantics=("arbitrary", "arbitrary"),
                                             vmem_limit_bytes=VMEM_LIMIT),
        name="inproj",
    )(x2d, sc1, sh1, g_mix, w_in_bf, qk_gain, lbs, g64)


def _t5_bias_tile(pos_q, pos_k, table):
    half = REL_BUCKETS // 2
    max_exact = half // 2
    rel = pos_k - pos_q
    n = jnp.abs(rel)
    nf = jnp.maximum(n, 1).astype(F32)
    large = max_exact + (jnp.log(nf / max_exact) / math.log(REL_MAX_DIST / max_exact)
                         * (half - max_exact)).astype(I32)
    large = jnp.minimum(large, half - 1)
    bucket = jnp.where(rel > 0, half, 0) + jnp.where(n < max_exact, n, large)
    rows = bucket.shape[0]
    tbl = jnp.broadcast_to(table, (rows, LANES))
    cols = [jnp.take_along_axis(tbl, bucket[:, c * LANES:(c + 1) * LANES], axis=1)
            for c in range(bucket.shape[1] // LANES)]
    return jnp.concatenate(cols, axis=1)


def _attn_kernel(smin_ref, smax_ref, q_ref, k_ref, v_ref, posq_ref, posk_ref, rbt_ref, rbx_ref, rb_ref, lam_ref,
                 gsub_ref, o_ref, vt_scr, s_a, s_b, cm_a, cm_b, p_a, p_b, al_a, al_b, m_scr, a_scr,
                 kmax_scr):
    b = pl.program_id(0)
    h = pl.program_id(1)
    i = pl.program_id(2)
    tq = T_ATT
    tk = T_KEY
    n_sub = tk // T_SUB
    nq_sub = tq // T_SUB
    ntk = k_ref.shape[0] // tk
    n_pairs = ntk // 2
    subs_per_batch = k_ref.shape[0] // T_SUB

    @pl.when(i == 0)
    def _():
        ones_row = jnp.where(lax.broadcasted_iota(I32, (V_PAD, tk), 0) == 0, 1.0, 0.0).astype(BF16)
        kmax_scr[...] = jnp.zeros(kmax_scr.shape, F32)

        def tr(c, carry):
            r0 = pl.multiple_of(c * tk, tk)
            vt_scr[c, :HEAD_DIM, :] = v_ref[pl.ds(r0, tk), :].astype(F32).T.astype(BF16)
            vt_scr[c, HEAD_DIM:, :] = ones_row
            kf = k_ref[pl.ds(r0, tk), :].astype(F32)
            for mp in range(2):
                km = kf[:, mp * QK_DIM:(mp + 1) * QK_DIM]
                nk = jnp.dot(km * km, jnp.ones((QK_DIM, LANES), F32), preferred_element_type=F32)
                kmax_scr[mp] = jnp.maximum(kmax_scr[mp], jnp.max(nk, axis=0, keepdims=True))
            return carry

        lax.fori_loop(0, ntk, tr, 0)

    q = q_ref[...]
    qs = (q[:, :QK_DIM], q[:, QK_DIM:])
    m_scr[...] = jnp.full(m_scr.shape, NEG_BIG, F32)
    a_scr[...] = jnp.zeros(a_scr.shape, F32)

    sub0 = b * subs_per_batch + i * nq_sub
    q_lo = smin_ref[sub0]
    q_hi = smax_ref[sub0]
    for u in range(1, nq_sub):
        q_lo = jnp.minimum(q_lo, smin_ref[sub0 + u])
        q_hi = jnp.maximum(q_hi, smax_ref[sub0 + u])
    c_pos = rb_ref[h, REL_BUCKETS - 1]
    c_neg = rb_ref[h, REL_BUCKETS // 2 - 1]
    pos_q = posq_ref[0, 0]

    def classify(j, u):
        ksub = b * subs_per_batch + j * n_sub + u
        lo = smin_ref[ksub] - q_hi
        hi = smax_ref[ksub] - q_lo
        far = (lo >= REL_MAX_DIST) | (hi <= -REL_MAX_DIST)
        shift = jnp.where(lo >= REL_MAX_DIST, c_pos, jnp.where(hi <= -REL_MAX_DIST, c_neg, 0.0))
        return far, shift

    def scores(j, s_ref, cm_ref):
        kk = k_ref[pl.ds(pl.multiple_of(j * tk, tk), tk), :]
        ks = (kk[:, :QK_DIM], kk[:, QK_DIM:])
        for mp in range(2):
            s = lax.dot_general(ks[mp], qs[mp], NT_DIMS, preferred_element_type=F32)
            s_ref[mp] = s
            for u in range(n_sub):
                _, shift = classify(j, u)
                cm_ref[mp, u] = jnp.max(s[u * T_SUB:(u + 1) * T_SUB], axis=0, keepdims=True) + shift

    def fixup(j, s_ref, cm_ref):
        for u in range(n_sub):
            far, _ = classify(j, u)

            @pl.when(jnp.logical_not(far))
            def _():
                rows = pl.ds(u * T_SUB, T_SUB)
                pos_k = posk_ref[0, pl.ds(pl.multiple_of(j * tk + u * T_SUB, T_SUB), T_SUB), :]
                bias = _t5_bias_tile(pos_q, pos_k, rbt_ref[pl.ds(h, 1), :])
                for mp in range(2):
                    sb = s_ref[mp, rows, :] + bias
                    s_ref[mp, rows, :] = sb
                    cm_ref[mp, u] = jnp.max(sb, axis=0, keepdims=True)

    def soft(j, s_ref, cm_ref, p_ref, al_ref):
        for mp in range(2):
            m_old = m_scr[mp]
            m_new = m_old
            for u in range(n_sub):
                m_new = jnp.maximum(m_new, cm_ref[mp, u])
            for u in range(n_sub):
                rows = pl.ds(u * T_SUB, T_SUB)
                _, shift = classify(j, u)
                p_ref[mp, rows, :] = jnp.exp2(s_ref[mp, rows, :] - (m_new - shift)).astype(BF16)
            m_scr[mp] = m_new
            al_ref[mp] = jnp.exp2(m_old - m_new)

    def pv(j, p_ref, al_ref):
        vt = vt_scr[j]
        for mp in range(2):
            a_scr[mp] = al_ref[mp] * a_scr[mp] + jnp.dot(vt, p_ref[mp], preferred_element_type=F32)

    sbuf = ((s_a, cm_a), (s_b, cm_b))
    pbuf = ((p_a, al_a), (p_b, al_b))
    last = 2 * (n_pairs - 1)

    def online_path():
        def step(j, par):
            scores(j + 2, *sbuf[par])
            soft(j + 1, *sbuf[1 - par], *pbuf[1 - par])
            pv(j, *pbuf[par])
            fixup(j + 2, *sbuf[par])

        scores(0, *sbuf[0])
        fixup(0, *sbuf[0])
        scores(1, *sbuf[1])
        soft(0, *sbuf[0], *pbuf[0])
        fixup(1, *sbuf[1])

        def body(jj, carry):
            step(2 * jj, 0)
            step(2 * jj + 1, 1)
            return carry

        lax.fori_loop(0, n_pairs - 1, body, 0)
        soft(last + 1, *sbuf[1], *pbuf[1])
        pv(last, *pbuf[0])
        pv(last + 1, *pbuf[1])

    def col_bound(mp):
        qf = qs[mp].astype(F32)
        nq = lax.dot_general(jnp.ones((8, QK_DIM), F32), qf * qf, NT_DIMS,
                             preferred_element_type=F32)[0:1, :]
        return jnp.sqrt(nq * kmax_scr[mp, :, 0:1]) * BOUND_SLACK

    b_max = rb_ref[h, 0]
    for e in range(1, REL_BUCKETS):
        b_max = jnp.maximum(b_max, rb_ref[h, e])
    bounds = [col_bound(mp) + (b_max + BOUND_PAD) for mp in range(2)]
    bound_max = jnp.max(jnp.maximum(bounds[0], bounds[1]))

    def fast_scores(j, p_ref):
        kk = k_ref[pl.ds(pl.multiple_of(j * tk, tk), tk), :]
        ks = (kk[:, :QK_DIM], kk[:, QK_DIM:])
        for mp in range(2):
            s = lax.dot_general(ks[mp], qs[mp], NT_DIMS, preferred_element_type=F32)
            for u in range(n_sub):
                _, shift = classify(j, u)
                p_ref[mp, pl.ds(u * T_SUB, T_SUB), :] = jnp.exp2(
                    s[u * T_SUB:(u + 1) * T_SUB] - (bounds[mp] - shift)).astype(BF16)

    def fast_fixup(j, p_ref):
        fars = [classify(j, u)[0] for u in range(n_sub)]
        all_far = fars[0]
        for f in fars[1:]:
            all_far = all_far & f

        @pl.when(jnp.logical_not(all_far))
        def _():
            for u in range(n_sub):
                @pl.when(jnp.logical_not(fars[u]))
                def _():
                    rows = pl.ds(u * T_SUB, T_SUB)
                    pos_k = posk_ref[0, pl.ds(pl.multiple_of(j * tk + u * T_SUB, T_SUB), T_SUB), :]
                    scale = _t5_bias_tile(pos_q, pos_k, rbx_ref[pl.ds(h, 1), :]).astype(BF16)
                    for mp in range(2):
                        p_ref[mp, rows, :] = p_ref[mp, rows, :] * scale

    def fast_pv(j, p_ref):
        vt = vt_scr[j]
        for mp in range(2):
            a_scr[mp] = a_scr[mp] + jnp.dot(vt, p_ref[mp], preferred_element_type=F32)

    def fast_path():
        pb = (p_a, p_b)

        def step(j, par):
            fast_scores(j + 1, pb[1 - par])
            fast_pv(j, pb[par])
            fast_fixup(j + 1, pb[1 - par])

        fast_scores(0, pb[0])
        fast_fixup(0, pb[0])

        def body(jj, carry):
            step(2 * jj, 0)
            step(2 * jj + 1, 1)
            return carry

        lax.fori_loop(0, n_pairs - 1, body, 0)
        step(last, 0)
        fast_pv(last + 1, pb[1])

    use_fast = bound_max <= FAST_BOUND

    @pl.when(use_fast)
    def _():
        fast_path()

    @pl.when(jnp.logical_not(use_fast))
    def _():
        online_path()

    num = [a_scr[mp, :HEAD_DIM, :] / a_scr[mp, HEAD_DIM:HEAD_DIM + 1, :] for mp in range(2)]
    o = num[0] - lam_ref[0] * num[1]
    ms = jnp.mean(o * o, axis=0, keepdims=True)
    o = o * lax.rsqrt(ms + EPS) * (gsub_ref[...] * (1.0 - LAM_INIT))
    o_ref[...] = o.T.astype(BF16)


def _attention(p, smin, smax, posq, posk, rb_tab, rb_t, lam, g_sub, batch, seq):
    t = T_ATT
    tk = T_KEY
    nt = seq // t
    ntk = seq // tk
    assert seq % (2 * tk) == 0
    n = batch * seq
    n_sub = tk // T_SUB
    va = HEAD_DIM + V_PAD
    grid_spec = pltpu.PrefetchScalarGridSpec(
        num_scalar_prefetch=2,
        grid=(batch, N_HEADS, nt),
        in_specs=[
            pl.BlockSpec((t, HEAD_DIM), lambda b, h, i, *_: (b * nt + i, h)),
            pl.BlockSpec((seq, HEAD_DIM), lambda b, h, i, *_: (b, N_HEADS + h)),
            pl.BlockSpec((seq, HEAD_DIM), lambda b, h, i, *_: (b, 2 * N_HEADS + h)),
            pl.BlockSpec((1, 1, 1, t), lambda b, h, i, *_: (b, i, 0, 0)),
            pl.BlockSpec((1, seq, 1), lambda b, h, i, *_: (b, 0, 0)),
            pl.BlockSpec((N_HEADS, LANES), lambda b, h, i, *_: (0, 0)),
            pl.BlockSpec((N_HEADS, LANES), lambda b, h, i, *_: (0, 0)),
            pl.BlockSpec(memory_space=pltpu.SMEM),
            pl.BlockSpec(memory_space=pltpu.SMEM),
            pl.BlockSpec((HEAD_DIM, 1), lambda b, h, i, *_: (0, 0)),
        ],
        out_specs=pl.BlockSpec((t, HEAD_DIM), lambda b, h, i, *_: (b * nt + i, h)),
        scratch_shapes=[pltpu.VMEM((ntk, va, tk), BF16),
                        pltpu.VMEM((2, tk, t), F32),
                        pltpu.VMEM((2, tk, t), F32),
                        pltpu.VMEM((2, n_sub, 1, t), F32),
                        pltpu.VMEM((2, n_sub, 1, t), F32),
                        pltpu.VMEM((2, tk, t), BF16),
                        pltpu.VMEM((2, tk, t), BF16),
                        pltpu.VMEM((2, 1, t), F32),
                        pltpu.VMEM((2, 1, t), F32),
                        pltpu.VMEM((2, 1, t), F32),
                        pltpu.VMEM((2, va, t), F32),
                        pltpu.VMEM((2, 1, LANES), F32)],
    )
    return pl.pallas_call(
        _attn_kernel,
        grid_spec=grid_spec,
        out_shape=jax.ShapeDtypeStruct((n, N_HEADS * HEAD_DIM), BF16),
        compiler_params=pltpu.CompilerParams(
            dimension_semantics=("arbitrary", "arbitrary", "arbitrary"),
            vmem_limit_bytes=VMEM_LIMIT),
        name="attn",
    )(smin, smax, p, p, p, posq, posk, rb_tab, jnp.exp2(rb_tab), rb_t, lam, g_sub)


def _hgrn_group(r0, forward, q_ref, v_ref, lf_ref, tri_ref, st):
    c = C_HGRN
    order = range(HGRN_GROUP) if forward else range(HGRN_GROUP - 1, -1, -1)
    rows = [pl.ds(r0 + k * c, c) for k in order]
    tri = tri_ref[...]
    row = lax.broadcasted_iota(I32, (c, c), 0)
    col = lax.broadcasted_iota(I32, (c, c), 1)
    keep = (col <= row) if forward else (col >= row)

    gs = [lf_ref[r, :] for r in rows]
    bsums = []
    for g in gs:
        g_hi = g.astype(BF16)
        g_lo = (g - g_hi.astype(F32)).astype(BF16)
        bsums.append(jnp.dot(tri, g_hi, preferred_element_type=F32)
                     + jnp.dot(tri, g_lo, preferred_element_type=F32))

    q_in, q_t, k_t, k_st, v_t, vs, decay = [], [], [], [], [], [], []
    for r, g, bsum in zip(rows, gs, bsums):
        if forward:
            ref = bsum[c // 2 - 1:c // 2, :]
            b_end = bsum[c - 1:c, :]
        else:
            ref = bsum[c // 2:c // 2 + 1, :]
            b_end = bsum[0:1, :]
        q = q_ref[r, :].astype(F32)
        v = v_ref[r, :]
        kf = 1.0 - jnp.exp(g)
        q_in.append((q * jnp.exp(bsum)).astype(BF16))
        q_t.append((q * jnp.exp(jnp.minimum(bsum - ref, EXP_CLAMP))).astype(BF16))
        k_t.append((kf * jnp.exp(jnp.minimum(ref - bsum, EXP_CLAMP))).astype(BF16))
        k_st.append((kf * jnp.exp(b_end - bsum)).astype(BF16))
        v_t.append(v.astype(F32).T.astype(BF16))
        vs.append(v)
        decay.append(jnp.exp(b_end))

    scores = [lax.dot_general(a, b, NT_DIMS, preferred_element_type=F32) for a, b in zip(q_t, k_t)]
    st_add = [jnp.dot(a, b, preferred_element_type=F32) for a, b in zip(v_t, k_st)]
    intra = [jnp.dot(jnp.where(keep, s, 0.0).astype(BF16), v, preferred_element_type=F32)
             for s, v in zip(scores, vs)]

    outs = []
    for k in range(HGRN_GROUP):
        o = lax.dot_general(q_in[k], st.astype(BF16), NT_DIMS, preferred_element_type=F32) + intra[k]
        st = st * decay[k] + st_add[k]
        outs.append((rows[k], o))
    return outs, st


def _hgrn_kernel(q_ref, v_ref, g_ref, lff_ref, lfb_ref, tril_ref, triu_ref, gh_ref, o_ref, of_scr):
    rows_per_group = C_HGRN * HGRN_GROUP
    n_groups = q_ref.shape[0] // rows_per_group
    st0 = jnp.zeros((HEAD_DIM, HEAD_DIM), F32)

    def fwd(gi, st):
        r0 = pl.multiple_of(gi * rows_per_group, rows_per_group)
        outs, st = _hgrn_group(r0, True, q_ref, v_ref, lff_ref, tril_ref, st)
        for r, o in outs:
            of_scr[r, :] = o
        return st

    lax.fori_loop(0, n_groups, fwd, st0)

    def bwd(gi, st):
        r0 = pl.multiple_of((n_groups - 1 - gi) * rows_per_group, rows_per_group)
        outs, st = _hgrn_group(r0, False, q_ref, v_ref, lfb_ref, triu_ref, st)
        for r, o in outs:
            o = of_scr[r, :] + o
            ms = jnp.mean(o * o, axis=-1, keepdims=True)
            y = o * lax.rsqrt(ms + EPS) * gh_ref[...] * g_ref[r, :].astype(F32)
            o_ref[r, :] = y.astype(BF16)
        return st

    lax.fori_loop(0, n_groups, bwd, st0)


def _hgrn(p, lf, tril, triu, g_hgrn, batch, seq):
    n = batch * seq
    c = C_HGRN
    blk = lambda off: pl.BlockSpec((seq, HEAD_DIM), lambda b, h: (b, off + h))
    return pl.pallas_call(
        _hgrn_kernel,
        grid=(batch, N_HEADS),
        in_specs=[blk(3 * N_HEADS), blk(4 * N_HEADS), blk(7 * N_HEADS), blk(0), blk(N_HEADS),
                  pl.BlockSpec((c, c), lambda b, h: (0, 0)),
                  pl.BlockSpec((c, c), lambda b, h: (0, 0)),
                  pl.BlockSpec((1, HEAD_DIM), lambda b, h: (0, 0))],
        out_specs=pl.BlockSpec((seq, HEAD_DIM), lambda b, h: (b, h)),
        out_shape=jax.ShapeDtypeStruct((n, N_HEADS * HEAD_DIM), BF16),
        scratch_shapes=[pltpu.VMEM((seq, HEAD_DIM), F32)],
        compiler_params=pltpu.CompilerParams(dimension_semantics=("arbitrary", "arbitrary"),
                                             vmem_limit_bytes=VMEM_LIMIT),
        name="hgrn",
    )(p, p, p, lf, lf, tril, triu, g_hgrn)


def _col_max(x):
    return jnp.max(x, axis=0, keepdims=True)


def _outproj_kernel(oa_ref, oh_ref, w_ref, x_ref, gt_ref, gffn_ref, sc_ref, sh_ref,
                    wr_hi_ref, wr_lo_ref, rbias_ref, upper_ref, ones_ref,
                    x1_ref, hp_ref, eidx_ref, slot_ref, gate_ref, cnt_ref, cnt_scr):
    i = pl.program_id(0)
    tm = x_ref.shape[0]

    @pl.when(i == 0)
    def _():
        cnt_scr[...] = jnp.zeros(cnt_scr.shape, F32)

    mixed = jnp.concatenate([oa_ref[...], oh_ref[...]], axis=1)
    acc = jnp.dot(mixed, w_ref[...], preferred_element_type=F32)
    x1 = x_ref[...] + gt_ref[0] * acc
    x1_ref[...] = x1
    ms = jnp.mean(x1 * x1, axis=-1, keepdims=True)
    h2 = x1 * lax.rsqrt(ms + EPS) * gffn_ref[...] * (1.0 + sc_ref[0]) + sh_ref[0]
    half = h2.shape[1] // 2
    hp_ref[...] = _pack_pair(h2[:, :half], h2[:, half:])

    h_hi = h2.astype(BF16)
    h_lo = (h2 - h_hi.astype(F32)).astype(BF16)
    wr_hi = wr_hi_ref[...]
    logits = (lax.dot_general(wr_hi, h_hi, NT_DIMS, preferred_element_type=F32)
              + lax.dot_general(wr_hi, h_lo, NT_DIMS, preferred_element_type=F32)
              + lax.dot_general(wr_lo_ref[...], h_hi, NT_DIMS, preferred_element_type=F32))
    scores = jax.nn.sigmoid(logits)
    biased = scores + rbias_ref[...]

    gs = []
    for g in range(N_GROUPS):
        blk = biased[g * GROUP_SIZE:(g + 1) * GROUP_SIZE, :]
        top1 = _col_max(blk)
        eq = blk == top1
        n_eq = jnp.sum(eq.astype(F32), axis=0, keepdims=True)
        second = _col_max(jnp.where(eq, -jnp.inf, blk))
        gs.append(top1 + jnp.where(n_eq > 1.0, top1, second))
    gsm = jnp.concatenate(gs, axis=0)
    giota = lax.broadcasted_iota(I32, gsm.shape, 0)
    gsel = jnp.zeros(gsm.shape, F32)
    for _ in range(TOPK_GROUPS):
        top = _col_max(gsm)
        idx = jnp.min(jnp.where(gsm == top, giota, N_GROUPS), axis=0, keepdims=True)
        pick = giota == idx
        gsel = jnp.where(pick, 1.0, gsel)
        gsm = jnp.where(pick, -jnp.inf, gsm)
    emask = jnp.concatenate(
        [jnp.broadcast_to(gsel[g:g + 1, :], (GROUP_SIZE, tm)) for g in range(N_GROUPS)], axis=0)
    masked = jnp.where(emask > 0.5, biased, -jnp.inf)

    eiota = lax.broadcasted_iota(I32, masked.shape, 0)
    idxs, gates = [], []
    for _ in range(TOP_K):
        top = _col_max(masked)
        idx = jnp.min(jnp.where(masked == top, eiota, N_EXPERTS), axis=0, keepdims=True)
        pick = eiota == idx
        gates.append(jnp.sum(jnp.where(pick, scores, 0.0), axis=0, keepdims=True))
        idxs.append(idx)
        masked = jnp.where(pick, -jnp.inf, masked)
    gate = jnp.concatenate(gates, axis=0)
    gate = gate / jnp.sum(gate, axis=0, keepdims=True) * ROUTED_SCALE
    eidx = jnp.concatenate(idxs, axis=0)
    eidx_ref[...] = eidx
    gate_ref[...] = gate

    sel = jnp.zeros(masked.shape, F32)
    for k in range(TOP_K):
        sel = jnp.where(eiota == idxs[k], 1.0, sel)
    sel_bf = sel.astype(BF16)
    rank = jnp.dot(sel_bf, upper_ref[...], preferred_element_type=F32)
    base = cnt_scr[...]
    posn = base[:, :1] + rank
    slots = [jnp.sum(jnp.where(eiota == idxs[k], posn, 0.0), axis=0, keepdims=True)
             for k in range(TOP_K)]
    slot_ref[...] = jnp.concatenate(slots, axis=0).astype(I32)
    new_cnt = base + jnp.dot(sel_bf, ones_ref[...], preferred_element_type=F32)
    cnt_scr[...] = new_cnt
    cnt_ref[...] = new_cnt


def _outproj(oa, oh, w_out, x2d, gt1, g_ffn, sc2, sh2, wr_hi, wr_lo, rbias, upper, ones, seq):
    n, d = x2d.shape
    tm = TM_OUT
    tiles_per_batch = seq // tm
    half = d // 2
    row = lambda w: pl.BlockSpec((tm, w), lambda i: (i, 0))
    const = lambda shape: pl.BlockSpec(shape, lambda i: tuple(0 for _ in shape))
    per_batch = pl.BlockSpec((1, 1, d), lambda i: (i // tiles_per_batch, 0, 0))
    tok = pl.BlockSpec((TOP_K, tm), lambda i: (0, i))
    return pl.pallas_call(
        _outproj_kernel,
        grid=(n // tm,),
        in_specs=[row(half), row(half), const((d, d)), row(d), per_batch,
                  const((1, d)), per_batch, per_batch,
                  const((N_EXPERTS, d)), const((N_EXPERTS, d)), const((N_EXPERTS, 1)),
                  const((tm, tm)), const((tm, LANES))],
        out_specs=[row(d), row(half), tok, tok, tok, const((N_EXPERTS, LANES))],
        out_shape=[jax.ShapeDtypeStruct((n, d), F32),
                   jax.ShapeDtypeStruct((n, half), U32),
                   jax.ShapeDtypeStruct((TOP_K, n), I32),
                   jax.ShapeDtypeStruct((TOP_K, n), I32),
                   jax.ShapeDtypeStruct((TOP_K, n), F32),
                   jax.ShapeDtypeStruct((N_EXPERTS, LANES), F32)],
        scratch_shapes=[pltpu.VMEM((N_EXPERTS, LANES), F32)],
        compiler_params=pltpu.CompilerParams(dimension_semantics=("arbitrary",),
                                             vmem_limit_bytes=VMEM_LIMIT),
        name="outproj",
    )(oa, oh, w_out, x2d, gt1, g_ffn, sc2, sh2, wr_hi, wr_lo, rbias, upper, ones)


def _dest_kernel(pstart_ref, e_ref, slot_ref, o_ref):
    e = e_ref[...]

    def body(x, acc):
        return acc + jnp.where(e == x, pstart_ref[x], 0)

    o_ref[...] = lax.fori_loop(0, N_EXPERTS, body, slot_ref[...], unroll=8)


def _dest(pstart, eidx, slot):
    k, n = eidx.shape
    tn = min(n, 2048)
    grid_spec = pltpu.PrefetchScalarGridSpec(
        num_scalar_prefetch=1,
        grid=(n // tn,),
        in_specs=[pl.BlockSpec((k, tn), lambda i, *_: (0, i)),
                  pl.BlockSpec((k, tn), lambda i, *_: (0, i))],
        out_specs=pl.BlockSpec((k, tn), lambda i, *_: (0, i)),
    )
    return pl.pallas_call(
        _dest_kernel,
        grid_spec=grid_spec,
        out_shape=jax.ShapeDtypeStruct((k, n), I32),
        compiler_params=pltpu.CompilerParams(dimension_semantics=("arbitrary",)),
        name="dest",
    )(pstart, eidx, slot)


def _scatter_kernel(pfill_ref, pend_ref, nv_ref, dest_ref, h_hbm, xs_ref, zero_scr, h_buf, sems, lsem,
                    zsem):
    i = pl.program_id(0)
    n_steps = pl.num_programs(0)
    ts = h_buf.shape[1]
    tb = zero_scr.shape[0]
    n_tail = xs_ref.shape[0] // tb - nv_ref[0]

    def pad_fill(e, wait):
        def go(src, dst):
            cp = pltpu.make_async_copy(src, dst, zsem)
            cp.wait() if wait else cp.start()

        start = pfill_ref[e]
        end = pend_ref[e]
        head = jnp.minimum((-start) & 7, end - start)
        for r in range(7):
            @pl.when(r < head)
            def _():
                go(zero_scr.at[pl.ds(0, 1)], xs_ref.at[pl.ds(start + r, 1)])
        off = start + head
        rem = end - off
        size = tb // 2
        while size >= 8:
            cond = (rem & size) != 0

            @pl.when(cond)
            def _():
                go(zero_scr.at[pl.ds(0, size)], xs_ref.at[pl.ds(pl.multiple_of(off, 8), size)])
            off = off + jnp.where(cond, size, 0)
            size //= 2

    @pl.when(i == 0)
    def _():
        zero_scr[...] = jnp.zeros(zero_scr.shape, U32)

        def fill(e, carry):
            pad_fill(e, False)
            return carry

        lax.fori_loop(0, N_EXPERTS, fill, 0)

        def fill_tail(j, carry):
            start = pl.multiple_of((nv_ref[0] + j) * tb, tb)
            pltpu.make_async_copy(zero_scr, xs_ref.at[pl.ds(start, tb)], zsem).start()
            return carry

        lax.fori_loop(0, n_tail, fill_tail, 0)

        def drain(e, carry):
            pad_fill(e, True)
            return carry

        lax.fori_loop(0, N_EXPERTS, drain, 0)

        def drain_tail(j, carry):
            pltpu.make_async_copy(zero_scr, xs_ref.at[pl.ds(0, tb)], zsem).wait()
            return carry

        lax.fori_loop(0, n_tail, drain_tail, 0)

    def load(j):
        return pltpu.make_async_copy(h_hbm.at[pl.ds(pl.multiple_of(j * ts, ts), ts)],
                                     h_buf.at[lax.rem(j, H_SLOTS)], lsem.at[lax.rem(j, H_SLOTS)])

    def drain_scatters(j):
        sl = lax.rem(j, H_SLOTS)
        for k in range(TOP_K):
            pltpu.make_async_copy(h_buf.at[sl], xs_ref.at[pl.ds(0, ts)], sems.at[sl]).wait()

    @pl.when(i == 0)
    def _():
        load(0).start()

    @pl.when(i + 1 < n_steps)
    def _():
        load(i + 1).start()

    load(i).wait()
    cur = lax.rem(i, H_SLOTS)

    for t in range(ts):
        for k in range(TOP_K):
            pltpu.make_async_copy(h_buf.at[cur, pl.ds(t, 1)], xs_ref.at[pl.ds(dest_ref[k, t], 1)],
                                  sems.at[cur]).start(priority=k % 2)

    @pl.when(i >= 1)
    def _():
        drain_scatters(i - 1)

    @pl.when(i == n_steps - 1)
    def _():
        drain_scatters(i)


def _scatter(pfill, pend, n_valid, dest, hp, n_rows):
    n, w = hp.shape
    ts = T_ROW
    grid_spec = pltpu.PrefetchScalarGridSpec(
        num_scalar_prefetch=3,
        grid=(n // ts,),
        in_specs=[pl.BlockSpec((TOP_K, ts), lambda i, *_: (0, i), memory_space=pltpu.SMEM),
                  pl.BlockSpec(memory_space=pl.ANY)],
        out_specs=pl.BlockSpec(memory_space=pl.ANY),
        scratch_shapes=[pltpu.VMEM((TB_EXP, w), U32), pltpu.VMEM((H_SLOTS, ts, w), U32),
                        pltpu.SemaphoreType.DMA((H_SLOTS,)), pltpu.SemaphoreType.DMA((H_SLOTS,)),
                        pltpu.SemaphoreType.DMA],
    )
    return pl.pallas_call(
        _scatter_kernel,
        grid_spec=grid_spec,
        out_shape=jax.ShapeDtypeStruct((n_rows, w), U32),
        compiler_params=pltpu.CompilerParams(dimension_semantics=("arbitrary",),
                                             vmem_limit_bytes=VMEM_LIMIT),
        name="scatter",
    )(pfill, pend, n_valid, dest, hp)


def _experts_kernel(be_ref, nv_ref, ge_ref, ng_ref, xs_ref, wg_hbm, wu_hbm, wd_hbm, y_ref,
                    wg_f, wu_f, wd_f, sems, gctr):
    i = pl.program_id(0)
    prev = jnp.maximum(i - 1, 0)
    valid = i < nv_ref[0]
    fresh = valid & ((i == 0) | (be_ref[i] != be_ref[prev]))

    def weight_copies(g, slot):
        e = ge_ref[g]
        return (pltpu.make_async_copy(wg_hbm.at[e], wg_f.at[slot], sems.at[slot, 0]),
                pltpu.make_async_copy(wu_hbm.at[e], wu_f.at[slot], sems.at[slot, 1]),
                pltpu.make_async_copy(wd_hbm.at[e], wd_f.at[slot], sems.at[slot, 2]))

    @pl.when(i == 0)
    def _():
        gctr[0] = 0
        for g in range(W_AHEAD):
            @pl.when(g < ng_ref[0])
            def _():
                for cp in weight_copies(g, g):
                    cp.start(priority=W_PRIORITY)

    @pl.when(fresh)
    def _():
        g = gctr[0]
        slot = lax.rem(g, W_SLOTS)
        for cp in weight_copies(g, slot):
            cp.wait()

        @pl.when(g + W_AHEAD < ng_ref[0])
        def _():
            for cp in weight_copies(g + W_AHEAD, lax.rem(g + W_AHEAD, W_SLOTS)):
                cp.start(priority=W_PRIORITY)

        gctr[0] = g + 1
        gctr[1] = slot

    @pl.when(valid)
    def _():
        slot = gctr[1]
        lo, hi = _unpack_pair(xs_ref[...])
        x = jnp.concatenate([lo, hi], axis=1)
        hg = jnp.dot(x, wg_f[slot], preferred_element_type=F32)
        hu = jnp.dot(x, wu_f[slot], preferred_element_type=F32)
        y = jnp.dot(_silu(hg) * hu, wd_f[slot], preferred_element_type=F32)
        half = y.shape[1] // 2
        y_ref[...] = _pack_pair(y[:, :half], y[:, half:])

    @pl.when(jnp.logical_not(valid))
    def _():
        y_ref[...] = jnp.zeros(y_ref.shape, U32)


def _experts(block_e, n_valid, group_e, n_groups, xs, w_gate, w_up, w_down, n_blocks):
    tb = TB_EXP
    w = xs.shape[1]
    _, d, f = w_gate.shape
    grid_spec = pltpu.PrefetchScalarGridSpec(
        num_scalar_prefetch=4,
        grid=(n_blocks,),
        in_specs=[pl.BlockSpec((tb, w), lambda i, be, nv, ge, ng: (jnp.minimum(i, nv[0] - 1), 0)),
                  pl.BlockSpec(memory_space=pl.ANY),
                  pl.BlockSpec(memory_space=pl.ANY),
                  pl.BlockSpec(memory_space=pl.ANY)],
        out_specs=pl.BlockSpec((tb, w), lambda i, be, nv, ge, ng: (i, 0)),
        scratch_shapes=[pltpu.VMEM((W_SLOTS, d, f), F32), pltpu.VMEM((W_SLOTS, d, f), F32),
                        pltpu.VMEM((W_SLOTS, f, d), F32),
                        pltpu.SemaphoreType.DMA((W_SLOTS, 3)), pltpu.SMEM((2,), I32)],
    )
    return pl.pallas_call(
        _experts_kernel,
        grid_spec=grid_spec,
        out_shape=jax.ShapeDtypeStruct((n_blocks * tb, w), U32),
        compiler_params=pltpu.CompilerParams(dimension_semantics=("arbitrary",),
                                             vmem_limit_bytes=VMEM_LIMIT),
        name="experts",
    )(block_e, n_valid, group_e, n_groups, xs, w_gate, w_up, w_down)


def _combine_kernel(dcur_ref, dnxt_ref, x1_ref, hp_ref, gate_ref, gt_ref,
                    wsg_ref, wsu_ref, wsd_ref, y_ref, o_ref, buf, sems):
    i = pl.program_id(0)
    tc = x1_ref.shape[0]
    slot = lax.rem(i, 2)

    def issue(d_ref, sl):
        for t in range(tc):
            for k in range(TOP_K):
                pltpu.make_async_copy(y_ref.at[pl.ds(d_ref[k, t], 1)], buf.at[sl, k, pl.ds(t, 1)],
                                      sems.at[sl]).start(priority=k % 2)

    def compute():
        lo, hi = _unpack_pair(hp_ref[...])
        x = jnp.concatenate([lo.astype(BF16), hi.astype(BF16)], axis=1)
        hg = jnp.dot(x, wsg_ref[...], preferred_element_type=F32)
        hu = jnp.dot(x, wsu_ref[...], preferred_element_type=F32)
        a = (_silu(hg) * hu).astype(BF16)
        shared = jnp.dot(a, wsd_ref[...], preferred_element_type=F32)
        half = shared.shape[1] // 2
        gate = gate_ref[...]
        r_lo = shared[:, :half]
        r_hi = shared[:, half:]
        for k in range(TOP_K):
            lo, hi = _unpack_pair(buf[slot, k])
            gk = gate[:, k:k + 1]
            r_lo = r_lo + gk * lo
            r_hi = r_hi + gk * hi
        gt = gt_ref[0]
        o_ref[:, :half] = x1_ref[:, :half] + gt[:, :half] * r_lo
        o_ref[:, half:] = x1_ref[:, half:] + gt[:, half:] * r_hi

    @pl.when(i == 0)
    def _():
        issue(dcur_ref, 0)

    for k in range(TOP_K):
        pltpu.make_async_copy(y_ref.at[pl.ds(0, tc)], buf.at[slot, k], sems.at[slot]).wait()

    @pl.when(i + 1 < pl.num_programs(0))
    def _():
        issue(dnxt_ref, 1 - slot)
        compute()

    @pl.when(i + 1 >= pl.num_programs(0))
    def _():
        compute()


def _combine(dest, x1, hp, gate_t, gt2, wsg, wsu, wsd, y, seq):
    n, d = x1.shape
    tc = T_ROW
    w = hp.shape[1]
    f = wsg.shape[1]
    tiles_per_batch = seq // tc
    last = n // tc - 1
    return pl.pallas_call(
        _combine_kernel,
        grid=(n // tc,),
        in_specs=[pl.BlockSpec((TOP_K, tc), lambda i: (0, i), memory_space=pltpu.SMEM),
                  pl.BlockSpec((TOP_K, tc), lambda i: (0, jnp.minimum(i + 1, last)),
                               memory_space=pltpu.SMEM),
                  pl.BlockSpec((tc, d), lambda i: (i, 0)),
                  pl.BlockSpec((tc, w), lambda i: (i, 0)),
                  pl.BlockSpec((tc, TOP_K), lambda i: (i, 0)),
                  pl.BlockSpec((1, 1, d), lambda i: (i // tiles_per_batch, 0, 0)),
                  pl.BlockSpec((d, f), lambda i: (0, 0)),
                  pl.BlockSpec((d, f), lambda i: (0, 0)),
                  pl.BlockSpec((f, d), lambda i: (0, 0)),
                  pl.BlockSpec(memory_space=pl.ANY)],
        out_specs=pl.BlockSpec((tc, d), lambda i: (i, 0)),
        scratch_shapes=[pltpu.VMEM((2, TOP_K, tc, w), U32), pltpu.SemaphoreType.DMA((2,))],
        out_shape=jax.ShapeDtypeStruct((n, d), F32),
        compiler_params=pltpu.CompilerParams(dimension_semantics=("arbitrary",),
                                             vmem_limit_bytes=VMEM_LIMIT),
        name="combine",
    )(dest, dest, x1, hp, gate_t, gt2, wsg, wsu, wsd, y)


def kernel(x, c, positions, rel_bias, hgrn_lb_logits, w_ada, b_ada, g_mix, w_in, g_q, g_k, lam_q1, lam_k1, lam_q2, lam_k2, g_sub, g_hgrn, w_out, g_ffn, w_router, router_bias, w_exp_gate, w_exp_up, w_exp_down, w_sh_gate, w_sh_up, w_sh_down):
    batch, seq, d = x.shape
    n = batch * seq
    layer = 0
    x2d = x.reshape(n, d)

    c_pad = jnp.zeros((8, d), F32).at[:batch].set(c.astype(F32))
    mod = _ada(c_pad, w_ada[layer], b_ada[layer][None, :])[:batch]
    sh1, sc1, gt1, sh2, sc2, gt2 = [m.reshape(batch, 1, d) for m in jnp.split(mod, 6, axis=-1)]

    lbs = jnp.cumsum(jax.nn.softmax(hgrn_lb_logits.astype(F32), axis=1), axis=1)[:, layer]
    lbs = lbs.reshape(2, 1, SEG)
    reps = SEG // QK_DIM
    qk_gain = jnp.stack([jnp.tile(g_q[layer].astype(F32), reps) * (QK_DIM ** -0.5 * LOG2E),
                         jnp.tile(g_k[layer].astype(F32), reps)]).reshape(2, 1, SEG)
    lane = jnp.arange(LANES)
    g64 = jnp.where((lane[:, None] // QK_DIM) == (lane[None, :] // QK_DIM), 1.0 / QK_DIM, 0.0).astype(BF16)
    lam = (jnp.exp(jnp.sum(lam_q1[layer].astype(F32) * lam_k1[layer].astype(F32)))
           - jnp.exp(jnp.sum(lam_q2[layer].astype(F32) * lam_k2[layer].astype(F32)))
           + LAM_INIT).reshape(1)

    p, lf = _inproj(x2d, sc1, sh1, g_mix[layer][None, :], w_in[layer].astype(BF16), qk_gain, lbs, g64, seq)

    nt = seq // T_ATT
    pos_sub = positions.astype(I32).reshape(batch * seq // T_SUB, T_SUB)
    smin = jnp.min(pos_sub, axis=1)
    smax = jnp.max(pos_sub, axis=1)
    posq = positions.astype(I32).reshape(batch, nt, 1, T_ATT)
    posk = positions.astype(I32).reshape(batch, seq, 1)
    rb_t = rel_bias.astype(F32).T * LOG2E
    rb_tab = jnp.zeros((N_HEADS, LANES), F32).at[:, :REL_BUCKETS].set(rb_t)
    oa = _attention(p, smin, smax, posq, posk, rb_tab, rb_t, lam,
                    g_sub[layer][:, None].astype(F32), batch, seq)

    ci = jnp.arange(C_HGRN)
    tril = (ci[None, :] <= ci[:, None]).astype(BF16)
    triu = (ci[None, :] >= ci[:, None]).astype(BF16)
    oh = _hgrn(p, lf, tril, triu, g_hgrn[layer][None, :].astype(F32), batch, seq)

    half = d // 2
    w_out_bf = w_out[layer].astype(BF16)
    wr_t = w_router[layer].astype(F32).T
    wr_hi = wr_t.astype(BF16)
    wr_lo = (wr_t - wr_hi.astype(F32)).astype(BF16)
    ti = jnp.arange(TM_OUT)
    upper = (ti[:, None] < ti[None, :]).astype(BF16)
    ones = jnp.ones((TM_OUT, LANES), BF16)
    x1, hp, eidx, slot, gate, cnt = _outproj(
        oa, oh, w_out_bf, x2d, gt1, g_ffn[layer][None, :], sc2, sh2,
        wr_hi, wr_lo, router_bias[layer].astype(F32)[:, None], upper, ones, seq)

    tb = TB_EXP
    counts = cnt[:, 0].astype(I32)
    padded = (counts + tb - 1) // tb * tb
    pends = jnp.cumsum(padded)
    pstart = (pends - padded).astype(I32)
    n_blocks = (n * TOP_K) // tb + N_EXPERTS
    n_valid = (pends[-1] // tb).astype(I32).reshape(1)
    blk_start = jnp.arange(n_blocks, dtype=I32) * tb
    block_e = jnp.minimum(jnp.sum(pends[None, :] <= blk_start[:, None], axis=1), N_EXPERTS - 1).astype(I32)
    pfill = (pstart + counts).astype(I32)
    pend = pends.astype(I32)
    dest = _dest(pstart, eidx, slot)
    has_rows = counts > 0
    group_e = jnp.nonzero(has_rows, size=N_EXPERTS, fill_value=0)[0].astype(I32)
    n_groups = jnp.sum(has_rows).astype(I32).reshape(1)

    xs = _scatter(pfill, pend, n_valid, dest, hp, n_blocks * tb)
    y = _experts(block_e, n_valid, group_e, n_groups, xs,
                 w_exp_gate[layer], w_exp_up[layer], w_exp_down[layer], n_blocks)
    out = _combine(dest, x1, hp, gate.T, gt2,
                   w_sh_gate[layer].astype(BF16), w_sh_up[layer].astype(BF16),
                   w_sh_down[layer].astype(BF16), y, seq)
    return out.reshape(batch, seq, d)
```
